```python
import jax
import jax.numpy as jnp
from jax import lax
import numpy as np

D_MODEL = 1024
BATCH = 8
SEQ = 4096
DEPTH = 4

CTX_LEN = 256
GRID_W = 64
HEAD_DIM = 64
ROPE_HALF = HEAD_DIM // 2
ROPE_THETA = 10000.0
ATTN_HEADS = 8
ATTN_KV_HEADS = 2
ATTN_GROUP = ATTN_HEADS // ATTN_KV_HEADS
ATTN_WIDTH = ATTN_HEADS * HEAD_DIM
KV_WIDTH = ATTN_KV_HEADS * HEAD_DIM
ATTN_IN = ATTN_WIDTH + 2 * KV_WIDTH
ATTN_SCALE = HEAD_DIM ** -0.5
Q_BLOCK = 128
RWKV_HEADS = 8
RWKV_WIDTH = RWKV_HEADS * HEAD_DIM
DECAY_LORA = 64
ICLR_LORA = 64
GATE_LORA = 128
SHIFT_WIDTH = 3 * RWKV_WIDTH + 2 * DECAY_LORA + 2 * ICLR_LORA + GATE_LORA
IN_WIDTH = ATTN_IN + SHIFT_WIDTH
MIX_WIDTH = ATTN_WIDTH + RWKV_WIDTH
D_FF_DENSE = 2816
D_FF_EXPERT = 3584
N_EXPERTS = 8
TOP_K = 2
MOE_BLOCK = 128
N_DENSE = (DEPTH + 1) // 2
N_MOE = DEPTH // 2
N_MOD = 6
EPS = 1e-6
GN_EPS = 64e-5
F32 = jnp.float32

kernel_name = 'hymba_gqa_rwkv7_moe_prefix_dit'


def rmsnorm(x, g):
    xf = x.astype(F32)
    y = xf * lax.rsqrt(jnp.mean(xf * xf, axis=-1, keepdims=True) + EPS)
    return (y * g.astype(F32)).astype(x.dtype)


def axial_rope(rows):
    row = jnp.repeat(jnp.arange(rows, dtype=F32), GRID_W)
    col = jnp.tile(jnp.arange(GRID_W, dtype=F32), rows)
    inv = ROPE_THETA ** (-jnp.arange(0, ROPE_HALF, 2, dtype=F32) / ROPE_HALF)
    ang = jnp.concatenate([row[:, None] * inv, col[:, None] * inv], axis=-1)
    return jnp.cos(ang), jnp.sin(ang)


def apply_rope(x, cos, sin):
    shape = (1, x.shape[1]) + (1,) * (x.ndim - 3) + (ROPE_HALF,)
    c, s = cos.reshape(shape), sin.reshape(shape)
    xf = x.astype(F32)
    x1, x2 = xf[..., :ROPE_HALF], xf[..., ROPE_HALF:]
    return jnp.concatenate([x1 * c - x2 * s, x1 * s + x2 * c], axis=-1).astype(x.dtype)


def attn_heads(p):
    B, L = p.shape[:2]
    q = p[..., :ATTN_WIDTH].reshape(B, L, ATTN_KV_HEADS, ATTN_GROUP, HEAD_DIM)
    k = p[..., ATTN_WIDTH:ATTN_WIDTH + KV_WIDTH].reshape(B, L, ATTN_KV_HEADS, HEAD_DIM)
    v = p[..., ATTN_WIDTH + KV_WIDTH:ATTN_IN].reshape(B, L, ATTN_KV_HEADS, HEAD_DIM)
    return q, k, v


def attend(q, k, v):
    s = jnp.einsum('bqkgd,bskd->bkgqs', q, k, preferred_element_type=F32) * ATTN_SCALE
    p = jax.nn.softmax(s, axis=-1).astype(v.dtype)
    return jnp.einsum('bkgqs,bskd->bqkgd', p, v)


def blocked_attention(q, k, v):
    B, L = q.shape[:2]
    qb = q.reshape((B, L // Q_BLOCK, Q_BLOCK) + q.shape[2:]).swapaxes(0, 1)
    ob = lax.map(lambda qq: attend(qq, k, v), qb)
    return ob.swapaxes(0, 1).reshape(q.shape)


def token_shift(p, mu):
    zero = jnp.zeros_like(p[:, :1])
    prev = jnp.concatenate([zero, p[:, :-1]], axis=1)
    nxt = jnp.concatenate([p[:, 1:], zero], axis=1)
    return p + mu * (0.5 * (prev + nxt) - p)


def rwkv_streams(p, lp):
    p = p.astype(F32)
    B, L = p.shape[:2]
    W = RWKV_WIDTH
    heads = lambda t: t.reshape(B, L, RWKV_HEADS, HEAD_DIM)
    r, k, v = p[..., :W], p[..., W:2 * W], p[..., 2 * W:3 * W]
    o = 3 * W
    wd = [p[..., o + d * DECAY_LORA:o + (d + 1) * DECAY_LORA] for d in range(2)]
    o += 2 * DECAY_LORA
    ad = [p[..., o + d * ICLR_LORA:o + (d + 1) * ICLR_LORA] for d in range(2)]
    gd = p[..., o + 2 * ICLR_LORA:]
    kk = heads(k * lp['kk'])
    kk = kk * lax.rsqrt(jnp.sum(kk * kk, axis=-1, keepdims=True) + 1e-12)
    dirs = []
    for d in range(2):
        w_log = -jax.nn.softplus(-(lp['w0'][d] + jnp.tanh(wd[d]) @ lp['w2'][d])) - 0.5
        a = jax.nn.sigmoid(lp['a0'][d] + ad[d] @ lp['a2'][d])
        k_d = k * (1.0 + (a - 1.0) * lp['ka'])
        dirs.append((heads(jnp.exp(-jnp.exp(w_log))), heads(k_d), heads(a)))
    return heads(r), heads(v), kk, gd, dirs


def rwkv_scan(S0, dirn, v, kk, r, reverse):
    decay, k, a = dirn
    xs = (decay, k, v, kk, kk * a) + (() if r is None else (r,))
    xs = tuple(t.swapaxes(0, 1) for t in xs)

    def step(S, inp):
        w_t, k_t, v_t, kk_t, b_t = inp[:5]
        sa = jnp.einsum('bhvk,bhk->bhv', S, kk_t)
        S = S * w_t[:, :, None, :] - sa[..., None] * b_t[:, :, None, :] + v_t[..., None] * k_t[:, :, None, :]
        out = None if r is None else jnp.einsum('bhvk,bhk->bhv', S, inp[5])
        return S, out

    S, o = lax.scan(step, S0, xs, reverse=reverse)
    return S, (None if r is None else o.swapaxes(0, 1))


def rwkv_readout(o_dirs, r, v, dirs, gd, lp):
    o = o_dirs[0] + o_dirs[1]
    B, L = o.shape[:2]
    mu = jnp.mean(o, axis=-1, keepdims=True)
    var = jnp.mean(jnp.square(o - mu), axis=-1, keepdims=True)
    y = (o - mu) * lax.rsqrt(var + GN_EPS)
    k_sum = dirs[0][1] + dirs[1][1]
    bonus = jnp.sum(r * k_sum * lp['rk'].reshape(RWKV_HEADS, HEAD_DIM), axis=-1, keepdims=True) * v
    y = y.reshape(B, L, RWKV_WIDTH) * lp['gn_w'] + lp['gn_b'] + bonus.reshape(B, L, RWKV_WIDTH)
    g = jax.nn.sigmoid(gd) @ lp['g2']
    return y * g


def mixer(h, hc, cos, sin, lp, ctx_out):
    B, L, _ = h.shape
    Lc = hc.shape[1]
    p = h @ lp['w_in']
    pc = hc @ lp['w_in']
    q, k, v = attn_heads(p)
    q = apply_rope(rmsnorm(q, lp['q_gain']), cos, sin)
    k = apply_rope(rmsnorm(k, lp['k_gain']), cos, sin)
    qc, kc, vc = attn_heads(pc)
    kc = rmsnorm(kc, lp['k_gain'])
    k_all = jnp.concatenate([k, kc], axis=1)
    v_all = jnp.concatenate([v, vc], axis=1)
    att = blocked_attention(q, k_all, v_all).reshape(B, L, ATTN_WIDTH)
    r_l, v_l, kk_l, gd_l, dirs_l = rwkv_streams(token_shift(p[..., ATTN_IN:], lp['mu']), lp)
    r_c, v_c, kk_c, gd_c, dirs_c = rwkv_streams(token_shift(pc[..., ATTN_IN:], lp['mu']), lp)
    S0 = jnp.zeros((B, RWKV_HEADS, HEAD_DIM, HEAD_DIM), F32)
    o_l, o_c = [], []
    for d, rev in enumerate((False, True)):
        S_c, oc = rwkv_scan(S0, dirs_c[d], v_c, kk_c, r_c if ctx_out else None, rev)
        _, ol = rwkv_scan(S_c, dirs_l[d], v_l, kk_l, r_l, rev)
        o_l.append(ol)
        o_c.append(oc)
    rw = rwkv_readout(o_l, r_l, v_l, dirs_l, gd_l, lp).astype(h.dtype)
    out = jnp.concatenate([att, rw], axis=-1) @ lp['w_out']
    if not ctx_out:
        return out, None
    att_c = attend(rmsnorm(qc, lp['q_gain']), kc, vc).reshape(B, Lc, ATTN_WIDTH)
    rw_c = rwkv_readout(o_c, r_c, v_c, dirs_c, gd_c, lp).astype(hc.dtype)
    out_c = jnp.concatenate([att_c, rw_c], axis=-1) @ lp['w_out']
    return out, out_c


def swiglu(h, wg, wu, wd):
    return (jax.nn.silu(h @ wg) * (h @ wu)) @ wd


def moe_ffn(h, w_router, wg, wu, wd):
    T, D = h.shape
    logits = (h @ w_router).astype(F32)
    top_val, top_idx = lax.top_k(logits, TOP_K)
    gates = jax.nn.softmax(top_val, axis=-1)
    n_assign = T * TOP_K
    expert = top_idx.reshape(-1)
    token = jnp.repeat(jnp.arange(T, dtype=jnp.int32), TOP_K)
    gate = gates.reshape(-1)
    order = jnp.argsort(expert)
    e_sorted = expert[order]
    counts = jnp.bincount(expert, length=N_EXPERTS)
    padded = (counts + MOE_BLOCK - 1) // MOE_BLOCK * MOE_BLOCK
    pad_end = jnp.cumsum(padded)
    pad_start = pad_end - padded
    grp_start = jnp.cumsum(counts) - counts
    slot = pad_start[e_sorted] + jnp.arange(n_assign) - grp_start[e_sorted]
    nb = (n_assign + N_EXPERTS * (MOE_BLOCK - 1) + MOE_BLOCK - 1) // MOE_BLOCK
    n_slots = nb * MOE_BLOCK
    slot_token = jnp.full((n_slots,), T, jnp.int32).at[slot].set(token[order])
    slot_gate = jnp.zeros((n_slots,), h.dtype).at[slot].set(gate[order].astype(h.dtype))
    block_expert = jnp.minimum(jnp.searchsorted(pad_end, jnp.arange(nb) * MOE_BLOCK, side='right'), N_EXPERTS - 1)
    h_pad = jnp.concatenate([h, jnp.zeros((1, D), h.dtype)], axis=0)
    xb = h_pad[slot_token].reshape(nb, MOE_BLOCK, D)

    def expert_block(args):
        xblk, e = args
        return swiglu(xblk, wg[e], wu[e], wd[e])

    yb = lax.map(expert_block, (xb, block_expert)).reshape(n_slots, D)
    out = jnp.zeros((T + 1, D), h.dtype).at[slot_token].add(yb * slot_gate[:, None])
    return out[:T]


def setup_inputs(seed: int = 0) -> dict:
    key = jax.random.key(seed)
    keys = jax.random.split(key, 40)
    counter = [0]

    def nxt():
        counter[0] += 1
        return keys[counter[0] - 1]

    def nrm(shape, scale):
        return jax.random.normal(nxt(), shape, F32) * scale

    def gain(shape):
        return 1.0 + nrm(shape, 0.02)

    def unif(shape, lo, hi):
        return jax.random.uniform(nxt(), shape, F32, lo, hi)

    D = D_MODEL
    return {
        'x': nrm((BATCH, SEQ, D), 1.0),
        'c': nrm((BATCH, D), 1.0),
        'ctx': nrm((BATCH, CTX_LEN, D), 1.0),
        'c_ctx': nrm((D,), 1.0),
        'ada_w': nrm((DEPTH, D, N_MOD * D), 0.5 * D ** -0.5),
        'ada_b': nrm((DEPTH, N_MOD * D), 0.02),
        'norm1_g': gain((DEPTH, D)),
        'norm2_g': gain((DEPTH, D)),
        'w_in': nrm((DEPTH, D, IN_WIDTH), D ** -0.5),
        'w_out': nrm((DEPTH, MIX_WIDTH, D), MIX_WIDTH ** -0.5),
        'q_gain': gain((DEPTH, HEAD_DIM)),
        'k_gain': gain((DEPTH, HEAD_DIM)),
        'shift_mu': unif((DEPTH, SHIFT_WIDTH), 0.0, 1.0),
        'rw_w0': unif((DEPTH, 2, RWKV_WIDTH), -4.0, 1.0),
        'rw_w2': nrm((DEPTH, 2, DECAY_LORA, RWKV_WIDTH), 0.1),
        'rw_a0': nrm((DEPTH, 2, RWKV_WIDTH), 0.5),
        'rw_a2': nrm((DEPTH, 2, ICLR_LORA, RWKV_WIDTH), ICLR_LORA ** -0.5),
        'rw_g2': nrm((DEPTH, GATE_LORA, RWKV_WIDTH), GATE_LORA ** -0.5),
        'rw_kk': 0.85 + nrm((DEPTH, RWKV_WIDTH), 0.02),
        'rw_ka': gain((DEPTH, RWKV_WIDTH)),
        'rw_rk': nrm((DEPTH, RWKV_WIDTH), 0.1),
        'rw_gn_w': gain((DEPTH, RWKV_WIDTH)),
        'rw_gn_b': nrm((DEPTH, RWKV_WIDTH), 0.02),
        'ffn_wg': nrm((N_DENSE, D, D_FF_DENSE), D ** -0.5),
        'ffn_wu': nrm((N_DENSE, D, D_FF_DENSE), D ** -0.5),
        'ffn_wd': nrm((N_DENSE, D_FF_DENSE, D), D_FF_DENSE ** -0.5),
        'moe_router': nrm((N_MOE, D, N_EXPERTS), D ** -0.5),
        'moe_wg': nrm((N_MOE, N_EXPERTS, D, D_FF_EXPERT), D ** -0.5),
        'moe_wu': nrm((N_MOE, N_EXPERTS, D, D_FF_EXPERT), D ** -0.5),
        'moe_wd': nrm((N_MOE, N_EXPERTS, D_FF_EXPERT, D), D_FF_EXPERT ** -0.5),
    }


def reference(x, c, ctx, c_ctx, ada_w, ada_b, norm1_g, norm2_g, w_in, w_out, q_gain, k_gain, shift_mu,
              rw_w0, rw_w2, rw_a0, rw_a2, rw_g2, rw_kk, rw_ka, rw_rk, rw_gn_w, rw_gn_b,
              ffn_wg, ffn_wu, ffn_wd, moe_router, moe_wg, moe_wu, moe_wd):
    B, L, D = x.shape
    rows = L // GRID_W
    cos, sin = axial_rope(rows)
    silu_c = jax.nn.silu(c)
    silu_cc = jax.nn.silu(c_ctx)
    xc = ctx
    for i in range(DEPTH):
        last = i == DEPTH - 1
        lp = dict(w_in=w_in[i], w_out=w_out[i], q_gain=q_gain[i], k_gain=k_gain[i], mu=shift_mu[i],
                  w0=rw_w0[i], w2=rw_w2[i], a0=rw_a0[i], a2=rw_a2[i], g2=rw_g2[i], kk=rw_kk[i],
                  ka=rw_ka[i], rk=rw_rk[i], gn_w=rw_gn_w[i], gn_b=rw_gn_b[i])
        m = (silu_c @ ada_w[i] + ada_b[i]).reshape(B, N_MOD, 1, D)
        mc = (silu_cc @ ada_w[i] + ada_b[i]).reshape(N_MOD, D)
        h = rmsnorm(x, norm1_g[i]) * (1.0 + m[:, 1]) + m[:, 0]
        hc = rmsnorm(xc, norm1_g[i]) * (1.0 + mc[1]) + mc[0]
        mix, mix_c = mixer(h, hc, cos, sin, lp, not last)
        x = x + m[:, 2] * mix
        h = rmsnorm(x, norm2_g[i]) * (1.0 + m[:, 4]) + m[:, 3]
        tokens = h.reshape(B * L, D)
        if not last:
            xc = xc + mc[2] * mix_c
            hc = rmsnorm(xc, norm2_g[i]) * (1.0 + mc[4]) + mc[3]
            tokens = jnp.concatenate([tokens, hc.reshape(-1, D)], axis=0)
        j = i // 2
        if i % 2 == 0:
            f = swiglu(tokens, ffn_wg[j], ffn_wu[j], ffn_wd[j])
        else:
            f = moe_ffn(tokens, moe_router[j], moe_wg[j], moe_wu[j], moe_wd[j])
        x = x + m[:, 5] * f[:B * L].reshape(B, L, D)
        if not last:
            xc = xc + mc[5] * f[B * L:].reshape(xc.shape)
    return x
```

```python
import functools
import math

import jax
import jax.numpy as jnp
from jax import lax
from jax.experimental import pallas as pl
from jax.experimental.pallas import tpu as pltpu

F32 = jnp.float32
BF16 = jnp.bfloat16
HI = lax.Precision.HIGHEST

HEAD_DIM = 64
ROPE_HALF = HEAD_DIM // 2
ROPE_THETA = 10000.0
GRID_W = 64
ATTN_HEADS = 8
ATTN_KV_HEADS = 2
ATTN_GROUP = ATTN_HEADS // ATTN_KV_HEADS
ATTN_WIDTH = ATTN_HEADS * HEAD_DIM
KV_WIDTH = ATTN_KV_HEADS * HEAD_DIM
ATTN_IN = ATTN_WIDTH + 2 * KV_WIDTH
ATTN_SCALE = HEAD_DIM ** -0.5
RWKV_HEADS = 8
RWKV_WIDTH = RWKV_HEADS * HEAD_DIM
DECAY_LORA = 64
ICLR_LORA = 64
GATE_LORA = 128
SHIFT_WIDTH = 3 * RWKV_WIDTH + 2 * DECAY_LORA + 2 * ICLR_LORA + GATE_LORA
N_EXPERTS = 8
N_MOD = 6
EPS = 1e-6
GN_EPS = 64e-5

LANES = 128
SUBLANES = 8
TR = 256
CH = 128
PAIR = LANES // HEAD_DIM
MOE_BM = 512
MOE_TF = 896
VMEM_LIMIT = 56 * 1024 * 1024


def _cparams(sem):
    return pltpu.CompilerParams(dimension_semantics=sem, vmem_limit_bytes=VMEM_LIMIT)


def _dot(a, b, prec=None):
    return jnp.dot(a, b, preferred_element_type=F32, precision=prec)


def _dot_nt(a, b, prec=None):
    return lax.dot_general(a, b, (((1,), (1,)), ((), ())), preferred_element_type=F32, precision=prec)


def _dot_tn(a, b, prec=None):
    return lax.dot_general(a, b, (((0,), (0,)), ((), ())), preferred_element_type=F32, precision=prec)


def _sigmoid(x):
    return 1.0 / (1.0 + jnp.exp(-x))


def _group_ones(n, scale=1.0):
    i = jnp.arange(n) // HEAD_DIM
    return jnp.where(i[:, None] == i[None, :], scale, 0.0).astype(F32)


def _ada_kernel(c_ref, w_ref, b_ref, o_ref):
    c = c_ref[...]
    sc = c * _sigmoid(c)
    o_ref[0] = _dot(sc, w_ref[0], HI) + b_ref[0]


def _ada_all(c_all, ada_w, ada_b):
    depth, d, n = ada_w.shape
    rows = c_all.shape[0]
    tn = 1536
    return pl.pallas_call(
        _ada_kernel,
        grid=(depth, n // tn),
        in_specs=[pl.BlockSpec((rows, d), lambda i, j: (0, 0)),
                  pl.BlockSpec((1, d, tn), lambda i, j: (i, 0, j)),
                  pl.BlockSpec((1, 1, tn), lambda i, j: (i, 0, j))],
        out_specs=pl.BlockSpec((1, rows, tn), lambda i, j: (i, 0, j)),
        out_shape=jax.ShapeDtypeStruct((depth, rows, n), F32),
        compiler_params=_cparams(("arbitrary", "arbitrary")),
        name="ada_mod",
    )(c_all, ada_w, ada_b.reshape(depth, 1, n))


def _norm_mod(x, g, shift, scale):
    y = x * lax.rsqrt(jnp.mean(x * x, axis=-1, keepdims=True) + EPS)
    return (y * g) * (1.0 + scale) + shift


def _in_proj_kernel(x_ref, mod_ref, g_ref, wa_ref, wr_ref, pa_ref, pr_ref):
    h = _norm_mod(x_ref[0], g_ref[...], mod_ref[0, 0:1], mod_ref[0, 1:2]).astype(BF16)
    pa_ref[0] = _dot(h, wa_ref[...])
    pr_ref[0] = _dot(h, wr_ref[...])


def _mod_index(nb, nct):
    return lambda b, t: (jnp.where(t < nct, nb, b), 0, 0)


def _in_proj(xa, mod, g, wa, wr, nct):
    nb, s, d = xa.shape
    na, nr = wa.shape[1], wr.shape[1]
    return pl.pallas_call(
        _in_proj_kernel,
        grid=(nb, s // TR),
        in_specs=[pl.BlockSpec((1, TR, d), lambda b, t: (b, t, 0)),
                  pl.BlockSpec((1, N_MOD, d), _mod_index(nb, nct)),
                  pl.BlockSpec((1, d), lambda b, t: (0, 0)),
                  pl.BlockSpec((d, na), lambda b, t: (0, 0)),
                  pl.BlockSpec((d, nr), lambda b, t: (0, 0))],
        out_specs=[pl.BlockSpec((1, TR, na), lambda b, t: (b, t, 0)),
                   pl.BlockSpec((1, TR, nr), lambda b, t: (b, t, 0))],
        out_shape=[jax.ShapeDtypeStruct((nb, s, na), F32),
                   jax.ShapeDtypeStruct((nb, s, nr), F32)],
        compiler_params=_cparams(("arbitrary", "arbitrary")),
        name="in_proj",
    )(xa, mod, g.reshape(1, d), wa, wr)


def _rope(x, cos, sin_lo, sin_hi):
    n = x.shape[-1]
    return x * cos + pltpu.roll(x, n - ROPE_HALF, 1) * sin_lo + pltpu.roll(x, ROPE_HALF, 1) * sin_hi


def _attn_prep_kernel(pa_ref, cos_ref, slo_ref, shi_ref, qg_ref, kg_ref, gm_ref, q_ref, k_ref, v_ref):
    pa = pa_ref[0]
    q = pa[:, :ATTN_WIDTH]
    k = pa[:, ATTN_WIDTH:ATTN_WIDTH + KV_WIDTH]
    v = pa[:, ATTN_WIDTH + KV_WIDTH:ATTN_IN]
    gm = gm_ref[...]
    cos, slo, shi = cos_ref[...], slo_ref[...], shi_ref[...]
    qn = q * lax.rsqrt(_dot(q * q, gm, HI) + EPS) * qg_ref[...]
    qr = _rope(qn, cos, slo, shi) * ATTN_SCALE
    kn = k * lax.rsqrt(_dot(k * k, gm[:KV_WIDTH, :KV_WIDTH], HI) + EPS) * kg_ref[...]
    kr = _rope(kn, cos[:, :KV_WIDTH], slo[:, :KV_WIDTH], shi[:, :KV_WIDTH])
    k_ref[0] = kr.astype(BF16)
    v_ref[0] = v.astype(BF16)
    lane = lax.broadcasted_iota(jnp.int32, (TR, LANES), 1)
    for h in range(ATTN_HEADS):
        src = qr[:, (h // PAIR) * LANES:(h // PAIR + 1) * LANES]
        kv = h // ATTN_GROUP
        if h % PAIR != kv:
            src = pltpu.roll(src, HEAD_DIM, 1)
        keep = (lane // HEAD_DIM) == kv
        q_ref[0, h] = jnp.where(keep, src, 0.0).astype(BF16)


def _attn_prep(pa, cos, slo, shi, q_gain, k_gain):
    nb, s, _ = pa.shape
    gm = _group_ones(ATTN_WIDTH, 1.0 / HEAD_DIM)
    qg = jnp.tile(q_gain, ATTN_HEADS).reshape(1, ATTN_WIDTH)
    kg = jnp.tile(k_gain, ATTN_KV_HEADS).reshape(1, KV_WIDTH)
    tab = pl.BlockSpec((TR, ATTN_WIDTH), lambda t, b: (t, 0))
    return pl.pallas_call(
        _attn_prep_kernel,
        grid=(s // TR, nb),
        in_specs=[pl.BlockSpec((1, TR, ATTN_IN), lambda t, b: (b, t, 0)), tab, tab, tab,
                  pl.BlockSpec((1, ATTN_WIDTH), lambda t, b: (0, 0)),
                  pl.BlockSpec((1, KV_WIDTH), lambda t, b: (0, 0)),
                  pl.BlockSpec((ATTN_WIDTH, ATTN_WIDTH), lambda t, b: (0, 0))],
        out_specs=[pl.BlockSpec((1, ATTN_HEADS, TR, LANES), lambda t, b: (b, 0, t, 0)),
                   pl.BlockSpec((1, TR, KV_WIDTH), lambda t, b: (b, t, 0)),
                   pl.BlockSpec((1, TR, KV_WIDTH), lambda t, b: (b, t, 0))],
        out_shape=[jax.ShapeDtypeStruct((nb, ATTN_HEADS, s, LANES), BF16),
                   jax.ShapeDtypeStruct((nb, s, KV_WIDTH), BF16),
                   jax.ShapeDtypeStruct((nb, s, KV_WIDTH), BF16)],
        compiler_params=_cparams(("arbitrary", "arbitrary")),
        name="attn_prep",
    )(pa, cos, slo, shi, qg, kg, gm)


def _rope_tables(lc, l):
    rows = l // GRID_W
    row = jnp.repeat(jnp.arange(rows, dtype=F32), GRID_W)
    col = jnp.tile(jnp.arange(GRID_W, dtype=F32), rows)
    inv = ROPE_THETA ** (-jnp.arange(0, ROPE_HALF, 2, dtype=F32) / ROPE_HALF)
    ang = jnp.concatenate([row[:, None] * inv, col[:, None] * inv], axis=-1)
    cos = jnp.concatenate([jnp.ones((lc, ROPE_HALF), F32), jnp.cos(ang)], axis=0)
    sin = jnp.concatenate([jnp.zeros((lc, ROPE_HALF), F32), jnp.sin(ang)], axis=0)
    zero = jnp.zeros_like(sin)
    head = lambda lo, hi: jnp.tile(jnp.concatenate([lo, hi], axis=-1), (1, ATTN_HEADS))
    return head(cos, cos), head(-sin, zero), head(zero, sin)


def _attn_kernel(q_ref, k_ref, v_ref, o_ref, acc_ref, *, nct, lc):
    kv = pl.program_id(1)
    t = pl.program_id(2)

    def heads(k, v):
        def body(g, carry):
            s = _dot_nt(q_ref[0, g], k)
            m = jnp.max(s, axis=-1, keepdims=True)
            p = jnp.exp(s - m)
            l = jnp.sum(p, axis=-1, keepdims=True)
            acc_ref[g] = _dot(p.astype(BF16), v) / l
            return carry
        lax.fori_loop(0, ATTN_GROUP, body, 0)

    @pl.when(t < nct)
    def _():
        heads(k_ref[0, :lc, :], v_ref[0, :lc, :])

    @pl.when(t >= nct)
    def _():
        heads(k_ref[0], v_ref[0])

    lane = lax.broadcasted_iota(jnp.int32, (TR, LANES), 1)
    low = lane < HEAD_DIM
    for pr in range(ATTN_GROUP // PAIR):
        a, b = acc_ref[PAIR * pr], acc_ref[PAIR * pr + 1]
        a_lo = jnp.where(kv == 0, a, pltpu.roll(a, HEAD_DIM, 1))
        b_hi = jnp.where(kv == 0, pltpu.roll(b, HEAD_DIM, 1), b)
        o_ref[0, :, pr * LANES:(pr + 1) * LANES] = jnp.where(low, a_lo, b_hi).astype(BF16)


def _attention(q, k, v, nct, lc):
    nb, _, s, _ = q.shape
    gw = ATTN_GROUP * HEAD_DIM
    return pl.pallas_call(
        functools.partial(_attn_kernel, nct=nct, lc=lc),
        grid=(nb, ATTN_KV_HEADS, s // TR),
        in_specs=[pl.BlockSpec((1, ATTN_GROUP, TR, LANES), lambda b, j, t: (b, j, t, 0)),
                  pl.BlockSpec((1, s, KV_WIDTH), lambda b, j, t: (b, 0, 0)),
                  pl.BlockSpec((1, s, KV_WIDTH), lambda b, j, t: (b, 0, 0))],
        out_specs=pl.BlockSpec((1, TR, gw), lambda b, j, t: (b, t, j)),
        out_shape=jax.ShapeDtypeStruct((nb, s, ATTN_WIDTH), BF16),
        scratch_shapes=[pltpu.VMEM((ATTN_GROUP, TR, LANES), F32)],
        compiler_params=_cparams(("arbitrary", "arbitrary", "arbitrary")),
        name="attention",
    )(q, k, v)


def _token_shift(p, prev_row, next_row, mu):
    n = p.shape[0]
    row = lax.broadcasted_iota(jnp.int32, p.shape, 0)
    prev = jnp.where(row == 0, prev_row, pltpu.roll(p, 1, 0))
    nxt = jnp.where(row == n - 1, next_row, pltpu.roll(p, n - 1, 0))
    return p + mu * (0.5 * (prev + nxt) - p)


def _halo_rows(prev_ref, next_ref, tile, nct, ng):
    first = jnp.logical_or(tile == 0, tile == nct)
    last = jnp.logical_or(tile == nct - 1, tile == ng - 1)
    prev_row = jnp.where(first, 0.0, prev_ref[0, SUBLANES - 1:SUBLANES, :])
    next_row = jnp.where(last, 0.0, next_ref[0, 0:1, :])
    return prev_row, next_row


def _rwkv_tile(d, g, nct, ng):
    fwd = g
    rev = jnp.where(g < nct, nct - 1 - g, ng - 1 - (g - nct))
    return jnp.where(d == 0, fwd, rev)


def _rwkv_dir_kernel(p_ref, prev_ref, next_ref, mu_ref, kkw_ref, ka_ref, rk_ref, w0_ref, a0_ref, w2_ref, a2_ref,
                     gs_ref, o_ref, bv_ref,
                     s_ref, r_s, v_s, kap_s, lw_s, k_s, bet_s, *, nct, ng):
    d = pl.program_id(0)
    g = pl.program_id(2)
    tile = _rwkv_tile(d, g, nct, ng)
    w = RWKV_WIDTH

    @pl.when(g == 0)
    def _():
        s_ref[...] = jnp.zeros_like(s_ref)

    prev_row, next_row = _halo_rows(prev_ref, next_ref, tile, nct, ng)
    ps = _token_shift(p_ref[0], prev_row, next_row, mu_ref[...])
    r, k, v = ps[:, :w], ps[:, w:2 * w], ps[:, 2 * w:3 * w]
    wd = ps[:, 3 * w:3 * w + 2 * DECAY_LORA]
    ad = ps[:, 3 * w + 2 * DECAY_LORA:3 * w + 2 * DECAY_LORA + 2 * ICLR_LORA]
    gs = gs_ref[...]
    kk = k * kkw_ref[...]
    kk = kk * lax.rsqrt(_dot(kk * kk, gs, HI) + 1e-12)
    x = w0_ref[0] + _dot(jnp.tanh(wd), w2_ref[0], HI)
    lw = (-math.exp(-0.5)) * _sigmoid(x)
    a = _sigmoid(a0_ref[0] + _dot(ad, a2_ref[0], HI))
    k_d = k * (1.0 + (a - 1.0) * ka_ref[...])
    bv_ref[0, 0] = _dot(r * k_d * rk_ref[...], gs, HI) * v
    r_s[...] = r
    v_s[...] = v
    kap_s[...] = kk
    lw_s[...] = lw
    k_s[...] = k_d
    bet_s[...] = kk * a

    ti = lax.broadcasted_iota(jnp.int32, (CH, CH), 0)
    tj = lax.broadcasted_iota(jnp.int32, (CH, CH), 1)
    prec = (ti - tj) * jnp.where(d == 0, 1, -1) > 0
    eye = ti == tj
    incl = jnp.logical_or(prec, eye)
    tri = incl.astype(F32)
    same_head = (ti // HEAD_DIM) == (tj // HEAD_DIM)
    lane = lax.broadcasted_iota(jnp.int32, (CH, LANES), 1)
    head0 = lane < HEAD_DIM
    sel = lambda x0, x1: jnp.where(head0, x0, x1)
    n_chunks = TR // CH

    def chunk(i, carry):
        ci = jnp.where(d == 0, i, n_chunks - 1 - i)
        rows = pl.ds(pl.multiple_of(ci * CH, CH), CH)
        for pp in range(RWKV_HEADS // PAIR):
            cols = slice(pp * LANES, (pp + 1) * LANES)
            rx, vx, kap = r_s[rows, cols], v_s[rows, cols], kap_s[rows, cols]
            lwx, kx, bet = lw_s[rows, cols], k_s[rows, cols], bet_s[rows, cols]
            cumi = _dot(tri, lwx, HI)
            tot = jnp.where(d == 0, cumi[CH - 1:CH, :], cumi[0:1, :])
            cc = cumi - cumi[CH // 2:CH // 2 + 1, :]
            e_neg = jnp.exp(-cc)
            e_end = jnp.exp(tot - cumi)
            kap_t = kap * jnp.exp(cc - lwx)
            r_t = rx * jnp.exp(cc)
            kap_0 = kap * jnp.exp(cumi - lwx)
            r_0 = rx * jnp.exp(cumi)
            k_t = (kx * e_neg).astype(BF16)
            bet_t = (bet * e_neg).astype(BF16)
            k_e = kx * e_end
            bet_e = bet * e_end
            p_end = jnp.exp(tot)
            qs = jnp.concatenate([jnp.where(head0, kap_t, 0.0), jnp.where(head0, 0.0, kap_t),
                                  jnp.where(head0, r_t, 0.0), jnp.where(head0, 0.0, r_t)], axis=0).astype(BF16)
            s_b = _dot_nt(qs, bet_t)
            s_k = _dot_nt(qs, k_t)
            vb = vx.astype(BF16)
            kq_h, z0_h, rb_h, o0_h = [], [], [], []
            for h in range(PAIR):
                a_b = jnp.where(prec, s_b[h * CH:(h + 1) * CH], 0.0)
                a_k = jnp.where(prec, s_k[h * CH:(h + 1) * CH], 0.0)
                b_b = jnp.where(incl, s_b[(PAIR + h) * CH:(PAIR + h + 1) * CH], 0.0)
                b_k = jnp.where(incl, s_k[(PAIR + h) * CH:(PAIR + h + 1) * CH], 0.0)
                xp = -a_b
                t_inv = jnp.where(eye, 1.0, xp)
                for _ in range(int(math.log2(CH)) - 1):
                    xp = _dot(xp, xp)
                    t_inv = t_inv + _dot(t_inv, xp)
                av = _dot(a_k.astype(BF16), vb)
                wz = _dot(t_inv, jnp.concatenate([kap_0, av], axis=1))
                kq_h.append(wz[:, :LANES])
                z0_h.append(wz[:, LANES:])
                rb_h.append(b_b)
                o0_h.append(_dot(b_k.astype(BF16), vb))
            kq = sel(kq_h[0], kq_h[1])
            z0 = sel(z0_h[0], z0_h[1])
            kz = jnp.concatenate([kq, z0], axis=1)
            bz = [_dot(rb_h[h], kz) for h in range(PAIR)]
            r_new = r_0 - sel(bz[0][:, :LANES], bz[1][:, :LANES])
            o0 = sel(o0_h[0], o0_h[1]) - sel(bz[0][:, LANES:], bz[1][:, LANES:])
            m_c = jnp.where(eye, p_end, 0.0) - jnp.where(same_head, _dot_tn(bet_e, kq), 0.0)
            g_c = jnp.where(same_head, _dot_tn(k_e, vx) - _dot_tn(bet_e, z0), 0.0)
            st = s_ref[pp]
            o_ref[0, 0, rows, cols] = _dot(r_new, st, HI) + o0
            s_ref[pp] = _dot(m_c, st, HI) + g_c
        return carry

    lax.fori_loop(0, n_chunks, chunk, 0)


def _rwkv_scan(pr, lp, nct):
    nb, s, pw = pr.shape
    ng = s // TR
    w = RWKV_WIDTH
    hb = TR // SUBLANES
    tile = functools.partial(_rwkv_tile, nct=nct, ng=ng)
    n_row_blocks = s // SUBLANES
    main = pl.BlockSpec((1, TR, pw), lambda d, b, g: (b, tile(d, g), 0))
    prev = pl.BlockSpec((1, SUBLANES, pw), lambda d, b, g: (b, jnp.maximum(tile(d, g) * hb - 1, 0), 0))
    nxt = pl.BlockSpec((1, SUBLANES, pw),
                       lambda d, b, g: (b, jnp.minimum((tile(d, g) + 1) * hb, n_row_blocks - 1), 0))
    vec = lambda n: pl.BlockSpec((1, n), lambda d, b, g: (0, 0))
    dvec = pl.BlockSpec((1, 1, w), lambda d, b, g: (d, 0, 0))
    dmat = pl.BlockSpec((1, 2 * DECAY_LORA, w), lambda d, b, g: (d, 0, 0))
    out = pl.BlockSpec((1, 1, TR, w), lambda d, b, g: (d, b, tile(d, g), 0))
    zeros = jnp.zeros((DECAY_LORA, w), F32)
    w2x = jnp.stack([jnp.concatenate([lp['w2'][0], zeros]), jnp.concatenate([zeros, lp['w2'][1]])])
    a2x = jnp.stack([jnp.concatenate([lp['a2'][0], zeros]), jnp.concatenate([zeros, lp['a2'][1]])])
    tile_buf = pltpu.VMEM((TR, w), F32)
    return pl.pallas_call(
        functools.partial(_rwkv_dir_kernel, nct=nct, ng=ng),
        grid=(2, nb, ng),
        in_specs=[main, prev, nxt, vec(pw), vec(w), vec(w), vec(w), dvec, dvec, dmat, dmat,
                  pl.BlockSpec((w, w), lambda d, b, g: (0, 0))],
        out_specs=[out, out],
        out_shape=[jax.ShapeDtypeStruct((2, nb, s, w), F32), jax.ShapeDtypeStruct((2, nb, s, w), F32)],
        scratch_shapes=[pltpu.VMEM((RWKV_HEADS // PAIR, LANES, LANES), F32)] + [tile_buf] * 6,
        compiler_params=_cparams(("arbitrary", "arbitrary", "arbitrary")),
        name="rwkv_scan",
    )(pr, pr, pr, lp['mu'].reshape(1, pw), lp['kk'].reshape(1, w), lp['ka'].reshape(1, w), lp['rk'].reshape(1, w),
      lp['w0'].reshape(2, 1, w), lp['a0'].reshape(2, 1, w), w2x, a2x, _group_ones(w))


def _rwkv_out_kernel(of_ref, ob_ref, bf_ref, bb_ref, gd_ref, gprev_ref, gnext_ref, mu_ref, g2_ref, gw_ref, gb_ref,
                     gm_ref, y_ref, *, nct, ng):
    tile = pl.program_id(1)
    o = of_ref[0, 0] + ob_ref[0, 0]
    gm = gm_ref[...]
    mean = _dot(o, gm, HI)
    cen = o - mean
    var = _dot(cen * cen, gm, HI)
    y = cen * lax.rsqrt(var + GN_EPS) * gw_ref[...] + gb_ref[...] + (bf_ref[0, 0] + bb_ref[0, 0])
    prev_row, next_row = _halo_rows(gprev_ref, gnext_ref, tile, nct, ng)
    gd = _token_shift(gd_ref[0], prev_row, next_row, mu_ref[...])
    gate = _dot(_sigmoid(gd), g2_ref[...], HI)
    y_ref[0] = (y * gate).astype(BF16)


def _rwkv_out(o, bv, pr, lp, nct):
    _, nb, s, w = o.shape
    ng = s // TR
    hb = TR // SUBLANES
    gcol = (SHIFT_WIDTH - GATE_LORA) // GATE_LORA
    n_row_blocks = s // SUBLANES
    dirspec = lambda d: pl.BlockSpec((1, 1, TR, w), lambda b, t: (d, b, t, 0))
    vec = lambda n: pl.BlockSpec((1, n), lambda b, t: (0, 0))
    return pl.pallas_call(
        functools.partial(_rwkv_out_kernel, nct=nct, ng=ng),
        grid=(nb, ng),
        in_specs=[dirspec(0), dirspec(1), dirspec(0), dirspec(1),
                  pl.BlockSpec((1, TR, GATE_LORA), lambda b, t: (b, t, gcol)),
                  pl.BlockSpec((1, SUBLANES, GATE_LORA), lambda b, t: (b, jnp.maximum(t * hb - 1, 0), gcol)),
                  pl.BlockSpec((1, SUBLANES, GATE_LORA),
                               lambda b, t: (b, jnp.minimum((t + 1) * hb, n_row_blocks - 1), gcol)),
                  vec(GATE_LORA), pl.BlockSpec((GATE_LORA, w), lambda b, t: (0, 0)), vec(w), vec(w),
                  pl.BlockSpec((w, w), lambda b, t: (0, 0))],
        out_specs=pl.BlockSpec((1, TR, w), lambda b, t: (b, t, 0)),
        out_shape=jax.ShapeDtypeStruct((nb, s, w), BF16),
        compiler_params=_cparams(("arbitrary", "arbitrary")),
        name="rwkv_out",
    )(o, o, bv, bv, pr, pr, pr, lp['mu'][SHIFT_WIDTH - GATE_LORA:].reshape(1, GATE_LORA), lp['g2'],
      lp['gn_w'].reshape(1, w), lp['gn_b'].reshape(1, w), _group_ones(w, 1.0 / HEAD_DIM))


def _top2(logits):
    lane = lax.broadcasted_iota(jnp.int32, logits.shape, 1)
    v1 = jnp.max(logits, axis=-1, keepdims=True)
    i1 = jnp.min(jnp.where(logits == v1, lane, LANES), axis=-1, keepdims=True)
    rest = jnp.where(lane == i1, -jnp.inf, logits)
    v2 = jnp.max(rest, axis=-1, keepdims=True)
    i2 = jnp.min(jnp.where(rest == v2, lane, LANES), axis=-1, keepdims=True)
    e = jnp.exp(v2 - v1)
    g1 = 1.0 / (1.0 + e)
    g2 = e / (1.0 + e)
    out = jnp.where(lane == 0, i1.astype(F32), 0.0)
    out = jnp.where(lane == 1, i2.astype(F32), out)
    out = jnp.where(lane == 2, g1, out)
    return jnp.where(lane == 3, g2, out)


def _out_proj_kernel(x_ref, att_ref, rw_ref, mod_ref, g_ref, wa_ref, wr_ref, *rest, moe):
    if moe:
        wrt_ref, xo_ref, h_ref, route_ref = rest
    else:
        xo_ref, h_ref = rest
    mix = _dot(att_ref[0], wa_ref[...]) + _dot(rw_ref[0], wr_ref[...])
    x = x_ref[0] + mod_ref[0, 2:3] * mix
    xo_ref[0] = x
    h = _norm_mod(x, g_ref[...], mod_ref[0, 3:4], mod_ref[0, 4:5])
    h_ref[0] = h.astype(BF16)
    if moe:
        lane = lax.broadcasted_iota(jnp.int32, (TR, LANES), 1)
        logits = jnp.where(lane < N_EXPERTS, _dot(h, wrt_ref[...], HI), -jnp.inf)
        route_ref[0] = _top2(logits)


def _out_proj(xa, att, rw, mod, g, wo_a, wo_r, nct, w_router=None):
    nb, s, d = xa.shape
    moe = w_router is not None
    row = lambda n: pl.BlockSpec((1, TR, n), lambda b, t: (b, t, 0))
    full = lambda a: pl.BlockSpec(a.shape, lambda b, t: (0, 0))
    in_specs = [row(d), row(ATTN_WIDTH), row(RWKV_WIDTH), pl.BlockSpec((1, N_MOD, d), _mod_index(nb, nct)),
                pl.BlockSpec((1, d), lambda b, t: (0, 0)), full(wo_a), full(wo_r)]
    args = [xa, att, rw, mod, g.reshape(1, d), wo_a, wo_r]
    out_specs = [row(d), row(d)]
    out_shape = [jax.ShapeDtypeStruct((nb, s, d), F32), jax.ShapeDtypeStruct((nb, s, d), BF16)]
    if moe:
        wrt = jnp.pad(w_router, ((0, 0), (0, LANES - N_EXPERTS)))
        in_specs.append(full(wrt))
        args.append(wrt)
        out_specs.append(row(LANES))
        out_shape.append(jax.ShapeDtypeStruct((nb, s, LANES), F32))
    return pl.pallas_call(
        functools.partial(_out_proj_kernel, moe=moe),
        grid=(nb, s // TR),
        in_specs=in_specs, out_specs=out_specs, out_shape=out_shape,
        compiler_params=_cparams(("arbitrary", "arbitrary")),
        name="out_proj_moe" if moe else "out_proj",
    )(*args)


def _ffn_kernel(x_ref, h_ref, mod_ref, wg_ref, wu_ref, wd_ref, o_ref):
    h = h_ref[0]
    gt = _dot(h, wg_ref[...])
    up = _dot(h, wu_ref[...])
    act = (gt * _sigmoid(gt) * up).astype(BF16)
    o_ref[0] = x_ref[0] + mod_ref[0, 5:6] * _dot(act, wd_ref[...])


def _ffn_dense(xa, h, mod, wg, wu, wd, nct):
    nb, s, d = xa.shape
    row = pl.BlockSpec((1, TR, d), lambda b, t: (b, t, 0))
    resident = lambda a: pl.BlockSpec(a.shape, lambda b, t: (0, 0), pipeline_mode=pl.Buffered(1))
    return pl.pallas_call(
        _ffn_kernel,
        grid=(nb, s // TR),
        in_specs=[row, row, pl.BlockSpec((1, N_MOD, d), _mod_index(nb, nct)), resident(wg), resident(wu),
                  resident(wd)],
        out_specs=row,
        out_shape=jax.ShapeDtypeStruct((nb, s, d), F32),
        compiler_params=_cparams(("arbitrary", "arbitrary")),
        name="ffn_dense",
    )(xa, h, mod, wg, wu, wd)


def _moe_kernel(be_ref, nu_ref, x_ref, wg_ref, wu_ref, wd_ref, o_ref):
    i = pl.program_id(0)
    j = pl.program_id(1)

    @pl.when(j == 0)
    def _():
        o_ref[...] = jnp.zeros_like(o_ref)

    @pl.when(i < nu_ref[0])
    def _():
        x = x_ref[...]
        gt = _dot(x, wg_ref[0])
        up = _dot(x, wu_ref[0])
        act = (gt * _sigmoid(gt) * up).astype(BF16)
        o_ref[...] += _dot(act, wd_ref[0])


def _moe_experts(xs, block_expert, n_used, wg, wu, wd):
    n_slots, d = xs.shape
    dff = wg.shape[2]
    nblk = n_slots // MOE_BM
    grid_spec = pltpu.PrefetchScalarGridSpec(
        num_scalar_prefetch=2,
        grid=(nblk, dff // MOE_TF),
        in_specs=[pl.BlockSpec((MOE_BM, d), lambda i, j, be, nu: (i, 0)),
                  pl.BlockSpec((1, d, MOE_TF), lambda i, j, be, nu: (be[i], 0, j)),
                  pl.BlockSpec((1, d, MOE_TF), lambda i, j, be, nu: (be[i], 0, j)),
                  pl.BlockSpec((1, MOE_TF, d), lambda i, j, be, nu: (be[i], j, 0))],
        out_specs=pl.BlockSpec((MOE_BM, d), lambda i, j, be, nu: (i, 0)),
    )
    return pl.pallas_call(
        _moe_kernel,
        grid_spec=grid_spec,
        out_shape=jax.ShapeDtypeStruct((n_slots, d), F32),
        compiler_params=_cparams(("arbitrary", "arbitrary")),
        name="moe_experts",
    )(block_expert, n_used, xs, wg, wu, wd)


def _moe_ffn(xa, h, route, gate_row, wg, wu, wd):
    nb, s, d = xa.shape
    t = nb * s
    n_assign = 2 * t
    nblk = (n_assign + N_EXPERTS * (MOE_BM - 1) + MOE_BM - 1) // MOE_BM
    n_slots = nblk * MOE_BM
    route = route.reshape(t, LANES)
    expert = route[:, 0:2].astype(jnp.int32).reshape(-1)
    gates = route[:, 2:4]
    onehot = (expert[:, None] == jnp.arange(N_EXPERTS, dtype=jnp.int32)[None, :]).astype(jnp.int32)
    rank = jnp.sum((jnp.cumsum(onehot, axis=0) - onehot) * onehot, axis=1)
    counts = jnp.sum(onehot, axis=0)
    padded = (counts + MOE_BM - 1) // MOE_BM * MOE_BM
    pad_end = jnp.cumsum(padded)
    pad_start = pad_end - padded
    slot = pad_start[expert] + rank
    token = jnp.repeat(jnp.arange(t, dtype=jnp.int32), 2)
    slot_token = jnp.full((n_slots,), t, jnp.int32).at[slot].set(token)
    block_expert = jnp.minimum(
        jnp.searchsorted(pad_end, jnp.arange(nblk, dtype=jnp.int32) * MOE_BM, side='right'), N_EXPERTS - 1
    ).astype(jnp.int32)
    n_used = (pad_end[-1] // MOE_BM).astype(jnp.int32).reshape(1)
    h_pad = jnp.concatenate([h.reshape(t, d), jnp.zeros((1, d), h.dtype)], axis=0)
    xs = h_pad[slot_token]
    ys = _moe_experts(xs, block_expert, n_used, wg, wu, wd)
    slot2 = slot.reshape(t, 2)
    f = ys[slot2[:, 0]] * gates[:, 0:1] + ys[slot2[:, 1]] * gates[:, 1:2]
    return xa + gate_row * f.reshape(nb, s, d)


def kernel(x, c, ctx, c_ctx, ada_w, ada_b, norm1_g, norm2_g, w_in, w_out, q_gain, k_gain, shift_mu, rw_w0, rw_w2,
           rw_a0, rw_a2, rw_g2, rw_kk, rw_ka, rw_rk, rw_gn_w, rw_gn_b, ffn_wg, ffn_wu, ffn_wd, moe_router, moe_wg,
           moe_wu, moe_wd):
    nb, l, d = x.shape
    lc = ctx.shape[1]
    depth = ada_w.shape[0]
    assert lc % TR == 0 and l % TR == 0 and l % GRID_W == 0
    nct = lc // TR
    xa = jnp.concatenate([ctx, x], axis=1)

    mod_rows = -(-(nb + 1) // SUBLANES) * SUBLANES
    c_all = jnp.concatenate([c, c_ctx[None], jnp.zeros((mod_rows - nb - 1, d), F32)], axis=0)
    mod_all = _ada_all(c_all, ada_w, ada_b)[:, :nb + 1].reshape(depth, nb + 1, N_MOD, d)
    cos, slo, shi = _rope_tables(lc, l)

    for i in range(depth):
        mod = mod_all[i]
        lp = dict(mu=shift_mu[i], w0=rw_w0[i], w2=rw_w2[i], a0=rw_a0[i], a2=rw_a2[i], g2=rw_g2[i], kk=rw_kk[i],
                  ka=rw_ka[i], rk=rw_rk[i], gn_w=rw_gn_w[i], gn_b=rw_gn_b[i])
        wi = w_in[i].astype(BF16)
        pa, pr = _in_proj(xa, mod, norm1_g[i], wi[:, :ATTN_IN], wi[:, ATTN_IN:], nct)
        q, k, v = _attn_prep(pa, cos, slo, shi, q_gain[i], k_gain[i])
        att = _attention(q, k, v, nct, lc)
        o, bv = _rwkv_scan(pr, lp, nct)
        rw = _rwkv_out(o, bv, pr, lp, nct)
        wo = w_out[i].astype(BF16)
        j = i // 2
        if i % 2 == 0:
            xa, h = _out_proj(xa, att, rw, mod, norm2_g[i], wo[:ATTN_WIDTH], wo[ATTN_WIDTH:], nct)
            xa = _ffn_dense(xa, h, mod, ffn_wg[j].astype(BF16), ffn_wu[j].astype(BF16), ffn_wd[j].astype(BF16), nct)
        else:
            xa, h, route = _out_proj(xa, att, rw, mod, norm2_g[i], wo[:ATTN_WIDTH], wo[ATTN_WIDTH:], nct,
                                     moe_router[j])
            gate_row = jnp.concatenate(
                [jnp.broadcast_to(mod[nb, 5][None, None], (nb, lc, d)),
                 jnp.broadcast_to(mod[:nb, 5][:, None], (nb, l, d))], axis=1)
            xa = _moe_ffn(xa, h, route, gate_row, moe_wg[j].astype(BF16), moe_wu[j].astype(BF16),
                          moe_wd[j].astype(BF16))
    return xa[:, lc:]
```

```python
import functools
import math

import jax
import jax.numpy as jnp
from jax import lax
from jax.experimental import pallas as pl
from jax.experimental.pallas import tpu as pltpu

F32 = jnp.float32
BF16 = jnp.bfloat16
HI = lax.Precision.HIGHEST

HEAD_DIM = 64
ROPE_HALF = HEAD_DIM // 2
ROPE_THETA = 10000.0
GRID_W = 64
ATTN_HEADS = 8
ATTN_KV_HEADS = 2
ATTN_GROUP = ATTN_HEADS // ATTN_KV_HEADS
ATTN_WIDTH = ATTN_HEADS * HEAD_DIM
KV_WIDTH = ATTN_KV_HEADS * HEAD_DIM
ATTN_IN = ATTN_WIDTH + 2 * KV_WIDTH
ATTN_SCALE = HEAD_DIM ** -0.5
RWKV_HEADS = 8
RWKV_WIDTH = RWKV_HEADS * HEAD_DIM
DECAY_LORA = 64
ICLR_LORA = 64
GATE_LORA = 128
SHIFT_WIDTH = 3 * RWKV_WIDTH + 2 * DECAY_LORA + 2 * ICLR_LORA + GATE_LORA
N_EXPERTS = 8
N_MOD = 6
EPS = 1e-6
GN_EPS = 64e-5

LANES = 128
SUBLANES = 8
TR = 256
CH = 128
PAIR = LANES // HEAD_DIM
MOE_BM = 512
MOE_TF = 896
VMEM_LIMIT = 56 * 1024 * 1024


def _cparams(sem):
    return pltpu.CompilerParams(dimension_semantics=sem, vmem_limit_bytes=VMEM_LIMIT)


def _dot(a, b, prec=None):
    return jnp.dot(a, b, preferred_element_type=F32, precision=prec)


def _dot_nt(a, b, prec=None):
    return lax.dot_general(a, b, (((1,), (1,)), ((), ())), preferred_element_type=F32, precision=prec)


def _dot_tn(a, b, prec=None):
    return lax.dot_general(a, b, (((0,), (0,)), ((), ())), preferred_element_type=F32, precision=prec)


def _sigmoid(x):
    return 1.0 / (1.0 + jnp.exp(-x))


def _split2(x):
    hi = x.astype(BF16)
    return hi, (x - hi.astype(F32)).astype(BF16)


def _split3(x):
    hi = x.astype(BF16)
    r = x - hi.astype(F32)
    mid = r.astype(BF16)
    return hi, mid, (r - mid.astype(F32)).astype(BF16)


def _dot3(a, b):
    a_hi, a_lo = _split2(a)
    b_hi, b_lo = _split2(b)
    return _dot(a_hi, b_hi) + (_dot(a_hi, b_lo) + _dot(a_lo, b_hi))


def _head_sums(x, scale=1.0):
    r, n = x.shape
    nblk = n // LANES
    xs = jnp.concatenate([x[:, c * LANES:(c + 1) * LANES] for c in range(nblk)], axis=0)
    i = lax.broadcasted_iota(jnp.int32, (LANES, LANES), 0) // HEAD_DIM
    j = lax.broadcasted_iota(jnp.int32, (LANES, LANES), 1) // HEAD_DIM
    ones = jnp.where(i == j, scale, 0.0).astype(BF16)
    m = nblk * r
    s = _dot(jnp.concatenate(_split3(xs), axis=0), ones)
    s = s[:m] + s[m:2 * m] + s[2 * m:]
    return jnp.concatenate([s[c * r:(c + 1) * r] for c in range(nblk)], axis=1)


def _ada_kernel(c_ref, w_ref, b_ref, o_ref):
    c = c_ref[...]
    sc = c * _sigmoid(c)
    o_ref[0] = _dot(sc, w_ref[0], HI) + b_ref[0]


def _ada_all(c_all, ada_w, ada_b):
    depth, d, n = ada_w.shape
    rows = c_all.shape[0]
    tn = 1536
    return pl.pallas_call(
        _ada_kernel,
        grid=(depth, n // tn),
        in_specs=[pl.BlockSpec((rows, d), lambda i, j: (0, 0)),
                  pl.BlockSpec((1, d, tn), lambda i, j: (i, 0, j)),
                  pl.BlockSpec((1, 1, tn), lambda i, j: (i, 0, j))],
        out_specs=pl.BlockSpec((1, rows, tn), lambda i, j: (i, 0, j)),
        out_shape=jax.ShapeDtypeStruct((depth, rows, n), F32),
        compiler_params=_cparams(("arbitrary", "arbitrary")),
        name="ada_mod",
    )(c_all, ada_w, ada_b.reshape(depth, 1, n))


def _norm_mod(x, g, shift, scale):
    y = x * lax.rsqrt(jnp.mean(x * x, axis=-1, keepdims=True) + EPS)
    return (y * g) * (1.0 + scale) + shift


def _in_proj_kernel(x_ref, mod_ref, g_ref, wa_ref, wr_ref, pa_ref, pr_ref):
    h = _norm_mod(x_ref[0], g_ref[...], mod_ref[0, 0:1], mod_ref[0, 1:2]).astype(BF16)
    pa_ref[0] = _dot(h, wa_ref[...])
    pr_ref[0] = _dot(h, wr_ref[...])


def _mod_index(nb, nct):
    return lambda b, t: (jnp.where(t < nct, nb, b), 0, 0)


def _in_proj(xa, mod, g, wa, wr, nct):
    nb, s, d = xa.shape
    na, nr = wa.shape[1], wr.shape[1]
    return pl.pallas_call(
        _in_proj_kernel,
        grid=(nb, s // TR),
        in_specs=[pl.BlockSpec((1, TR, d), lambda b, t: (b, t, 0)),
                  pl.BlockSpec((1, N_MOD, d), _mod_index(nb, nct)),
                  pl.BlockSpec((1, d), lambda b, t: (0, 0)),
                  pl.BlockSpec((d, na), lambda b, t: (0, 0)),
                  pl.BlockSpec((d, nr), lambda b, t: (0, 0))],
        out_specs=[pl.BlockSpec((1, TR, na), lambda b, t: (b, t, 0)),
                   pl.BlockSpec((1, TR, nr), lambda b, t: (b, t, 0))],
        out_shape=[jax.ShapeDtypeStruct((nb, s, na), F32),
                   jax.ShapeDtypeStruct((nb, s, nr), F32)],
        compiler_params=_cparams(("arbitrary", "arbitrary")),
        name="in_proj",
    )(xa, mod, g.reshape(1, d), wa, wr)


def _rope(x, cos, sin_lo, sin_hi):
    n = x.shape[-1]
    return x * cos + pltpu.roll(x, n - ROPE_HALF, 1) * sin_lo + pltpu.roll(x, ROPE_HALF, 1) * sin_hi


def _attn_prep_kernel(pa_ref, cos_ref, slo_ref, shi_ref, qg_ref, kg_ref, q_ref, k_ref, v_ref):
    pa = pa_ref[0]
    qk = pa[:, :ATTN_WIDTH + KV_WIDTH]
    v = pa[:, ATTN_WIDTH + KV_WIDTH:ATTN_IN]
    cos, slo, shi = cos_ref[...], slo_ref[...], shi_ref[...]
    inv = lax.rsqrt(_head_sums(qk * qk, 1.0 / HEAD_DIM) + EPS)
    qn = qk[:, :ATTN_WIDTH] * inv[:, :ATTN_WIDTH] * qg_ref[...]
    qr = _rope(qn, cos, slo, shi) * ATTN_SCALE
    kn = qk[:, ATTN_WIDTH:] * inv[:, ATTN_WIDTH:] * kg_ref[...]
    kr = _rope(kn, cos[:, :KV_WIDTH], slo[:, :KV_WIDTH], shi[:, :KV_WIDTH])
    k_ref[0] = kr.astype(BF16)
    v_ref[0] = v.astype(BF16)
    lane = lax.broadcasted_iota(jnp.int32, (TR, LANES), 1)
    for h in range(ATTN_HEADS):
        src = qr[:, (h // PAIR) * LANES:(h // PAIR + 1) * LANES]
        kv = h // ATTN_GROUP
        if h % PAIR != kv:
            src = pltpu.roll(src, HEAD_DIM, 1)
        keep = (lane // HEAD_DIM) == kv
        q_ref[0, h] = jnp.where(keep, src, 0.0).astype(BF16)


def _attn_prep(pa, cos, slo, shi, q_gain, k_gain):
    nb, s, _ = pa.shape
    qg = jnp.tile(q_gain, ATTN_HEADS).reshape(1, ATTN_WIDTH)
    kg = jnp.tile(k_gain, ATTN_KV_HEADS).reshape(1, KV_WIDTH)
    tab = pl.BlockSpec((TR, ATTN_WIDTH), lambda t, b: (t, 0))
    return pl.pallas_call(
        _attn_prep_kernel,
        grid=(s // TR, nb),
        in_specs=[pl.BlockSpec((1, TR, ATTN_IN), lambda t, b: (b, t, 0)), tab, tab, tab,
                  pl.BlockSpec((1, ATTN_WIDTH), lambda t, b: (0, 0)),
                  pl.BlockSpec((1, KV_WIDTH), lambda t, b: (0, 0))],
        out_specs=[pl.BlockSpec((1, ATTN_HEADS, TR, LANES), lambda t, b: (b, 0, t, 0)),
                   pl.BlockSpec((1, TR, KV_WIDTH), lambda t, b: (b, t, 0)),
                   pl.BlockSpec((1, TR, KV_WIDTH), lambda t, b: (b, t, 0))],
        out_shape=[jax.ShapeDtypeStruct((nb, ATTN_HEADS, s, LANES), BF16),
                   jax.ShapeDtypeStruct((nb, s, KV_WIDTH), BF16),
                   jax.ShapeDtypeStruct((nb, s, KV_WIDTH), BF16)],
        compiler_params=_cparams(("arbitrary", "arbitrary")),
        name="attn_prep",
    )(pa, cos, slo, shi, qg, kg)


def _rope_tables(lc, l):
    rows = l // GRID_W
    row = jnp.repeat(jnp.arange(rows, dtype=F32), GRID_W)
    col = jnp.tile(jnp.arange(GRID_W, dtype=F32), rows)
    inv = ROPE_THETA ** (-jnp.arange(0, ROPE_HALF, 2, dtype=F32) / ROPE_HALF)
    ang = jnp.concatenate([row[:, None] * inv, col[:, None] * inv], axis=-1)
    cos = jnp.concatenate([jnp.ones((lc, ROPE_HALF), F32), jnp.cos(ang)], axis=0)
    sin = jnp.concatenate([jnp.zeros((lc, ROPE_HALF), F32), jnp.sin(ang)], axis=0)
    zero = jnp.zeros_like(sin)
    head = lambda lo, hi: jnp.tile(jnp.concatenate([lo, hi], axis=-1), (1, ATTN_HEADS))
    return head(cos, cos), head(-sin, zero), head(zero, sin)


def _attn_kernel(q_ref, k_ref, v_ref, o_ref, acc_ref, *, nct, lc):
    kv = pl.program_id(1)
    t = pl.program_id(2)

    def heads(k, v):
        def body(g, carry):
            s = _dot_nt(q_ref[0, g], k)
            m = jnp.max(s, axis=-1, keepdims=True)
            p = jnp.exp(s - m)
            l = jnp.sum(p, axis=-1, keepdims=True)
            acc_ref[g] = _dot(p.astype(BF16), v) / l
            return carry
        lax.fori_loop(0, ATTN_GROUP, body, 0)

    @pl.when(t < nct)
    def _():
        heads(k_ref[0, :lc, :], v_ref[0, :lc, :])

    @pl.when(t >= nct)
    def _():
        heads(k_ref[0], v_ref[0])

    lane = lax.broadcasted_iota(jnp.int32, (TR, LANES), 1)
    low = lane < HEAD_DIM
    for pr in range(ATTN_GROUP // PAIR):
        a, b = acc_ref[PAIR * pr], acc_ref[PAIR * pr + 1]
        a_lo = jnp.where(kv == 0, a, pltpu.roll(a, HEAD_DIM, 1))
        b_hi = jnp.where(kv == 0, pltpu.roll(b, HEAD_DIM, 1), b)
        o_ref[0, :, pr * LANES:(pr + 1) * LANES] = jnp.where(low, a_lo, b_hi).astype(BF16)


def _attention(q, k, v, nct, lc):
    nb, _, s, _ = q.shape
    gw = ATTN_GROUP * HEAD_DIM
    return pl.pallas_call(
        functools.partial(_attn_kernel, nct=nct, lc=lc),
        grid=(nb, ATTN_KV_HEADS, s // TR),
        in_specs=[pl.BlockSpec((1, ATTN_GROUP, TR, LANES), lambda b, j, t: (b, j, t, 0)),
                  pl.BlockSpec((1, s, KV_WIDTH), lambda b, j, t: (b, 0, 0)),
                  pl.BlockSpec((1, s, KV_WIDTH), lambda b, j, t: (b, 0, 0))],
        out_specs=pl.BlockSpec((1, TR, gw), lambda b, j, t: (b, t, j)),
        out_shape=jax.ShapeDtypeStruct((nb, s, ATTN_WIDTH), BF16),
        scratch_shapes=[pltpu.VMEM((ATTN_GROUP, TR, LANES), F32)],
        compiler_params=_cparams(("arbitrary", "arbitrary", "arbitrary")),
        name="attention",
    )(q, k, v)


def _token_shift(p, prev_row, next_row, mu):
    n = p.shape[0]
    row = lax.broadcasted_iota(jnp.int32, p.shape, 0)
    prev = jnp.where(row == 0, prev_row, pltpu.roll(p, 1, 0))
    nxt = jnp.where(row == n - 1, next_row, pltpu.roll(p, n - 1, 0))
    return p + mu * (0.5 * (prev + nxt) - p)


def _halo_rows(prev_ref, next_ref, tile, nct, ng):
    first = jnp.logical_or(tile == 0, tile == nct)
    last = jnp.logical_or(tile == nct - 1, tile == ng - 1)
    prev_row = jnp.where(first, 0.0, prev_ref[0, SUBLANES - 1:SUBLANES, :])
    next_row = jnp.where(last, 0.0, next_ref[0, 0:1, :])
    return prev_row, next_row


def _rwkv_tile(rev, g, nct, ng):
    if not rev:
        return g
    return jnp.where(g < nct, nct - 1 - g, ng - 1 - (g - nct))


def _block_diag(xs):
    z = jnp.zeros((CH, CH), xs.dtype)
    return jnp.concatenate([jnp.concatenate([xs[:CH], z], axis=1), jnp.concatenate([z, xs[CH:]], axis=1)], axis=0)


def _rwkv_dir_kernel(p_ref, prev_ref, next_ref, mu_ref, kkw_ref, ka_ref, rk_ref, w0_ref, a0_ref, w2_ref, a2_ref,
                     o_ref, bv_ref, s_ref, *, rev, nct, ng):
    g = pl.program_id(1)
    tile = _rwkv_tile(rev, g, nct, ng)
    w = RWKV_WIDTH

    @pl.when(g == 0)
    def _():
        s_ref[...] = jnp.zeros_like(s_ref)

    prev_row, next_row = _halo_rows(prev_ref, next_ref, tile, nct, ng)
    ps = _token_shift(p_ref[0], prev_row, next_row, mu_ref[...])
    r, k, v = ps[:, :w], ps[:, w:2 * w], ps[:, 2 * w:3 * w]
    wd = ps[:, 3 * w:3 * w + 2 * DECAY_LORA]
    ad = ps[:, 3 * w + 2 * DECAY_LORA:3 * w + 2 * DECAY_LORA + 2 * ICLR_LORA]
    kk = k * kkw_ref[...]
    kk = kk * lax.rsqrt(_head_sums(kk * kk) + 1e-12)
    x = w0_ref[...] + _dot3(jnp.tanh(wd), w2_ref[...])
    lw = (-math.exp(-0.5)) * _sigmoid(x)
    a = _sigmoid(a0_ref[...] + _dot3(ad, a2_ref[...]))
    k_d = k * (1.0 + (a - 1.0) * ka_ref[...])
    bv_ref[0] = _head_sums(r * k_d * rk_ref[...]) * v
    beta = kk * a

    ti = lax.broadcasted_iota(jnp.int32, (PAIR * CH, CH), 0) & (CH - 1)
    tj = lax.broadcasted_iota(jnp.int32, (PAIR * CH, CH), 1)
    prec2 = (tj > ti) if rev else (tj < ti)
    eye2 = ti == tj
    incl2 = jnp.logical_or(prec2, eye2)
    late, early = (tj, ti) if rev else (ti, tj)
    join = [jnp.logical_and(jnp.logical_and((ti ^ tj) < 2 * b, (late & b) != 0), (early & b) == 0)
            for b in (2 ** m for m in range(int(math.log2(CH))))]
    si = lax.broadcasted_iota(jnp.int32, (CH, CH), 0)
    sj = lax.broadcasted_iota(jnp.int32, (CH, CH), 1)
    eye = si == sj
    tri = jnp.where((sj >= si) if rev else (sj <= si), 1.0, 0.0).astype(BF16)
    same_head = (si // HEAD_DIM) == (sj // HEAD_DIM)
    lane = lax.broadcasted_iota(jnp.int32, (CH, LANES), 1)
    head0 = lane < HEAD_DIM
    sel = lambda x0, x1: jnp.where(head0, x0, x1)
    n_chunks = TR // CH
    n_pairs = RWKV_HEADS // PAIR
    chunk_order = list(range(n_chunks - 1, -1, -1) if rev else range(n_chunks))
    units = [(ci, pp) for ci in chunk_order for pp in range(n_pairs)]
    each = lambda f, *lists: [f(*args) for args in zip(*lists)]

    cum = {}
    for ci in chunk_order:
        cs = _dot(tri, jnp.concatenate(_split3(lw[ci * CH:(ci + 1) * CH]), axis=1))
        cum[ci] = cs[:, :w] + cs[:, w:2 * w] + cs[:, 2 * w:]
    blk = lambda x: [x[ci * CH:(ci + 1) * CH, pp * LANES:(pp + 1) * LANES] for ci, pp in units]
    rx, vx, kap, lwx, kx, bet = blk(r), blk(v), blk(kk), blk(lw), blk(k_d), blk(beta)
    cumi = [cum[ci][:, pp * LANES:(pp + 1) * LANES] for ci, pp in units]
    tot = each(lambda c: c[0:1, :] if rev else c[CH - 1:CH, :], cumi)
    cc = each(lambda c: c - c[CH // 2:CH // 2 + 1, :], cumi)
    e_neg = each(lambda c: jnp.exp(-c), cc)
    e_end = each(lambda t, c: jnp.exp(t - c), tot, cumi)
    kap_t = each(lambda x, c, l: x * jnp.exp(c - l), kap, cc, lwx)
    r_t = each(lambda x, c: x * jnp.exp(c), rx, cc)
    kap_0 = each(lambda x, c, l: x * jnp.exp(c - l), kap, cumi, lwx)
    r_0 = each(lambda x, c: x * jnp.exp(c), rx, cumi)
    k_t = each(lambda x, e: (x * e).astype(BF16), kx, e_neg)
    bet_t = each(lambda x, e: (x * e).astype(BF16), bet, e_neg)
    k_e = each(lambda x, e: x * e, kx, e_end)
    bet_e = each(lambda x, e: x * e, bet, e_end)
    p_end = each(jnp.exp, tot)
    qs = each(lambda a, b: jnp.concatenate([jnp.where(head0, a, 0.0), jnp.where(head0, 0.0, a),
                                            jnp.where(head0, b, 0.0), jnp.where(head0, 0.0, b)],
                                           axis=0).astype(BF16), kap_t, r_t)
    s_b = each(_dot_nt, qs, bet_t)
    s_k = each(_dot_nt, qs, k_t)
    a_b = each(lambda s: jnp.where(prec2, s[:PAIR * CH], 0.0), s_b)
    a_k = each(lambda s: jnp.where(prec2, s[:PAIR * CH], 0.0).astype(BF16), s_k)
    b_b = each(lambda s: jnp.where(incl2, s[PAIR * CH:], 0.0).astype(BF16), s_b)
    b_k = each(lambda s: jnp.where(incl2, s[PAIR * CH:], 0.0).astype(BF16), s_k)
    ts = each(lambda a: jnp.where(eye2, 1.0, jnp.where(join[0], -a, 0.0)), a_b)
    for lvl in range(1, len(join)):
        q_b = each(lambda a: jnp.where(join[lvl], a, 0.0).astype(BF16), a_b)
        t_b = each(lambda t: t.astype(BF16), ts)
        dq = each(lambda t, q: _dot(_block_diag(t), q).astype(BF16), t_b, q_b)
        ts = each(lambda t, e, tb: t - _dot(_block_diag(e), tb), ts, dq, t_b)
    t_bd = each(lambda t: _block_diag(t.astype(BF16)), ts)
    vb2 = each(lambda x: jnp.concatenate([x.astype(BF16)] * PAIR, axis=0), vx)
    av2 = each(lambda a, x: _dot(_block_diag(a), x), a_k, vb2)
    rhs = each(lambda k0, a: jnp.concatenate([k0, sel(a[:CH], a[CH:])], axis=1).astype(BF16), kap_0, av2)
    wz = each(lambda t, x: _dot(t, jnp.concatenate([x] * PAIR, axis=0)), t_bd, rhs)
    kq = each(lambda x: sel(x[:CH, :LANES], x[CH:, :LANES]), wz)
    z0 = each(lambda x: sel(x[:CH, LANES:], x[CH:, LANES:]), wz)
    kz = each(lambda a, b: jnp.concatenate([a, b], axis=1).astype(BF16), kq, z0)
    bz = each(lambda b, x: _dot(_block_diag(b), jnp.concatenate([x] * PAIR, axis=0)), b_b, kz)
    o0k = each(lambda b, x: _dot(_block_diag(b), x), b_k, vb2)
    r_new = each(lambda x, b: x - sel(b[:CH, :LANES], b[CH:, :LANES]), r_0, bz)
    o0 = each(lambda a, b: sel(a[:CH], a[CH:]) - sel(b[:CH, LANES:], b[CH:, LANES:]), o0k, bz)
    m_c = each(lambda p, b, q: jnp.where(eye, p, 0.0) - jnp.where(same_head, _dot_tn(b, q), 0.0), p_end, bet_e, kq)
    g_c = each(lambda ke, be, x, z: jnp.where(same_head, _dot_tn(jnp.concatenate([ke, be], axis=0),
                                                                  jnp.concatenate([x, -z], axis=0)), 0.0),
               k_e, bet_e, vx, z0)

    states = [s_ref[pp] for pp in range(n_pairs)]
    outs = {}
    for u, (ci, pp) in enumerate(units):
        st = states[pp]
        outs[ci, pp] = _dot3(r_new[u], st) + o0[u]
        states[pp] = _dot3(m_c[u], st) + g_c[u]
    for ci in chunk_order:
        o_ref[0, ci * CH:(ci + 1) * CH, :] = jnp.concatenate([outs[ci, pp] for pp in range(n_pairs)], axis=1)
    for pp in range(n_pairs):
        s_ref[pp] = states[pp]


def _rwkv_scan(pr, lp, nct, rev):
    nb, s, pw = pr.shape
    ng = s // TR
    w = RWKV_WIDTH
    hb = TR // SUBLANES
    d = int(rev)
    tile = functools.partial(_rwkv_tile, rev, nct=nct, ng=ng)
    n_row_blocks = s // SUBLANES
    main = pl.BlockSpec((1, TR, pw), lambda b, g: (b, tile(g), 0))
    prev = pl.BlockSpec((1, SUBLANES, pw), lambda b, g: (b, jnp.maximum(tile(g) * hb - 1, 0), 0))
    nxt = pl.BlockSpec((1, SUBLANES, pw), lambda b, g: (b, jnp.minimum((tile(g) + 1) * hb, n_row_blocks - 1), 0))
    vec = lambda n: pl.BlockSpec((1, n), lambda b, g: (0, 0))
    mat = pl.BlockSpec((2 * DECAY_LORA, w), lambda b, g: (0, 0))
    out = pl.BlockSpec((1, TR, w), lambda b, g: (b, tile(g), 0))
    zeros = jnp.zeros((DECAY_LORA, w), F32)
    pad = lambda m: jnp.concatenate([zeros, m] if rev else [m, zeros])
    return pl.pallas_call(
        functools.partial(_rwkv_dir_kernel, rev=rev, nct=nct, ng=ng),
        grid=(nb, ng),
        in_specs=[main, prev, nxt, vec(pw), vec(w), vec(w), vec(w), vec(w), vec(w), mat, mat],
        out_specs=[out, out],
        out_shape=[jax.ShapeDtypeStruct((nb, s, w), F32), jax.ShapeDtypeStruct((nb, s, w), F32)],
        scratch_shapes=[pltpu.VMEM((RWKV_HEADS // PAIR, LANES, LANES), F32)],
        compiler_params=_cparams(("arbitrary", "arbitrary")),
        name="rwkv_scan_bwd" if rev else "rwkv_scan_fwd",
    )(pr, pr, pr, lp['mu'].reshape(1, pw), lp['kk'].reshape(1, w), lp['ka'].reshape(1, w), lp['rk'].reshape(1, w),
      lp['w0'][d].reshape(1, w), lp['a0'][d].reshape(1, w), pad(lp['w2'][d]), pad(lp['a2'][d]))


def _rwkv_out_kernel(of_ref, ob_ref, bf_ref, bb_ref, gd_ref, gprev_ref, gnext_ref, mu_ref, g2_ref, gw_ref, gb_ref,
                     y_ref, *, nct, ng):
    tile = pl.program_id(1)
    o = of_ref[0] + ob_ref[0]
    cen = o - _head_sums(o, 1.0 / HEAD_DIM)
    var = _head_sums(cen * cen, 1.0 / HEAD_DIM)
    y = cen * lax.rsqrt(var + GN_EPS) * gw_ref[...] + gb_ref[...] + (bf_ref[0] + bb_ref[0])
    prev_row, next_row = _halo_rows(gprev_ref, gnext_ref, tile, nct, ng)
    gd = _token_shift(gd_ref[0], prev_row, next_row, mu_ref[...])
    gate = _dot3(_sigmoid(gd), g2_ref[...])
    y_ref[0] = (y * gate).astype(BF16)


def _rwkv_out(o_f, o_b, bv_f, bv_b, pr, lp, nct):
    nb, s, w = o_f.shape
    ng = s // TR
    hb = TR // SUBLANES
    gcol = (SHIFT_WIDTH - GATE_LORA) // GATE_LORA
    n_row_blocks = s // SUBLANES
    row = pl.BlockSpec((1, TR, w), lambda b, t: (b, t, 0))
    vec = lambda n: pl.BlockSpec((1, n), lambda b, t: (0, 0))
    return pl.pallas_call(
        functools.partial(_rwkv_out_kernel, nct=nct, ng=ng),
        grid=(nb, ng),
        in_specs=[row, row, row, row,
                  pl.BlockSpec((1, TR, GATE_LORA), lambda b, t: (b, t, gcol)),
                  pl.BlockSpec((1, SUBLANES, GATE_LORA), lambda b, t: (b, jnp.maximum(t * hb - 1, 0), gcol)),
                  pl.BlockSpec((1, SUBLANES, GATE_LORA),
                               lambda b, t: (b, jnp.minimum((t + 1) * hb, n_row_blocks - 1), gcol)),
                  vec(GATE_LORA), pl.BlockSpec((GATE_LORA, w), lambda b, t: (0, 0)), vec(w), vec(w)],
        out_specs=row,
        out_shape=jax.ShapeDtypeStruct((nb, s, w), BF16),
        compiler_params=_cparams(("arbitrary", "arbitrary")),
        name="rwkv_out",
    )(o_f, o_b, bv_f, bv_b, pr, pr, pr, lp['mu'][SHIFT_WIDTH - GATE_LORA:].reshape(1, GATE_LORA), lp['g2'],
      lp['gn_w'].reshape(1, w), lp['gn_b'].reshape(1, w))


def _top2(logits):
    lane = lax.broadcasted_iota(jnp.int32, logits.shape, 1)
    v1 = jnp.max(logits, axis=-1, keepdims=True)
    i1 = jnp.min(jnp.where(logits == v1, lane, LANES), axis=-1, keepdims=True)
    rest = jnp.where(lane == i1, -jnp.inf, logits)
    v2 = jnp.max(rest, axis=-1, keepdims=True)
    i2 = jnp.min(jnp.where(rest == v2, lane, LANES), axis=-1, keepdims=True)
    e = jnp.exp(v2 - v1)
    g1 = 1.0 / (1.0 + e)
    g2 = e / (1.0 + e)
    out = jnp.where(lane == 0, i1.astype(F32), 0.0)
    out = jnp.where(lane == 1, i2.astype(F32), out)
    out = jnp.where(lane == 2, g1, out)
    return jnp.where(lane == 3, g2, out)


def _out_proj_kernel(x_ref, att_ref, rw_ref, mod_ref, g_ref, wa_ref, wr_ref, *rest, moe):
    if moe:
        wrt_ref, xo_ref, h_ref, route_ref = rest
    else:
        xo_ref, h_ref = rest
    mix = _dot(att_ref[0], wa_ref[...]) + _dot(rw_ref[0], wr_ref[...])
    x = x_ref[0] + mod_ref[0, 2:3] * mix
    xo_ref[0] = x
    h = _norm_mod(x, g_ref[...], mod_ref[0, 3:4], mod_ref[0, 4:5])
    h_ref[0] = h.astype(BF16)
    if moe:
        lane = lax.broadcasted_iota(jnp.int32, (TR, LANES), 1)
        logits = jnp.where(lane < N_EXPERTS, _dot3(h, wrt_ref[...]), -jnp.inf)
        route_ref[0] = _top2(logits)


def _out_proj(xa, att, rw, mod, g, wo_a, wo_r, nct, w_router=None):
    nb, s, d = xa.shape
    moe = w_router is not None
    row = lambda n: pl.BlockSpec((1, TR, n), lambda b, t: (b, t, 0))
    full = lambda a: pl.BlockSpec(a.shape, lambda b, t: (0, 0))
    in_specs = [row(d), row(ATTN_WIDTH), row(RWKV_WIDTH), pl.BlockSpec((1, N_MOD, d), _mod_index(nb, nct)),
                pl.BlockSpec((1, d), lambda b, t: (0, 0)), full(wo_a), full(wo_r)]
    args = [xa, att, rw, mod, g.reshape(1, d), wo_a, wo_r]
    out_specs = [row(d), row(d)]
    out_shape = [jax.ShapeDtypeStruct((nb, s, d), F32), jax.ShapeDtypeStruct((nb, s, d), BF16)]
    if moe:
        wrt = jnp.pad(w_router, ((0, 0), (0, LANES - N_EXPERTS)))
        in_specs.append(full(wrt))
        args.append(wrt)
        out_specs.append(row(LANES))
        out_shape.append(jax.ShapeDtypeStruct((nb, s, LANES), F32))
    return pl.pallas_call(
        functools.partial(_out_proj_kernel, moe=moe),
        grid=(nb, s // TR),
        in_specs=in_specs, out_specs=out_specs, out_shape=out_shape,
        compiler_params=_cparams(("arbitrary", "arbitrary")),
        name="out_proj_moe" if moe else "out_proj",
    )(*args)


def _ffn_kernel(x_ref, h_ref, mod_ref, wg_ref, wu_ref, wd_ref, o_ref):
    h = h_ref[0]
    gt = _dot(h, wg_ref[...])
    up = _dot(h, wu_ref[...])
    act = (gt * _sigmoid(gt) * up).astype(BF16)
    o_ref[0] = x_ref[0] + mod_ref[0, 5:6] * _dot(act, wd_ref[...])


def _ffn_dense(xa, h, mod, wg, wu, wd, nct):
    nb, s, d = xa.shape
    row = pl.BlockSpec((1, TR, d), lambda b, t: (b, t, 0))
    resident = lambda a: pl.BlockSpec(a.shape, lambda b, t: (0, 0), pipeline_mode=pl.Buffered(1))
    return pl.pallas_call(
        _ffn_kernel,
        grid=(nb, s // TR),
        in_specs=[row, row, pl.BlockSpec((1, N_MOD, d), _mod_index(nb, nct)), resident(wg), resident(wu),
                  resident(wd)],
        out_specs=row,
        out_shape=jax.ShapeDtypeStruct((nb, s, d), F32),
        compiler_params=_cparams(("arbitrary", "arbitrary")),
        name="ffn_dense",
    )(xa, h, mod, wg, wu, wd)


def _moe_kernel(be_ref, nu_ref, x_ref, wg_ref, wu_ref, wd_ref, o_ref):
    i = pl.program_id(0)
    j = pl.program_id(1)

    @pl.when(j == 0)
    def _():
        o_ref[...] = jnp.zeros_like(o_ref)

    @pl.when(i < nu_ref[0])
    def _():
        x = x_ref[...]
        gt = _dot(x, wg_ref[0])
        up = _dot(x, wu_ref[0])
        act = (gt * _sigmoid(gt) * up).astype(BF16)
        o_ref[...] += _dot(act, wd_ref[0])


def _moe_experts(xs, block_expert, n_used, wg, wu, wd):
    n_slots, d = xs.shape
    dff = wg.shape[2]
    nblk = n_slots // MOE_BM
    grid_spec = pltpu.PrefetchScalarGridSpec(
        num_scalar_prefetch=2,
        grid=(nblk, dff // MOE_TF),
        in_specs=[pl.BlockSpec((MOE_BM, d), lambda i, j, be, nu: (i, 0)),
                  pl.BlockSpec((1, d, MOE_TF), lambda i, j, be, nu: (be[i], 0, j)),
                  pl.BlockSpec((1, d, MOE_TF), lambda i, j, be, nu: (be[i], 0, j)),
                  pl.BlockSpec((1, MOE_TF, d), lambda i, j, be, nu: (be[i], j, 0))],
        out_specs=pl.BlockSpec((MOE_BM, d), lambda i, j, be, nu: (i, 0)),
    )
    return pl.pallas_call(
        _moe_kernel,
        grid_spec=grid_spec,
        out_shape=jax.ShapeDtypeStruct((n_slots, d), F32),
        compiler_params=_cparams(("arbitrary", "arbitrary")),
        name="moe_experts",
    )(block_expert, n_used, xs, wg, wu, wd)


def _moe_ffn(xa, h, route, gate_row, wg, wu, wd):
    nb, s, d = xa.shape
    t = nb * s
    n_assign = 2 * t
    nblk = (n_assign + N_EXPERTS * (MOE_BM - 1) + MOE_BM - 1) // MOE_BM
    n_slots = nblk * MOE_BM
    route = route.reshape(t, LANES)
    expert = route[:, 0:2].astype(jnp.int32).reshape(-1)
    gates = route[:, 2:4]
    onehot = (expert[:, None] == jnp.arange(N_EXPERTS, dtype=jnp.int32)[None, :]).astype(jnp.int32)
    rank = jnp.sum((jnp.cumsum(onehot, axis=0) - onehot) * onehot, axis=1)
    counts = jnp.sum(onehot, axis=0)
    padded = (counts + MOE_BM - 1) // MOE_BM * MOE_BM
    pad_end = jnp.cumsum(padded)
    pad_start = pad_end - padded
    slot = pad_start[expert] + rank
    token = jnp.repeat(jnp.arange(t, dtype=jnp.int32), 2)
    slot_token = jnp.full((n_slots,), t, jnp.int32).at[slot].set(token)
    block_start = jnp.arange(nblk, dtype=jnp.int32) * MOE_BM
    block_expert = jnp.minimum(jnp.sum((pad_end[None, :] <= block_start[:, None]).astype(jnp.int32), axis=1),
                               N_EXPERTS - 1)
    n_used = (pad_end[-1] // MOE_BM).astype(jnp.int32).reshape(1)
    h_pad = jnp.concatenate([h.reshape(t, d), jnp.zeros((1, d), h.dtype)], axis=0)
    xs = h_pad[slot_token]
    ys = _moe_experts(xs, block_expert, n_used, wg, wu, wd)
    slot2 = slot.reshape(t, 2)
    f = ys[slot2[:, 0]] * gates[:, 0:1] + ys[slot2[:, 1]] * gates[:, 1:2]
    return xa + gate_row * f.reshape(nb, s, d)


def kernel(x, c, ctx, c_ctx, ada_w, ada_b, norm1_g, norm2_g, w_in, w_out, q_gain, k_gain, shift_mu, rw_w0, rw_w2,
           rw_a0, rw_a2, rw_g2, rw_kk, rw_ka, rw_rk, rw_gn_w, rw_gn_b, ffn_wg, ffn_wu, ffn_wd, moe_router, moe_wg,
           moe_wu, moe_wd):
    nb, l, d = x.shape
    lc = ctx.shape[1]
    depth = ada_w.shape[0]
    assert lc % TR == 0 and l % TR == 0 and l % GRID_W == 0
    nct = lc // TR
    xa = jnp.concatenate([ctx, x], axis=1)

    mod_rows = -(-(nb + 1) // SUBLANES) * SUBLANES
    c_all = jnp.concatenate([c, c_ctx[None], jnp.zeros((mod_rows - nb - 1, d), F32)], axis=0)
    mod_all = _ada_all(c_all, ada_w, ada_b)[:, :nb + 1].reshape(depth, nb + 1, N_MOD, d)
    cos, slo, shi = _rope_tables(lc, l)

    for i in range(depth):
        mod = mod_all[i]
        lp = dict(mu=shift_mu[i], w0=rw_w0[i], w2=rw_w2[i], a0=rw_a0[i], a2=rw_a2[i], g2=rw_g2[i], kk=rw_kk[i],
                  ka=rw_ka[i], rk=rw_rk[i], gn_w=rw_gn_w[i], gn_b=rw_gn_b[i])
        wi = w_in[i].astype(BF16)
        pa, pr = _in_proj(xa, mod, norm1_g[i], wi[:, :ATTN_IN], wi[:, ATTN_IN:], nct)
        q, k, v = _attn_prep(pa, cos, slo, shi, q_gain[i], k_gain[i])
        att = _attention(q, k, v, nct, lc)
        o_f, bv_f = _rwkv_scan(pr, lp, nct, False)
        o_b, bv_b = _rwkv_scan(pr, lp, nct, True)
        rw = _rwkv_out(o_f, o_b, bv_f, bv_b, pr, lp, nct)
        wo = w_out[i].astype(BF16)
        j = i // 2
        if i % 2 == 0:
            xa, h = _out_proj(xa, att, rw, mod, norm2_g[i], wo[:ATTN_WIDTH], wo[ATTN_WIDTH:], nct)
            xa = _ffn_dense(xa, h, mod, ffn_wg[j].astype(BF16), ffn_wu[j].astype(BF16), ffn_wd[j].astype(BF16), nct)
        else:
            xa, h, route = _out_proj(xa, att, rw, mod, norm2_g[i], wo[:ATTN_WIDTH], wo[ATTN_WIDTH:], nct,
                                     moe_router[j])
            gate_row = jnp.concatenate(
                [jnp.broadcast_to(mod[nb, 5][None, None], (nb, lc, d)),
                 jnp.broadcast_to(mod[:nb, 5][:, None], (nb, l, d))], axis=1)
            xa = _moe_ffn(xa, h, route, gate_row, moe_wg[j].astype(BF16), moe_wu[j].astype(BF16),
                          moe_wd[j].astype(BF16))
    return xa[:, lc:]
```

```python
import functools
import math

import jax
import jax.numpy as jnp
from jax import lax
from jax.experimental import pallas as pl
from jax.experimental.pallas import tpu as pltpu

F32 = jnp.float32
BF16 = jnp.bfloat16
HI = lax.Precision.HIGHEST

HEAD_DIM = 64
ROPE_HALF = HEAD_DIM // 2
ROPE_THETA = 10000.0
GRID_W = 64
ATTN_HEADS = 8
ATTN_KV_HEADS = 2
ATTN_GROUP = ATTN_HEADS // ATTN_KV_HEADS
ATTN_WIDTH = ATTN_HEADS * HEAD_DIM
KV_WIDTH = ATTN_KV_HEADS * HEAD_DIM
ATTN_IN = ATTN_WIDTH + 2 * KV_WIDTH
ATTN_SCALE = HEAD_DIM ** -0.5
RWKV_HEADS = 8
RWKV_WIDTH = RWKV_HEADS * HEAD_DIM
DECAY_LORA = 64
ICLR_LORA = 64
GATE_LORA = 128
SHIFT_WIDTH = 3 * RWKV_WIDTH + 2 * DECAY_LORA + 2 * ICLR_LORA + GATE_LORA
N_EXPERTS = 8
N_MOD = 6
EPS = 1e-6
GN_EPS = 64e-5

LANES = 128
SUBLANES = 8
TR = 256
CH = 128
PAIR = LANES // HEAD_DIM
MOE_BM = 512
MOE_TF = 896
VMEM_LIMIT = 56 * 1024 * 1024


def _cparams(sem):
    return pltpu.CompilerParams(dimension_semantics=sem, vmem_limit_bytes=VMEM_LIMIT)


def _dot(a, b, prec=None):
    return jnp.dot(a, b, preferred_element_type=F32, precision=prec)


def _dot_nt(a, b, prec=None):
    return lax.dot_general(a, b, (((1,), (1,)), ((), ())), preferred_element_type=F32, precision=prec)


def _dot_tn(a, b, prec=None):
    return lax.dot_general(a, b, (((0,), (0,)), ((), ())), preferred_element_type=F32, precision=prec)


def _sigmoid(x):
    return 1.0 / (1.0 + jnp.exp(-x))


def _split2(x):
    hi = x.astype(BF16)
    return hi, (x - hi.astype(F32)).astype(BF16)


def _split3(x):
    hi = x.astype(BF16)
    r = x - hi.astype(F32)
    mid = r.astype(BF16)
    return hi, mid, (r - mid.astype(F32)).astype(BF16)


def _dot3(a, b):
    a_hi, a_lo = _split2(a)
    b_hi, b_lo = _split2(b)
    return _dot(a_hi, b_hi) + (_dot(a_hi, b_lo) + _dot(a_lo, b_hi))


def _head_sums(x, scale=1.0):
    r, n = x.shape
    nblk = n // LANES
    xs = jnp.concatenate([x[:, c * LANES:(c + 1) * LANES] for c in range(nblk)], axis=0)
    i = lax.broadcasted_iota(jnp.int32, (LANES, LANES), 0) // HEAD_DIM
    j = lax.broadcasted_iota(jnp.int32, (LANES, LANES), 1) // HEAD_DIM
    ones = jnp.where(i == j, scale, 0.0).astype(BF16)
    m = nblk * r
    s = _dot(jnp.concatenate(_split3(xs), axis=0), ones)
    s = s[:m] + s[m:2 * m] + s[2 * m:]
    return jnp.concatenate([s[c * r:(c + 1) * r] for c in range(nblk)], axis=1)


def _ada_kernel(c_ref, w_ref, b_ref, o_ref):
    c = c_ref[...]
    sc = c * _sigmoid(c)
    o_ref[0] = _dot(sc, w_ref[0], HI) + b_ref[0]


def _ada_all(c_all, ada_w, ada_b):
    depth, d, n = ada_w.shape
    rows = c_all.shape[0]
    tn = 1536
    return pl.pallas_call(
        _ada_kernel,
        grid=(depth, n // tn),
        in_specs=[pl.BlockSpec((rows, d), lambda i, j: (0, 0)),
                  pl.BlockSpec((1, d, tn), lambda i, j: (i, 0, j)),
                  pl.BlockSpec((1, 1, tn), lambda i, j: (i, 0, j))],
        out_specs=pl.BlockSpec((1, rows, tn), lambda i, j: (i, 0, j)),
        out_shape=jax.ShapeDtypeStruct((depth, rows, n), F32),
        compiler_params=_cparams(("arbitrary", "arbitrary")),
        name="ada_mod",
    )(c_all, ada_w, ada_b.reshape(depth, 1, n))


def _norm_mod(x, g, shift, scale):
    y = x * lax.rsqrt(jnp.mean(x * x, axis=-1, keepdims=True) + EPS)
    return (y * g) * (1.0 + scale) + shift


def _in_proj_kernel(x_ref, mod_ref, g_ref, wa_ref, wr_ref, pa_ref, pr_ref):
    h = _norm_mod(x_ref[0], g_ref[...], mod_ref[0, 0:1], mod_ref[0, 1:2]).astype(BF16)
    pa_ref[0] = _dot(h, wa_ref[...])
    pr_ref[0] = _dot(h, wr_ref[...])


def _mod_index(nb, nct):
    return lambda b, t: (jnp.where(t < nct, nb, b), 0, 0)


def _in_proj(xa, mod, g, wa, wr, nct):
    nb, s, d = xa.shape
    na, nr = wa.shape[1], wr.shape[1]
    return pl.pallas_call(
        _in_proj_kernel,
        grid=(nb, s // TR),
        in_specs=[pl.BlockSpec((1, TR, d), lambda b, t: (b, t, 0)),
                  pl.BlockSpec((1, N_MOD, d), _mod_index(nb, nct)),
                  pl.BlockSpec((1, d), lambda b, t: (0, 0)),
                  pl.BlockSpec((d, na), lambda b, t: (0, 0)),
                  pl.BlockSpec((d, nr), lambda b, t: (0, 0))],
        out_specs=[pl.BlockSpec((1, TR, na), lambda b, t: (b, t, 0)),
                   pl.BlockSpec((1, TR, nr), lambda b, t: (b, t, 0))],
        out_shape=[jax.ShapeDtypeStruct((nb, s, na), F32),
                   jax.ShapeDtypeStruct((nb, s, nr), F32)],
        compiler_params=_cparams(("arbitrary", "arbitrary")),
        name="in_proj",
    )(xa, mod, g.reshape(1, d), wa, wr)


def _rope(x, cos, sin_lo, sin_hi):
    n = x.shape[-1]
    return x * cos + pltpu.roll(x, n - ROPE_HALF, 1) * sin_lo + pltpu.roll(x, ROPE_HALF, 1) * sin_hi


def _attn_prep_kernel(pa_ref, cos_ref, slo_ref, shi_ref, qg_ref, kg_ref, q_ref, k_ref, v_ref):
    pa = pa_ref[0]
    qk = pa[:, :ATTN_WIDTH + KV_WIDTH]
    v = pa[:, ATTN_WIDTH + KV_WIDTH:ATTN_IN]
    cos, slo, shi = cos_ref[...], slo_ref[...], shi_ref[...]
    inv = lax.rsqrt(_head_sums(qk * qk, 1.0 / HEAD_DIM) + EPS)
    qn = qk[:, :ATTN_WIDTH] * inv[:, :ATTN_WIDTH] * qg_ref[...]
    qr = _rope(qn, cos, slo, shi) * ATTN_SCALE
    kn = qk[:, ATTN_WIDTH:] * inv[:, ATTN_WIDTH:] * kg_ref[...]
    kr = _rope(kn, cos[:, :KV_WIDTH], slo[:, :KV_WIDTH], shi[:, :KV_WIDTH])
    k_ref[0] = kr.astype(BF16)
    v_ref[0] = v.astype(BF16)
    lane = lax.broadcasted_iota(jnp.int32, (TR, LANES), 1)
    for h in range(ATTN_HEADS):
        src = qr[:, (h // PAIR) * LANES:(h // PAIR + 1) * LANES]
        kv = h // ATTN_GROUP
        if h % PAIR != kv:
            src = pltpu.roll(src, HEAD_DIM, 1)
        keep = (lane // HEAD_DIM) == kv
        q_ref[0, h] = jnp.where(keep, src, 0.0).astype(BF16)


def _attn_prep(pa, cos, slo, shi, q_gain, k_gain):
    nb, s, _ = pa.shape
    qg = jnp.tile(q_gain, ATTN_HEADS).reshape(1, ATTN_WIDTH)
    kg = jnp.tile(k_gain, ATTN_KV_HEADS).reshape(1, KV_WIDTH)
    tab = pl.BlockSpec((TR, ATTN_WIDTH), lambda t, b: (t, 0))
    return pl.pallas_call(
        _attn_prep_kernel,
        grid=(s // TR, nb),
        in_specs=[pl.BlockSpec((1, TR, ATTN_IN), lambda t, b: (b, t, 0)), tab, tab, tab,
                  pl.BlockSpec((1, ATTN_WIDTH), lambda t, b: (0, 0)),
                  pl.BlockSpec((1, KV_WIDTH), lambda t, b: (0, 0))],
        out_specs=[pl.BlockSpec((1, ATTN_HEADS, TR, LANES), lambda t, b: (b, 0, t, 0)),
                   pl.BlockSpec((1, TR, KV_WIDTH), lambda t, b: (b, t, 0)),
                   pl.BlockSpec((1, TR, KV_WIDTH), lambda t, b: (b, t, 0))],
        out_shape=[jax.ShapeDtypeStruct((nb, ATTN_HEADS, s, LANES), BF16),
                   jax.ShapeDtypeStruct((nb, s, KV_WIDTH), BF16),
                   jax.ShapeDtypeStruct((nb, s, KV_WIDTH), BF16)],
        compiler_params=_cparams(("arbitrary", "arbitrary")),
        name="attn_prep",
    )(pa, cos, slo, shi, qg, kg)


def _rope_tables(lc, l):
    rows = l // GRID_W
    row = jnp.repeat(jnp.arange(rows, dtype=F32), GRID_W)
    col = jnp.tile(jnp.arange(GRID_W, dtype=F32), rows)
    inv = ROPE_THETA ** (-jnp.arange(0, ROPE_HALF, 2, dtype=F32) / ROPE_HALF)
    ang = jnp.concatenate([row[:, None] * inv, col[:, None] * inv], axis=-1)
    cos = jnp.concatenate([jnp.ones((lc, ROPE_HALF), F32), jnp.cos(ang)], axis=0)
    sin = jnp.concatenate([jnp.zeros((lc, ROPE_HALF), F32), jnp.sin(ang)], axis=0)
    zero = jnp.zeros_like(sin)
    head = lambda lo, hi: jnp.tile(jnp.concatenate([lo, hi], axis=-1), (1, ATTN_HEADS))
    return head(cos, cos), head(-sin, zero), head(zero, sin)


def _attn_kernel(q_ref, k_ref, v_ref, o_ref, acc_ref, *, nct, lc):
    kv = pl.program_id(1)
    t = pl.program_id(2)

    def heads(k, v):
        def body(g, carry):
            s = _dot_nt(q_ref[0, g], k)
            m = jnp.max(s, axis=-1, keepdims=True)
            p = jnp.exp(s - m)
            l = jnp.sum(p, axis=-1, keepdims=True)
            acc_ref[g] = _dot(p.astype(BF16), v) / l
            return carry
        lax.fori_loop(0, ATTN_GROUP, body, 0)

    @pl.when(t < nct)
    def _():
        heads(k_ref[0, :lc, :], v_ref[0, :lc, :])

    @pl.when(t >= nct)
    def _():
        heads(k_ref[0], v_ref[0])

    lane = lax.broadcasted_iota(jnp.int32, (TR, LANES), 1)
    low = lane < HEAD_DIM
    for pr in range(ATTN_GROUP // PAIR):
        a, b = acc_ref[PAIR * pr], acc_ref[PAIR * pr + 1]
        a_lo = jnp.where(kv == 0, a, pltpu.roll(a, HEAD_DIM, 1))
        b_hi = jnp.where(kv == 0, pltpu.roll(b, HEAD_DIM, 1), b)
        o_ref[0, :, pr * LANES:(pr + 1) * LANES] = jnp.where(low, a_lo, b_hi).astype(BF16)


def _attention(q, k, v, nct, lc):
    nb, _, s, _ = q.shape
    gw = ATTN_GROUP * HEAD_DIM
    return pl.pallas_call(
        functools.partial(_attn_kernel, nct=nct, lc=lc),
        grid=(nb, ATTN_KV_HEADS, s // TR),
        in_specs=[pl.BlockSpec((1, ATTN_GROUP, TR, LANES), lambda b, j, t: (b, j, t, 0)),
                  pl.BlockSpec((1, s, KV_WIDTH), lambda b, j, t: (b, 0, 0)),
                  pl.BlockSpec((1, s, KV_WIDTH), lambda b, j, t: (b, 0, 0))],
        out_specs=pl.BlockSpec((1, TR, gw), lambda b, j, t: (b, t, j)),
        out_shape=jax.ShapeDtypeStruct((nb, s, ATTN_WIDTH), BF16),
        scratch_shapes=[pltpu.VMEM((ATTN_GROUP, TR, LANES), F32)],
        compiler_params=_cparams(("arbitrary", "arbitrary", "arbitrary")),
        name="attention",
    )(q, k, v)


def _token_shift(p, prev_row, next_row, mu):
    n = p.shape[0]
    row = lax.broadcasted_iota(jnp.int32, p.shape, 0)
    prev = jnp.where(row == 0, prev_row, pltpu.roll(p, 1, 0))
    nxt = jnp.where(row == n - 1, next_row, pltpu.roll(p, n - 1, 0))
    return p + mu * (0.5 * (prev + nxt) - p)


def _halo_rows(prev_ref, next_ref, tile, nct, ng):
    first = jnp.logical_or(tile == 0, tile == nct)
    last = jnp.logical_or(tile == nct - 1, tile == ng - 1)
    prev_row = jnp.where(first, 0.0, prev_ref[0, SUBLANES - 1:SUBLANES, :])
    next_row = jnp.where(last, 0.0, next_ref[0, 0:1, :])
    return prev_row, next_row


def _rwkv_tile(rev, g, nct, ng):
    if not rev:
        return g
    return jnp.where(g < nct, nct - 1 - g, ng - 1 - (g - nct))


def _block_diag(xs):
    z = jnp.zeros((CH, CH), xs.dtype)
    return jnp.concatenate([jnp.concatenate([xs[:CH], z], axis=1), jnp.concatenate([z, xs[CH:]], axis=1)], axis=0)


def _rwkv_dir_kernel(p_ref, prev_ref, next_ref, mu_ref, kkw_ref, ka_ref, rk_ref, w0_ref, a0_ref, w2_ref, a2_ref,
                     o_ref, bv_ref, s_ref, *, rev, nct, ng):
    g = pl.program_id(1)
    tile = _rwkv_tile(rev, g, nct, ng)
    w = RWKV_WIDTH

    @pl.when(g == 0)
    def _():
        s_ref[...] = jnp.zeros_like(s_ref)

    prev_row, next_row = _halo_rows(prev_ref, next_ref, tile, nct, ng)
    ps = _token_shift(p_ref[0], prev_row, next_row, mu_ref[...])
    r, k, v = ps[:, :w], ps[:, w:2 * w], ps[:, 2 * w:3 * w]
    wd = ps[:, 3 * w:3 * w + 2 * DECAY_LORA]
    ad = ps[:, 3 * w + 2 * DECAY_LORA:3 * w + 2 * DECAY_LORA + 2 * ICLR_LORA]
    kk = k * kkw_ref[...]
    kk = kk * lax.rsqrt(_head_sums(kk * kk) + 1e-12)
    x = w0_ref[...] + _dot3(jnp.tanh(wd), w2_ref[...])
    lw = (-math.exp(-0.5)) * _sigmoid(x)
    a = _sigmoid(a0_ref[...] + _dot3(ad, a2_ref[...]))
    k_d = k * (1.0 + (a - 1.0) * ka_ref[...])
    bv_ref[0] = _head_sums(r * k_d * rk_ref[...]) * v
    beta = kk * a

    ti = lax.broadcasted_iota(jnp.int32, (PAIR * CH, CH), 0) & (CH - 1)
    tj = lax.broadcasted_iota(jnp.int32, (PAIR * CH, CH), 1)
    prec2 = (tj > ti) if rev else (tj < ti)
    eye2 = ti == tj
    incl2 = jnp.logical_or(prec2, eye2)
    late, early = (tj, ti) if rev else (ti, tj)
    join = [jnp.logical_and(jnp.logical_and((ti ^ tj) < 2 * b, (late & b) != 0), (early & b) == 0)
            for b in (2 ** m for m in range(int(math.log2(CH))))]
    si = lax.broadcasted_iota(jnp.int32, (CH, CH), 0)
    sj = lax.broadcasted_iota(jnp.int32, (CH, CH), 1)
    eye = si == sj
    tri = jnp.where((sj >= si) if rev else (sj <= si), 1.0, 0.0).astype(BF16)
    same_head = (si // HEAD_DIM) == (sj // HEAD_DIM)
    lane = lax.broadcasted_iota(jnp.int32, (CH, LANES), 1)
    head0 = lane < HEAD_DIM
    sel = lambda x0, x1: jnp.where(head0, x0, x1)
    n_chunks = TR // CH
    n_pairs = RWKV_HEADS // PAIR
    chunk_order = list(range(n_chunks - 1, -1, -1) if rev else range(n_chunks))
    units = [(ci, pp) for ci in chunk_order for pp in range(n_pairs)]
    each = lambda f, *lists: [f(*args) for args in zip(*lists)]

    cum = {}
    for ci in chunk_order:
        cs = _dot(tri, jnp.concatenate(_split3(lw[ci * CH:(ci + 1) * CH]), axis=1))
        cum[ci] = cs[:, :w] + cs[:, w:2 * w] + cs[:, 2 * w:]
    blk = lambda x: [x[ci * CH:(ci + 1) * CH, pp * LANES:(pp + 1) * LANES] for ci, pp in units]
    rx, vx, kap, lwx, kx, bet = blk(r), blk(v), blk(kk), blk(lw), blk(k_d), blk(beta)
    cumi = [cum[ci][:, pp * LANES:(pp + 1) * LANES] for ci, pp in units]
    tot = each(lambda c: c[0:1, :] if rev else c[CH - 1:CH, :], cumi)
    cc = each(lambda c: c - c[CH // 2:CH // 2 + 1, :], cumi)
    e_neg = each(lambda c: jnp.exp(-c), cc)
    e_end = each(lambda t, c: jnp.exp(t - c), tot, cumi)
    kap_t = each(lambda x, c, l: x * jnp.exp(c - l), kap, cc, lwx)
    r_t = each(lambda x, c: x * jnp.exp(c), rx, cc)
    kap_0 = each(lambda x, c, l: x * jnp.exp(c - l), kap, cumi, lwx)
    r_0 = each(lambda x, c: x * jnp.exp(c), rx, cumi)
    k_t = each(lambda x, e: (x * e).astype(BF16), kx, e_neg)
    bet_t = each(lambda x, e: (x * e).astype(BF16), bet, e_neg)
    k_e = each(lambda x, e: x * e, kx, e_end)
    bet_e = each(lambda x, e: x * e, bet, e_end)
    p_end = each(jnp.exp, tot)
    qs = each(lambda a, b: jnp.concatenate([jnp.where(head0, a, 0.0), jnp.where(head0, 0.0, a),
                                            jnp.where(head0, b, 0.0), jnp.where(head0, 0.0, b)],
                                           axis=0).astype(BF16), kap_t, r_t)
    s_b = each(_dot_nt, qs, bet_t)
    s_k = each(_dot_nt, qs, k_t)
    a_b = each(lambda s: jnp.where(prec2, s[:PAIR * CH], 0.0), s_b)
    a_k = each(lambda s: jnp.where(prec2, s[:PAIR * CH], 0.0).astype(BF16), s_k)
    b_b = each(lambda s: jnp.where(incl2, s[PAIR * CH:], 0.0).astype(BF16), s_b)
    b_k = each(lambda s: jnp.where(incl2, s[PAIR * CH:], 0.0).astype(BF16), s_k)
    ts = each(lambda a: jnp.where(eye2, 1.0, jnp.where(join[0], -a, 0.0)), a_b)
    for lvl in range(1, len(join)):
        q_b = each(lambda a: jnp.where(join[lvl], a, 0.0).astype(BF16), a_b)
        t_b = each(lambda t: t.astype(BF16), ts)
        dq = each(lambda t, q: _dot(_block_diag(t), q).astype(BF16), t_b, q_b)
        ts = each(lambda t, e, tb: t - _dot(_block_diag(e), tb), ts, dq, t_b)
    t_bd = each(lambda t: _block_diag(t.astype(BF16)), ts)
    vb2 = each(lambda x: jnp.concatenate([x.astype(BF16)] * PAIR, axis=0), vx)
    av2 = each(lambda a, x: _dot(_block_diag(a), x), a_k, vb2)
    rhs = each(lambda k0, a: jnp.concatenate([k0, sel(a[:CH], a[CH:])], axis=1).astype(BF16), kap_0, av2)
    wz = each(lambda t, x: _dot(t, jnp.concatenate([x] * PAIR, axis=0)), t_bd, rhs)
    kq = each(lambda x: sel(x[:CH, :LANES], x[CH:, :LANES]), wz)
    z0 = each(lambda x: sel(x[:CH, LANES:], x[CH:, LANES:]), wz)
    kz = each(lambda a, b: jnp.concatenate([a, b], axis=1).astype(BF16), kq, z0)
    bz = each(lambda b, x: _dot(_block_diag(b), jnp.concatenate([x] * PAIR, axis=0)), b_b, kz)
    o0k = each(lambda b, x: _dot(_block_diag(b), x), b_k, vb2)
    r_new = each(lambda x, b: x - sel(b[:CH, :LANES], b[CH:, :LANES]), r_0, bz)
    o0 = each(lambda a, b: sel(a[:CH], a[CH:]) - sel(b[:CH, LANES:], b[CH:, LANES:]), o0k, bz)
    m_c = each(lambda p, b, q: jnp.where(eye, p, 0.0) - jnp.where(same_head, _dot_tn(b, q), 0.0), p_end, bet_e, kq)
    g_c = each(lambda ke, be, x, z: jnp.where(same_head, _dot_tn(jnp.concatenate([ke, be], axis=0),
                                                                  jnp.concatenate([x, -z], axis=0)), 0.0),
               k_e, bet_e, vx, z0)

    states = [s_ref[pp] for pp in range(n_pairs)]
    outs = {}
    for u, (ci, pp) in enumerate(units):
        st = states[pp]
        outs[ci, pp] = _dot3(r_new[u], st) + o0[u]
        states[pp] = _dot3(m_c[u], st) + g_c[u]
    for ci in chunk_order:
        o_ref[0, ci * CH:(ci + 1) * CH, :] = jnp.concatenate([outs[ci, pp] for pp in range(n_pairs)], axis=1)
    for pp in range(n_pairs):
        s_ref[pp] = states[pp]


def _rwkv_scan(pr, lp, nct, rev):
    nb, s, pw = pr.shape
    ng = s // TR
    w = RWKV_WIDTH
    hb = TR // SUBLANES
    d = int(rev)
    tile = functools.partial(_rwkv_tile, rev, nct=nct, ng=ng)
    n_row_blocks = s // SUBLANES
    main = pl.BlockSpec((1, TR, pw), lambda b, g: (b, tile(g), 0))
    prev = pl.BlockSpec((1, SUBLANES, pw), lambda b, g: (b, jnp.maximum(tile(g) * hb - 1, 0), 0))
    nxt = pl.BlockSpec((1, SUBLANES, pw), lambda b, g: (b, jnp.minimum((tile(g) + 1) * hb, n_row_blocks - 1), 0))
    vec = lambda n: pl.BlockSpec((1, n), lambda b, g: (0, 0))
    mat = pl.BlockSpec((2 * DECAY_LORA, w), lambda b, g: (0, 0))
    out = pl.BlockSpec((1, TR, w), lambda b, g: (b, tile(g), 0))
    zeros = jnp.zeros((DECAY_LORA, w), F32)
    pad = lambda m: jnp.concatenate([zeros, m] if rev else [m, zeros])
    return pl.pallas_call(
        functools.partial(_rwkv_dir_kernel, rev=rev, nct=nct, ng=ng),
        grid=(nb, ng),
        in_specs=[main, prev, nxt, vec(pw), vec(w), vec(w), vec(w), vec(w), vec(w), mat, mat],
        out_specs=[out, out],
        out_shape=[jax.ShapeDtypeStruct((nb, s, w), F32), jax.ShapeDtypeStruct((nb, s, w), F32)],
        scratch_shapes=[pltpu.VMEM((RWKV_HEADS // PAIR, LANES, LANES), F32)],
        compiler_params=_cparams(("arbitrary", "arbitrary")),
        name="rwkv_scan_bwd" if rev else "rwkv_scan_fwd",
    )(pr, pr, pr, lp['mu'].reshape(1, pw), lp['kk'].reshape(1, w), lp['ka'].reshape(1, w), lp['rk'].reshape(1, w),
      lp['w0'][d].reshape(1, w), lp['a0'][d].reshape(1, w), pad(lp['w2'][d]), pad(lp['a2'][d]))


def _rwkv_out_kernel(of_ref, ob_ref, bf_ref, bb_ref, gd_ref, gprev_ref, gnext_ref, mu_ref, g2_ref, gw_ref, gb_ref,
                     y_ref, *, nct, ng):
    tile = pl.program_id(1)
    o = of_ref[0] + ob_ref[0]
    cen = o - _head_sums(o, 1.0 / HEAD_DIM)
    var = _head_sums(cen * cen, 1.0 / HEAD_DIM)
    y = cen * lax.rsqrt(var + GN_EPS) * gw_ref[...] + gb_ref[...] + (bf_ref[0] + bb_ref[0])
    prev_row, next_row = _halo_rows(gprev_ref, gnext_ref, tile, nct, ng)
    gd = _token_shift(gd_ref[0], prev_row, next_row, mu_ref[...])
    gate = _dot3(_sigmoid(gd), g2_ref[...])
    y_ref[0] = (y * gate).astype(BF16)


def _rwkv_out(o_f, o_b, bv_f, bv_b, pr, lp, nct):
    nb, s, w = o_f.shape
    ng = s // TR
    hb = TR // SUBLANES
    gcol = (SHIFT_WIDTH - GATE_LORA) // GATE_LORA
    n_row_blocks = s // SUBLANES
    row = pl.BlockSpec((1, TR, w), lambda b, t: (b, t, 0))
    vec = lambda n: pl.BlockSpec((1, n), lambda b, t: (0, 0))
    return pl.pallas_call(
        functools.partial(_rwkv_out_kernel, nct=nct, ng=ng),
        grid=(nb, ng),
        in_specs=[row, row, row, row,
                  pl.BlockSpec((1, TR, GATE_LORA), lambda b, t: (b, t, gcol)),
                  pl.BlockSpec((1, SUBLANES, GATE_LORA), lambda b, t: (b, jnp.maximum(t * hb - 1, 0), gcol)),
                  pl.BlockSpec((1, SUBLANES, GATE_LORA),
                               lambda b, t: (b, jnp.minimum((t + 1) * hb, n_row_blocks - 1), gcol)),
                  vec(GATE_LORA), pl.BlockSpec((GATE_LORA, w), lambda b, t: (0, 0)), vec(w), vec(w)],
        out_specs=row,
        out_shape=jax.ShapeDtypeStruct((nb, s, w), BF16),
        compiler_params=_cparams(("arbitrary", "arbitrary")),
        name="rwkv_out",
    )(o_f, o_b, bv_f, bv_b, pr, pr, pr, lp['mu'][SHIFT_WIDTH - GATE_LORA:].reshape(1, GATE_LORA), lp['g2'],
      lp['gn_w'].reshape(1, w), lp['gn_b'].reshape(1, w))


def _top2(logits):
    lane = lax.broadcasted_iota(jnp.int32, logits.shape, 1)
    v1 = jnp.max(logits, axis=-1, keepdims=True)
    i1 = jnp.min(jnp.where(logits == v1, lane, LANES), axis=-1, keepdims=True)
    rest = jnp.where(lane == i1, -jnp.inf, logits)
    v2 = jnp.max(rest, axis=-1, keepdims=True)
    i2 = jnp.min(jnp.where(rest == v2, lane, LANES), axis=-1, keepdims=True)
    e = jnp.exp(v2 - v1)
    g1 = 1.0 / (1.0 + e)
    g2 = e / (1.0 + e)
    out = jnp.where(lane == 0, i1.astype(F32), 0.0)
    out = jnp.where(lane == 1, i2.astype(F32), out)
    out = jnp.where(lane == 2, g1, out)
    return jnp.where(lane == 3, g2, out)


def _out_proj_kernel(x_ref, att_ref, rw_ref, mod_ref, g_ref, wa_ref, wr_ref, *rest, moe):
    if moe:
        wrt_ref, xo_ref, h_ref, route_ref = rest
    else:
        xo_ref, h_ref = rest
    mix = _dot(att_ref[0], wa_ref[...]) + _dot(rw_ref[0], wr_ref[...])
    x = x_ref[0] + mod_ref[0, 2:3] * mix
    xo_ref[0] = x
    h = _norm_mod(x, g_ref[...], mod_ref[0, 3:4], mod_ref[0, 4:5])
    h_ref[0] = h.astype(h_ref.dtype)
    if moe:
        lane = lax.broadcasted_iota(jnp.int32, (TR, LANES), 1)
        logits = jnp.where(lane < N_EXPERTS, _dot3(h, wrt_ref[...]), -jnp.inf)
        route_ref[0] = _top2(logits)


def _out_proj(xa, att, rw, mod, g, wo_a, wo_r, nct, w_router=None):
    nb, s, d = xa.shape
    moe = w_router is not None
    row = lambda n: pl.BlockSpec((1, TR, n), lambda b, t: (b, t, 0))
    full = lambda a: pl.BlockSpec(a.shape, lambda b, t: (0, 0))
    in_specs = [row(d), row(ATTN_WIDTH), row(RWKV_WIDTH), pl.BlockSpec((1, N_MOD, d), _mod_index(nb, nct)),
                pl.BlockSpec((1, d), lambda b, t: (0, 0)), full(wo_a), full(wo_r)]
    args = [xa, att, rw, mod, g.reshape(1, d), wo_a, wo_r]
    out_specs = [row(d), row(d)]
    out_shape = [jax.ShapeDtypeStruct((nb, s, d), F32), jax.ShapeDtypeStruct((nb, s, d), F32 if moe else BF16)]
    if moe:
        wrt = jnp.pad(w_router, ((0, 0), (0, LANES - N_EXPERTS)))
        in_specs.append(full(wrt))
        args.append(wrt)
        out_specs.append(row(LANES))
        out_shape.append(jax.ShapeDtypeStruct((nb, s, LANES), F32))
    return pl.pallas_call(
        functools.partial(_out_proj_kernel, moe=moe),
        grid=(nb, s // TR),
        in_specs=in_specs, out_specs=out_specs, out_shape=out_shape,
        compiler_params=_cparams(("arbitrary", "arbitrary")),
        name="out_proj_moe" if moe else "out_proj",
    )(*args)


def _ffn_kernel(x_ref, h_ref, mod_ref, wg_ref, wu_ref, wd_ref, o_ref):
    h = h_ref[0]
    gt = _dot(h, wg_ref[...])
    up = _dot(h, wu_ref[...])
    act = (gt * _sigmoid(gt) * up).astype(BF16)
    o_ref[0] = x_ref[0] + mod_ref[0, 5:6] * _dot(act, wd_ref[...])


def _ffn_dense(xa, h, mod, wg, wu, wd, nct):
    nb, s, d = xa.shape
    row = pl.BlockSpec((1, TR, d), lambda b, t: (b, t, 0))
    resident = lambda a: pl.BlockSpec(a.shape, lambda b, t: (0, 0), pipeline_mode=pl.Buffered(1))
    return pl.pallas_call(
        _ffn_kernel,
        grid=(nb, s // TR),
        in_specs=[row, row, pl.BlockSpec((1, N_MOD, d), _mod_index(nb, nct)), resident(wg), resident(wu),
                  resident(wd)],
        out_specs=row,
        out_shape=jax.ShapeDtypeStruct((nb, s, d), F32),
        compiler_params=_cparams(("arbitrary", "arbitrary")),
        name="ffn_dense",
    )(xa, h, mod, wg, wu, wd)


def _moe_kernel(be_ref, nu_ref, x_ref, wg_ref, wu_ref, wd_ref, o_ref):
    i = pl.program_id(0)
    j = pl.program_id(1)

    @pl.when(j == 0)
    def _():
        o_ref[...] = jnp.zeros_like(o_ref)

    @pl.when(i < nu_ref[0])
    def _():
        x = x_ref[...].astype(BF16)
        gt = _dot(x, wg_ref[0])
        up = _dot(x, wu_ref[0])
        act = (gt * _sigmoid(gt) * up).astype(BF16)
        o_ref[...] += _dot(act, wd_ref[0])


def _moe_experts(xs, block_expert, n_used, wg, wu, wd):
    n_slots, d = xs.shape
    dff = wg.shape[2]
    nblk = n_slots // MOE_BM
    grid_spec = pltpu.PrefetchScalarGridSpec(
        num_scalar_prefetch=2,
        grid=(nblk, dff // MOE_TF),
        in_specs=[pl.BlockSpec((MOE_BM, d), lambda i, j, be, nu: (i, 0)),
                  pl.BlockSpec((1, d, MOE_TF), lambda i, j, be, nu: (be[i], 0, j)),
                  pl.BlockSpec((1, d, MOE_TF), lambda i, j, be, nu: (be[i], 0, j)),
                  pl.BlockSpec((1, MOE_TF, d), lambda i, j, be, nu: (be[i], j, 0))],
        out_specs=pl.BlockSpec((MOE_BM, d), lambda i, j, be, nu: (i, 0)),
    )
    return pl.pallas_call(
        _moe_kernel,
        grid_spec=grid_spec,
        out_shape=jax.ShapeDtypeStruct((n_slots, d), F32),
        compiler_params=_cparams(("arbitrary", "arbitrary")),
        name="moe_experts",
    )(block_expert, n_used, xs, wg, wu, wd)


def _row_copy(src_hbm, idx_ref, dst_ref, sem, r):
    return pltpu.make_async_copy(src_hbm.at[pl.ds(idx_ref[0, 0, r], 1), :], dst_ref.at[pl.ds(r, 1), :], sem)


def _gather_rows(src_hbm, idx_ref, dst_ref, sem):
    n = dst_ref.shape[0]

    def start(r, carry):
        _row_copy(src_hbm, idx_ref, dst_ref, sem, r).start()
        return carry

    def wait(r, carry):
        _row_copy(src_hbm, idx_ref, dst_ref, sem, r).wait()
        return carry

    lax.fori_loop(0, n, start, 0, unroll=8)
    lax.fori_loop(0, n, wait, 0, unroll=8)


def _dispatch_kernel(tok_ref, h_hbm, o_ref, sem):
    _gather_rows(h_hbm, tok_ref, o_ref, sem)


def _moe_dispatch(h, slot_token):
    t, d = h.shape
    nblk = slot_token.shape[0] // MOE_BM
    return pl.pallas_call(
        _dispatch_kernel,
        grid=(nblk,),
        in_specs=[pl.BlockSpec((1, 1, MOE_BM), lambda i: (i, 0, 0), memory_space=pltpu.SMEM),
                  pl.BlockSpec(memory_space=pl.ANY)],
        out_specs=pl.BlockSpec((MOE_BM, d), lambda i: (i, 0)),
        out_shape=jax.ShapeDtypeStruct((nblk * MOE_BM, d), h.dtype),
        scratch_shapes=[pltpu.SemaphoreType.DMA(())],
        compiler_params=_cparams(("arbitrary",)),
        name="moe_dispatch",
    )(slot_token.reshape(nblk, 1, MOE_BM), h)


def _combine_kernel(s1_ref, s2_ref, x_ref, route_ref, mod_ref, ys_hbm, o_ref, y1_ref, y2_ref, sem):
    _gather_rows(ys_hbm, s1_ref, y1_ref, sem.at[0])
    _gather_rows(ys_hbm, s2_ref, y2_ref, sem.at[1])
    route = route_ref[0]
    f = y1_ref[...] * route[:, 2:3] + y2_ref[...] * route[:, 3:4]
    o_ref[0] = x_ref[0] + mod_ref[0, 5:6] * f


def _moe_combine(xa, route, mod, ys, slot1, slot2, nct):
    nb, s, d = xa.shape
    ng = s // TR
    idx = pl.BlockSpec((1, 1, TR), lambda b, t: (b * ng + t, 0, 0), memory_space=pltpu.SMEM)
    row = lambda n: pl.BlockSpec((1, TR, n), lambda b, t: (b, t, 0))
    return pl.pallas_call(
        _combine_kernel,
        grid=(nb, ng),
        in_specs=[idx, idx, row(d), row(LANES), pl.BlockSpec((1, N_MOD, d), _mod_index(nb, nct)),
                  pl.BlockSpec(memory_space=pl.ANY)],
        out_specs=row(d),
        out_shape=jax.ShapeDtypeStruct((nb, s, d), F32),
        scratch_shapes=[pltpu.VMEM((TR, d), F32), pltpu.VMEM((TR, d), F32), pltpu.SemaphoreType.DMA((2,))],
        compiler_params=_cparams(("arbitrary", "arbitrary")),
        name="moe_combine",
    )(slot1.reshape(nb * ng, 1, TR), slot2.reshape(nb * ng, 1, TR), xa, route, mod, ys)


def _moe_ffn(xa, h, route, mod, wg, wu, wd, nct):
    nb, s, d = xa.shape
    t = nb * s
    n_assign = 2 * t
    nblk = (n_assign + N_EXPERTS * (MOE_BM - 1) + MOE_BM - 1) // MOE_BM
    n_slots = nblk * MOE_BM
    expert = route.reshape(t, LANES)[:, 0:2].astype(jnp.int32).reshape(-1)
    onehot = (expert[:, None] == jnp.arange(N_EXPERTS, dtype=jnp.int32)[None, :]).astype(jnp.int32)
    rank = jnp.sum((jnp.cumsum(onehot, axis=0) - onehot) * onehot, axis=1)
    counts = jnp.sum(onehot, axis=0)
    padded = (counts + MOE_BM - 1) // MOE_BM * MOE_BM
    pad_end = jnp.cumsum(padded)
    pad_start = pad_end - padded
    slot = pad_start[expert] + rank
    token = jnp.repeat(jnp.arange(t, dtype=jnp.int32), 2)
    slot_token = jnp.zeros((n_slots,), jnp.int32).at[slot].set(token)
    block_start = jnp.arange(nblk, dtype=jnp.int32) * MOE_BM
    block_expert = jnp.minimum(jnp.sum((pad_end[None, :] <= block_start[:, None]).astype(jnp.int32), axis=1),
                               N_EXPERTS - 1)
    n_used = (pad_end[-1] // MOE_BM).astype(jnp.int32).reshape(1)
    xs = _moe_dispatch(h.reshape(t, d), slot_token)
    ys = _moe_experts(xs, block_expert, n_used, wg, wu, wd)
    slot2 = slot.reshape(t, 2)
    return _moe_combine(xa, route, mod, ys, slot2[:, 0], slot2[:, 1], nct)


def kernel(x, c, ctx, c_ctx, ada_w, ada_b, norm1_g, norm2_g, w_in, w_out, q_gain, k_gain, shift_mu, rw_w0, rw_w2,
           rw_a0, rw_a2, rw_g2, rw_kk, rw_ka, rw_rk, rw_gn_w, rw_gn_b, ffn_wg, ffn_wu, ffn_wd, moe_router, moe_wg,
           moe_wu, moe_wd):
    nb, l, d = x.shape
    lc = ctx.shape[1]
    depth = ada_w.shape[0]
    assert lc % TR == 0 and l % TR == 0 and l % GRID_W == 0
    nct = lc // TR
    xa = jnp.concatenate([ctx, x], axis=1)

    mod_rows = -(-(nb + 1) // SUBLANES) * SUBLANES
    c_all = jnp.concatenate([c, c_ctx[None], jnp.zeros((mod_rows - nb - 1, d), F32)], axis=0)
    mod_all = _ada_all(c_all, ada_w, ada_b)[:, :nb + 1].reshape(depth, nb + 1, N_MOD, d)
    cos, slo, shi = _rope_tables(lc, l)

    for i in range(depth):
        mod = mod_all[i]
        lp = dict(mu=shift_mu[i], w0=rw_w0[i], w2=rw_w2[i], a0=rw_a0[i], a2=rw_a2[i], g2=rw_g2[i], kk=rw_kk[i],
                  ka=rw_ka[i], rk=rw_rk[i], gn_w=rw_gn_w[i], gn_b=rw_gn_b[i])
        wi = w_in[i].astype(BF16)
        pa, pr = _in_proj(xa, mod, norm1_g[i], wi[:, :ATTN_IN], wi[:, ATTN_IN:], nct)
        q, k, v = _attn_prep(pa, cos, slo, shi, q_gain[i], k_gain[i])
        att = _attention(q, k, v, nct, lc)
        o_f, bv_f = _rwkv_scan(pr, lp, nct, False)
        o_b, bv_b = _rwkv_scan(pr, lp, nct, True)
        rw = _rwkv_out(o_f, o_b, bv_f, bv_b, pr, lp, nct)
        wo = w_out[i].astype(BF16)
        j = i // 2
        if i % 2 == 0:
            xa, h = _out_proj(xa, att, rw, mod, norm2_g[i], wo[:ATTN_WIDTH], wo[ATTN_WIDTH:], nct)
            xa = _ffn_dense(xa, h, mod, ffn_wg[j].astype(BF16), ffn_wu[j].astype(BF16), ffn_wd[j].astype(BF16), nct)
        else:
            xa, h, route = _out_proj(xa, att, rw, mod, norm2_g[i], wo[:ATTN_WIDTH], wo[ATTN_WIDTH:], nct,
                                     moe_router[j])
            xa = _moe_ffn(xa, h, route, mod, moe_wg[j].astype(BF16), moe_wu[j].astype(BF16), moe_wd[j].astype(BF16),
                          nct)
    return xa[:, lc:]
```

```python
import functools
import math

import jax
import jax.numpy as jnp
from jax import lax
from jax.experimental import pallas as pl
from jax.experimental.pallas import tpu as pltpu

F32 = jnp.float32
BF16 = jnp.bfloat16
HI = lax.Precision.HIGHEST

HEAD_DIM = 64
ROPE_HALF = HEAD_DIM // 2
ROPE_THETA = 10000.0
GRID_W = 64
ATTN_HEADS = 8
ATTN_KV_HEADS = 2
ATTN_GROUP = ATTN_HEADS // ATTN_KV_HEADS
ATTN_WIDTH = ATTN_HEADS * HEAD_DIM
KV_WIDTH = ATTN_KV_HEADS * HEAD_DIM
ATTN_IN = ATTN_WIDTH + 2 * KV_WIDTH
ATTN_SCALE = HEAD_DIM ** -0.5
LOG2E = math.log2(math.e)
RWKV_HEADS = 8
RWKV_WIDTH = RWKV_HEADS * HEAD_DIM
DECAY_LORA = 64
ICLR_LORA = 64
GATE_LORA = 128
SHIFT_WIDTH = 3 * RWKV_WIDTH + 2 * DECAY_LORA + 2 * ICLR_LORA + GATE_LORA
N_EXPERTS = 8
N_MOD = 6
EPS = 1e-6
GN_EPS = 64e-5

LANES = 128
SUBLANES = 8
TR = 256
CH = 128
PAIR = LANES // HEAD_DIM
MXU_DIM = 256
ATTN_KT = 256
MOE_BM = 512
MOE_TF = 1792
VMEM_LIMIT = 56 * 1024 * 1024


def _cparams(sem):
    return pltpu.CompilerParams(dimension_semantics=sem, vmem_limit_bytes=VMEM_LIMIT)


def _dot(a, b, prec=None):
    return jnp.dot(a, b, preferred_element_type=F32, precision=prec)


def _dot_nt(a, b, prec=None):
    return lax.dot_general(a, b, (((1,), (1,)), ((), ())), preferred_element_type=F32, precision=prec)


def _dot_tn(a, b, prec=None):
    return lax.dot_general(a, b, (((0,), (0,)), ((), ())), preferred_element_type=F32, precision=prec)


def _sigmoid(x):
    return 1.0 / (1.0 + jnp.exp(-x))


def _split2(x):
    hi = x.astype(BF16)
    return hi, (x - hi.astype(F32)).astype(BF16)


def _split3(x):
    hi = x.astype(BF16)
    r = x - hi.astype(F32)
    mid = r.astype(BF16)
    return hi, mid, (r - mid.astype(F32)).astype(BF16)


def _dot3(a, b):
    a_hi, a_lo = _split2(a)
    b_hi, b_lo = _split2(b)
    return _dot(a_hi, b_hi) + (_dot(a_hi, b_lo) + _dot(a_lo, b_hi))


def _head_sums(x, scale=1.0):
    r, n = x.shape
    i = lax.broadcasted_iota(jnp.int32, (MXU_DIM, MXU_DIM), 0) // HEAD_DIM
    j = lax.broadcasted_iota(jnp.int32, (MXU_DIM, MXU_DIM), 1) // HEAD_DIM
    ones = jnp.where(i == j, scale, 0.0).astype(BF16)
    nfull = n // MXU_DIM
    parts = []
    if nfull:
        xs = jnp.concatenate([x[:, c * MXU_DIM:(c + 1) * MXU_DIM] for c in range(nfull)], axis=0)
        m = nfull * r
        s = _dot(jnp.concatenate(_split2(xs), axis=0), ones)
        s = s[:m] + s[m:]
        parts += [s[c * r:(c + 1) * r] for c in range(nfull)]
    if n % MXU_DIM:
        w = n % MXU_DIM
        s = _dot(jnp.concatenate(_split2(x[:, nfull * MXU_DIM:]), axis=0), ones[:w, :w])
        parts.append(s[:r] + s[r:])
    return jnp.concatenate(parts, axis=1)


def _ada_kernel(c_ref, w_ref, b_ref, o_ref):
    c = c_ref[...]
    sc = c * _sigmoid(c)
    o_ref[0] = _dot(sc, w_ref[0], HI) + b_ref[0]


def _ada_all(c_all, ada_w, ada_b):
    depth, d, n = ada_w.shape
    rows = c_all.shape[0]
    tn = 1536
    return pl.pallas_call(
        _ada_kernel,
        grid=(depth, n // tn),
        in_specs=[pl.BlockSpec((rows, d), lambda i, j: (0, 0)),
                  pl.BlockSpec((1, d, tn), lambda i, j: (i, 0, j)),
                  pl.BlockSpec((1, 1, tn), lambda i, j: (i, 0, j))],
        out_specs=pl.BlockSpec((1, rows, tn), lambda i, j: (i, 0, j)),
        out_shape=jax.ShapeDtypeStruct((depth, rows, n), F32),
        compiler_params=_cparams(("arbitrary", "arbitrary")),
        name="ada_mod",
    )(c_all, ada_w, ada_b.reshape(depth, 1, n))


def _norm_mod(x, g, shift, scale):
    y = x * lax.rsqrt(jnp.mean(x * x, axis=-1, keepdims=True) + EPS)
    return (y * g) * (1.0 + scale) + shift


def _in_proj_kernel(x_ref, mod_ref, g_ref, wa_ref, wr_ref, pa_ref, pr_ref):
    h = _norm_mod(x_ref[0], g_ref[...], mod_ref[0, 0:1], mod_ref[0, 1:2]).astype(BF16)
    pa_ref[0] = _dot(h, wa_ref[...])
    pr_ref[0] = _dot(h, wr_ref[...])


def _mod_index(nb, nct):
    return lambda b, t: (jnp.where(t < nct, nb, b), 0, 0)


def _in_proj(xa, mod, g, wa, wr, nct):
    nb, s, d = xa.shape
    na, nr = wa.shape[1], wr.shape[1]
    return pl.pallas_call(
        _in_proj_kernel,
        grid=(nb, s // TR),
        in_specs=[pl.BlockSpec((1, TR, d), lambda b, t: (b, t, 0)),
                  pl.BlockSpec((1, N_MOD, d), _mod_index(nb, nct)),
                  pl.BlockSpec((1, d), lambda b, t: (0, 0)),
                  pl.BlockSpec((d, na), lambda b, t: (0, 0)),
                  pl.BlockSpec((d, nr), lambda b, t: (0, 0))],
        out_specs=[pl.BlockSpec((1, TR, na), lambda b, t: (b, t, 0)),
                   pl.BlockSpec((1, TR, nr), lambda b, t: (b, t, 0))],
        out_shape=[jax.ShapeDtypeStruct((nb, s, na), F32),
                   jax.ShapeDtypeStruct((nb, s, nr), F32)],
        compiler_params=_cparams(("arbitrary", "arbitrary")),
        name="in_proj",
    )(xa, mod, g.reshape(1, d), wa, wr)


def _rope(x, cos, sin_lo, sin_hi):
    n = x.shape[-1]
    return x * cos + pltpu.roll(x, n - ROPE_HALF, 1) * sin_lo + pltpu.roll(x, ROPE_HALF, 1) * sin_hi


def _attn_prep_kernel(pa_ref, cos_ref, slo_ref, shi_ref, qg_ref, kg_ref, q_ref, k_ref, v_ref):
    pa = pa_ref[0]
    qk = pa[:, :ATTN_WIDTH + KV_WIDTH]
    v = pa[:, ATTN_WIDTH + KV_WIDTH:ATTN_IN]
    cos, slo, shi = cos_ref[...], slo_ref[...], shi_ref[...]
    inv = lax.rsqrt(_head_sums(qk * qk, 1.0 / HEAD_DIM) + EPS)
    qn = qk[:, :ATTN_WIDTH] * inv[:, :ATTN_WIDTH] * qg_ref[...]
    qr = _rope(qn, cos, slo, shi) * (ATTN_SCALE * LOG2E)
    kn = qk[:, ATTN_WIDTH:] * inv[:, ATTN_WIDTH:] * kg_ref[...]
    kr = _rope(kn, cos[:, :KV_WIDTH], slo[:, :KV_WIDTH], shi[:, :KV_WIDTH])
    lane = lax.broadcasted_iota(jnp.int32, (TR, LANES), 1)
    low = lane < HEAD_DIM
    for j in range(ATTN_KV_HEADS):
        kj = kr if j % PAIR == 0 else pltpu.roll(kr, HEAD_DIM, 1)
        vj = v if j % PAIR == 0 else pltpu.roll(v, HEAD_DIM, 1)
        k_ref[0, j] = jnp.where(low, kj, 0.0).astype(BF16)
        v_ref[0, j] = jnp.where(low, vj, jnp.where(lane == HEAD_DIM, 1.0, 0.0)).astype(BF16)
    for h in range(ATTN_HEADS):
        src = qr[:, (h // PAIR) * LANES:(h // PAIR + 1) * LANES]
        if h % PAIR:
            src = pltpu.roll(src, HEAD_DIM, 1)
        q_ref[0, h] = jnp.where(low, src, 0.0).astype(BF16)


def _attn_prep(pa, cos, slo, shi, q_gain, k_gain):
    nb, s, _ = pa.shape
    qg = jnp.tile(q_gain, ATTN_HEADS).reshape(1, ATTN_WIDTH)
    kg = jnp.tile(k_gain, ATTN_KV_HEADS).reshape(1, KV_WIDTH)
    tab = pl.BlockSpec((TR, ATTN_WIDTH), lambda t, b: (t, 0))
    return pl.pallas_call(
        _attn_prep_kernel,
        grid=(s // TR, nb),
        in_specs=[pl.BlockSpec((1, TR, ATTN_IN), lambda t, b: (b, t, 0)), tab, tab, tab,
                  pl.BlockSpec((1, ATTN_WIDTH), lambda t, b: (0, 0)),
                  pl.BlockSpec((1, KV_WIDTH), lambda t, b: (0, 0))],
        out_specs=[pl.BlockSpec((1, ATTN_HEADS, TR, LANES), lambda t, b: (b, 0, t, 0)),
                   pl.BlockSpec((1, ATTN_KV_HEADS, TR, LANES), lambda t, b: (b, 0, t, 0)),
                   pl.BlockSpec((1, ATTN_KV_HEADS, TR, LANES), lambda t, b: (b, 0, t, 0))],
        out_shape=[jax.ShapeDtypeStruct((nb, ATTN_HEADS, s, LANES), BF16),
                   jax.ShapeDtypeStruct((nb, ATTN_KV_HEADS, s, LANES), BF16),
                   jax.ShapeDtypeStruct((nb, ATTN_KV_HEADS, s, LANES), BF16)],
        compiler_params=_cparams(("arbitrary", "arbitrary")),
        name="attn_prep",
    )(pa, cos, slo, shi, qg, kg)


def _rope_tables(lc, l):
    rows = l // GRID_W
    row = jnp.repeat(jnp.arange(rows, dtype=F32), GRID_W)
    col = jnp.tile(jnp.arange(GRID_W, dtype=F32), rows)
    inv = ROPE_THETA ** (-jnp.arange(0, ROPE_HALF, 2, dtype=F32) / ROPE_HALF)
    ang = jnp.concatenate([row[:, None] * inv, col[:, None] * inv], axis=-1)
    cos = jnp.concatenate([jnp.ones((lc, ROPE_HALF), F32), jnp.cos(ang)], axis=0)
    sin = jnp.concatenate([jnp.zeros((lc, ROPE_HALF), F32), jnp.sin(ang)], axis=0)
    zero = jnp.zeros_like(sin)
    head = lambda lo, hi: jnp.tile(jnp.concatenate([lo, hi], axis=-1), (1, ATTN_HEADS))
    return head(cos, cos), head(-sin, zero), head(zero, sin)


def _attn_kernel(q_ref, k_ref, v_ref, o_ref, s_ref, *, nct, lc):
    t = pl.program_id(2)
    n_all = k_ref.shape[2]
    low = lax.broadcasted_iota(jnp.int32, (TR, LANES), 1) < HEAD_DIM

    def run(n_keys):
        tiles = [slice(j * ATTN_KT, (j + 1) * ATTN_KT) for j in range(n_keys // ATTN_KT)]
        mx, acc = [None] * ATTN_GROUP, [None] * ATTN_GROUP
        m = [None] * ATTN_GROUP
        for stage in range(ATTN_GROUP + 1):
            g1, g2 = stage, stage - 1
            if g1 < ATTN_GROUP:
                mx[g1] = jnp.full((TR, LANES), -jnp.inf, F32)
            if g2 >= 0:
                m[g2] = jnp.max(mx[g2], axis=-1, keepdims=True)
                acc[g2] = jnp.zeros((TR, LANES), F32)
            for ks in tiles:
                if g1 < ATTN_GROUP:
                    s = _dot_nt(q_ref[0, g1], k_ref[0, 0, ks, :])
                    s_ref[g1, :, ks] = s
                    for c in range(ATTN_KT // LANES):
                        mx[g1] = jnp.maximum(mx[g1], s[:, c * LANES:(c + 1) * LANES])
                if g2 >= 0:
                    p = jnp.exp2(s_ref[g2, :, ks] - m[g2]).astype(BF16)
                    acc[g2] = acc[g2] + _dot(p, v_ref[0, 0, ks, :])
        out = [a / a[:, HEAD_DIM:HEAD_DIM + 1] for a in acc]
        for pr in range(ATTN_GROUP // PAIR):
            hi = pltpu.roll(out[PAIR * pr + 1], HEAD_DIM, 1)
            o_ref[0, :, pr * LANES:(pr + 1) * LANES] = jnp.where(low, out[PAIR * pr], hi).astype(BF16)

    @pl.when(t < nct)
    def _():
        run(lc)

    @pl.when(t >= nct)
    def _():
        run(n_all)


def _attention(q, k, v, nct, lc):
    nb, _, s, _ = q.shape
    gw = ATTN_GROUP * HEAD_DIM
    kv_spec = pl.BlockSpec((1, 1, s, LANES), lambda b, j, t: (b, j, 0, 0))
    return pl.pallas_call(
        functools.partial(_attn_kernel, nct=nct, lc=lc),
        grid=(nb, ATTN_KV_HEADS, s // TR),
        in_specs=[pl.BlockSpec((1, ATTN_GROUP, TR, LANES), lambda b, j, t: (b, j, t, 0)), kv_spec, kv_spec],
        out_specs=pl.BlockSpec((1, TR, gw), lambda b, j, t: (b, t, j)),
        out_shape=jax.ShapeDtypeStruct((nb, s, ATTN_WIDTH), BF16),
        scratch_shapes=[pltpu.VMEM((ATTN_GROUP, TR, s), F32)],
        compiler_params=_cparams(("arbitrary", "arbitrary", "arbitrary")),
        name="attention",
    )(q, k, v)


def _token_shift(p, prev_row, next_row, mu):
    n = p.shape[0]
    row = lax.broadcasted_iota(jnp.int32, p.shape, 0)
    prev = jnp.where(row == 0, prev_row, pltpu.roll(p, 1, 0))
    nxt = jnp.where(row == n - 1, next_row, pltpu.roll(p, n - 1, 0))
    return p + mu * (0.5 * (prev + nxt) - p)


def _halo_rows(prev_ref, next_ref, tile, nct, ng):
    first = jnp.logical_or(tile == 0, tile == nct)
    last = jnp.logical_or(tile == nct - 1, tile == ng - 1)
    prev_row = jnp.where(first, 0.0, prev_ref[0, SUBLANES - 1:SUBLANES, :])
    next_row = jnp.where(last, 0.0, next_ref[0, 0:1, :])
    return prev_row, next_row


def _rwkv_tile(rev, g, nct, ng):
    if not rev:
        return g
    return jnp.where(g < nct, nct - 1 - g, ng - 1 - (g - nct))


def _block_diag(xs):
    z = jnp.zeros((CH, CH), xs.dtype)
    return jnp.concatenate([jnp.concatenate([xs[:CH], z], axis=1), jnp.concatenate([z, xs[CH:]], axis=1)], axis=0)


def _rwkv_dir_kernel(p_ref, prev_ref, next_ref, mu_ref, kkw_ref, ka_ref, rk_ref, w0_ref, a0_ref, w2_ref, a2_ref,
                     o_ref, bv_ref, s_ref, *, rev, nct, ng):
    g = pl.program_id(1)
    tile = _rwkv_tile(rev, g, nct, ng)
    w = RWKV_WIDTH

    @pl.when(g == 0)
    def _():
        s_ref[...] = jnp.zeros_like(s_ref)

    prev_row, next_row = _halo_rows(prev_ref, next_ref, tile, nct, ng)
    ps = _token_shift(p_ref[0], prev_row, next_row, mu_ref[...])
    r, k, v = ps[:, :w], ps[:, w:2 * w], ps[:, 2 * w:3 * w]
    wd = ps[:, 3 * w:3 * w + 2 * DECAY_LORA]
    ad = ps[:, 3 * w + 2 * DECAY_LORA:3 * w + 2 * DECAY_LORA + 2 * ICLR_LORA]
    kk = k * kkw_ref[...]
    kk = kk * lax.rsqrt(_head_sums(kk * kk) + 1e-12)
    x = w0_ref[...] + _dot3(jnp.tanh(wd), w2_ref[...])
    lw = (-math.exp(-0.5)) * _sigmoid(x)
    a = _sigmoid(a0_ref[...] + _dot3(ad, a2_ref[...]))
    k_d = k * (1.0 + (a - 1.0) * ka_ref[...])
    bv_ref[0] = _head_sums(r * k_d * rk_ref[...]) * v
    beta = kk * a

    ti = lax.broadcasted_iota(jnp.int32, (PAIR * CH, CH), 0) & (CH - 1)
    tj = lax.broadcasted_iota(jnp.int32, (PAIR * CH, CH), 1)
    prec2 = (tj > ti) if rev else (tj < ti)
    eye2 = ti == tj
    incl2 = jnp.logical_or(prec2, eye2)
    late, early = (tj, ti) if rev else (ti, tj)
    join = [jnp.logical_and(jnp.logical_and((ti ^ tj) < 2 * b, (late & b) != 0), (early & b) == 0)
            for b in (2 ** m for m in range(int(math.log2(CH))))]
    si = lax.broadcasted_iota(jnp.int32, (CH, CH), 0)
    sj = lax.broadcasted_iota(jnp.int32, (CH, CH), 1)
    eye = si == sj
    tri = jnp.where((sj >= si) if rev else (sj <= si), 1.0, 0.0).astype(BF16)
    same_head = (si // HEAD_DIM) == (sj // HEAD_DIM)
    lane = lax.broadcasted_iota(jnp.int32, (CH, LANES), 1)
    head0 = lane < HEAD_DIM
    sel = lambda x0, x1: jnp.where(head0, x0, x1)
    n_chunks = TR // CH
    n_pairs = RWKV_HEADS // PAIR
    chunk_order = list(range(n_chunks - 1, -1, -1) if rev else range(n_chunks))
    units = [(ci, pp) for ci in chunk_order for pp in range(n_pairs)]
    each = lambda f, *lists: [f(*args) for args in zip(*lists)]

    cum = {}
    for ci in chunk_order:
        cs = _dot(tri, jnp.concatenate(_split3(lw[ci * CH:(ci + 1) * CH]), axis=1))
        cum[ci] = cs[:, :w] + cs[:, w:2 * w] + cs[:, 2 * w:]
    blk = lambda x: [x[ci * CH:(ci + 1) * CH, pp * LANES:(pp + 1) * LANES] for ci, pp in units]
    rx, vx, kap, lwx, kx, bet = blk(r), blk(v), blk(kk), blk(lw), blk(k_d), blk(beta)
    cumi = [cum[ci][:, pp * LANES:(pp + 1) * LANES] for ci, pp in units]
    tot = each(lambda c: c[0:1, :] if rev else c[CH - 1:CH, :], cumi)
    cc = each(lambda c: c - c[CH // 2:CH // 2 + 1, :], cumi)
    e_neg = each(lambda c: jnp.exp(-c), cc)
    e_end = each(lambda t, c: jnp.exp(t - c), tot, cumi)
    kap_t = each(lambda x, c, l: x * jnp.exp(c - l), kap, cc, lwx)
    r_t = each(lambda x, c: x * jnp.exp(c), rx, cc)
    kap_0 = each(lambda x, c, l: x * jnp.exp(c - l), kap, cumi, lwx)
    r_0 = each(lambda x, c: x * jnp.exp(c), rx, cumi)
    k_t = each(lambda x, e: (x * e).astype(BF16), kx, e_neg)
    bet_t = each(lambda x, e: (x * e).astype(BF16), bet, e_neg)
    k_e = each(lambda x, e: x * e, kx, e_end)
    bet_e = each(lambda x, e: x * e, bet, e_end)
    p_end = each(jnp.exp, tot)
    qs = each(lambda a, b: jnp.concatenate([jnp.where(head0, a, 0.0), jnp.where(head0, 0.0, a),
                                            jnp.where(head0, b, 0.0), jnp.where(head0, 0.0, b)],
                                           axis=0).astype(BF16), kap_t, r_t)
    s_b = each(_dot_nt, qs, bet_t)
    s_k = each(_dot_nt, qs, k_t)
    a_b = each(lambda s: jnp.where(prec2, s[:PAIR * CH], 0.0), s_b)
    a_k = each(lambda s: jnp.where(prec2, s[:PAIR * CH], 0.0).astype(BF16), s_k)
    b_b = each(lambda s: jnp.where(incl2, s[PAIR * CH:], 0.0).astype(BF16), s_b)
    b_k = each(lambda s: jnp.where(incl2, s[PAIR * CH:], 0.0).astype(BF16), s_k)
    ts = each(lambda a: jnp.where(eye2, 1.0, jnp.where(join[0], -a, 0.0)), a_b)
    for lvl in range(1, len(join)):
        q_b = each(lambda a: jnp.where(join[lvl], a, 0.0).astype(BF16), a_b)
        t_b = each(lambda t: t.astype(BF16), ts)
        dq = each(lambda t, q: _dot(_block_diag(t), q).astype(BF16), t_b, q_b)
        ts = each(lambda t, e, tb: t - _dot(_block_diag(e), tb), ts, dq, t_b)
    t_bd = each(lambda t: _block_diag(t.astype(BF16)), ts)
    vb2 = each(lambda x: jnp.concatenate([x.astype(BF16)] * PAIR, axis=0), vx)
    av2 = each(lambda a, x: _dot(_block_diag(a), x), a_k, vb2)
    rhs = each(lambda k0, a: jnp.concatenate([k0, sel(a[:CH], a[CH:])], axis=1).astype(BF16), kap_0, av2)
    wz = each(lambda t, x: _dot(t, jnp.concatenate([x] * PAIR, axis=0)), t_bd, rhs)
    kq = each(lambda x: sel(x[:CH, :LANES], x[CH:, :LANES]), wz)
    z0 = each(lambda x: sel(x[:CH, LANES:], x[CH:, LANES:]), wz)
    kz = each(lambda a, b: jnp.concatenate([a, b], axis=1).astype(BF16), kq, z0)
    bz = each(lambda b, x: _dot(_block_diag(b), jnp.concatenate([x] * PAIR, axis=0)), b_b, kz)
    o0k = each(lambda b, x: _dot(_block_diag(b), x), b_k, vb2)
    r_new = each(lambda x, b: x - sel(b[:CH, :LANES], b[CH:, :LANES]), r_0, bz)
    o0 = each(lambda a, b: sel(a[:CH], a[CH:]) - sel(b[:CH, LANES:], b[CH:, LANES:]), o0k, bz)
    m_c = each(lambda p, b, q: jnp.where(eye, p, 0.0) - jnp.where(same_head, _dot_tn(b, q), 0.0), p_end, bet_e, kq)
    g_c = each(lambda ke, be, x, z: jnp.where(same_head, _dot_tn(jnp.concatenate([ke, be], axis=0),
                                                                  jnp.concatenate([x, -z], axis=0)), 0.0),
               k_e, bet_e, vx, z0)

    states = [s_ref[pp] for pp in range(n_pairs)]
    outs = {}
    for u, (ci, pp) in enumerate(units):
        st = states[pp]
        outs[ci, pp] = _dot3(r_new[u], st) + o0[u]
        states[pp] = _dot3(m_c[u], st) + g_c[u]
    for ci in chunk_order:
        o_ref[0, ci * CH:(ci + 1) * CH, :] = jnp.concatenate([outs[ci, pp] for pp in range(n_pairs)], axis=1)
    for pp in range(n_pairs):
        s_ref[pp] = states[pp]


def _rwkv_scan(pr, lp, nct, rev):
    nb, s, pw = pr.shape
    ng = s // TR
    w = RWKV_WIDTH
    hb = TR // SUBLANES
    d = int(rev)
    tile = functools.partial(_rwkv_tile, rev, nct=nct, ng=ng)
    n_row_blocks = s // SUBLANES
    main = pl.BlockSpec((1, TR, pw), lambda b, g: (b, tile(g), 0))
    prev = pl.BlockSpec((1, SUBLANES, pw), lambda b, g: (b, jnp.maximum(tile(g) * hb - 1, 0), 0))
    nxt = pl.BlockSpec((1, SUBLANES, pw), lambda b, g: (b, jnp.minimum((tile(g) + 1) * hb, n_row_blocks - 1), 0))
    vec = lambda n: pl.BlockSpec((1, n), lambda b, g: (0, 0))
    mat = pl.BlockSpec((2 * DECAY_LORA, w), lambda b, g: (0, 0))
    out = pl.BlockSpec((1, TR, w), lambda b, g: (b, tile(g), 0))
    zeros = jnp.zeros((DECAY_LORA, w), F32)
    pad = lambda m: jnp.concatenate([zeros, m] if rev else [m, zeros])
    return pl.pallas_call(
        functools.partial(_rwkv_dir_kernel, rev=rev, nct=nct, ng=ng),
        grid=(nb, ng),
        in_specs=[main, prev, nxt, vec(pw), vec(w), vec(w), vec(w), vec(w), vec(w), mat, mat],
        out_specs=[out, out],
        out_shape=[jax.ShapeDtypeStruct((nb, s, w), F32), jax.ShapeDtypeStruct((nb, s, w), F32)],
        scratch_shapes=[pltpu.VMEM((RWKV_HEADS // PAIR, LANES, LANES), F32)],
        compiler_params=_cparams(("arbitrary", "arbitrary")),
        name="rwkv_scan_bwd" if rev else "rwkv_scan_fwd",
    )(pr, pr, pr, lp['mu'].reshape(1, pw), lp['kk'].reshape(1, w), lp['ka'].reshape(1, w), lp['rk'].reshape(1, w),
      lp['w0'][d].reshape(1, w), lp['a0'][d].reshape(1, w), pad(lp['w2'][d]), pad(lp['a2'][d]))


def _rwkv_out_kernel(of_ref, ob_ref, bf_ref, bb_ref, gd_ref, gprev_ref, gnext_ref, mu_ref, g2_ref, gw_ref, gb_ref,
                     y_ref, *, nct, ng):
    tile = pl.program_id(1)
    o = of_ref[0] + ob_ref[0]
    cen = o - _head_sums(o, 1.0 / HEAD_DIM)
    var = _head_sums(cen * cen, 1.0 / HEAD_DIM)
    y = cen * lax.rsqrt(var + GN_EPS) * gw_ref[...] + gb_ref[...] + (bf_ref[0] + bb_ref[0])
    prev_row, next_row = _halo_rows(gprev_ref, gnext_ref, tile, nct, ng)
    gd = _token_shift(gd_ref[0], prev_row, next_row, mu_ref[...])
    gate = _dot3(_sigmoid(gd), g2_ref[...])
    y_ref[0] = (y * gate).astype(BF16)


def _rwkv_out(o_f, o_b, bv_f, bv_b, pr, lp, nct):
    nb, s, w = o_f.shape
    ng = s // TR
    hb = TR // SUBLANES
    gcol = (SHIFT_WIDTH - GATE_LORA) // GATE_LORA
    n_row_blocks = s // SUBLANES
    row = pl.BlockSpec((1, TR, w), lambda b, t: (b, t, 0))
    vec = lambda n: pl.BlockSpec((1, n), lambda b, t: (0, 0))
    return pl.pallas_call(
        functools.partial(_rwkv_out_kernel, nct=nct, ng=ng),
        grid=(nb, ng),
        in_specs=[row, row, row, row,
                  pl.BlockSpec((1, TR, GATE_LORA), lambda b, t: (b, t, gcol)),
                  pl.BlockSpec((1, SUBLANES, GATE_LORA), lambda b, t: (b, jnp.maximum(t * hb - 1, 0), gcol)),
                  pl.BlockSpec((1, SUBLANES, GATE_LORA),
                               lambda b, t: (b, jnp.minimum((t + 1) * hb, n_row_blocks - 1), gcol)),
                  vec(GATE_LORA), pl.BlockSpec((GATE_LORA, w), lambda b, t: (0, 0)), vec(w), vec(w)],
        out_specs=row,
        out_shape=jax.ShapeDtypeStruct((nb, s, w), BF16),
        compiler_params=_cparams(("arbitrary", "arbitrary")),
        name="rwkv_out",
    )(o_f, o_b, bv_f, bv_b, pr, pr, pr, lp['mu'][SHIFT_WIDTH - GATE_LORA:].reshape(1, GATE_LORA), lp['g2'],
      lp['gn_w'].reshape(1, w), lp['gn_b'].reshape(1, w))


def _top2(logits):
    lane = lax.broadcasted_iota(jnp.int32, logits.shape, 1)
    v1 = jnp.max(logits, axis=-1, keepdims=True)
    i1 = jnp.min(jnp.where(logits == v1, lane, LANES), axis=-1, keepdims=True)
    rest = jnp.where(lane == i1, -jnp.inf, logits)
    v2 = jnp.max(rest, axis=-1, keepdims=True)
    i2 = jnp.min(jnp.where(rest == v2, lane, LANES), axis=-1, keepdims=True)
    e = jnp.exp(v2 - v1)
    g1 = 1.0 / (1.0 + e)
    g2 = e / (1.0 + e)
    out = jnp.where(lane == 0, i1.astype(F32), 0.0)
    out = jnp.where(lane == 1, i2.astype(F32), out)
    out = jnp.where(lane == 2, g1, out)
    return jnp.where(lane == 3, g2, out)


def _out_proj_kernel(x_ref, att_ref, rw_ref, mod_ref, g_ref, wa_ref, wr_ref, *rest, moe):
    if moe:
        wrt_ref, xo_ref, h_ref, route_ref = rest
    else:
        xo_ref, h_ref = rest
    mix = _dot(att_ref[0], wa_ref[...]) + _dot(rw_ref[0], wr_ref[...])
    x = x_ref[0] + mod_ref[0, 2:3] * mix
    xo_ref[0] = x
    h = _norm_mod(x, g_ref[...], mod_ref[0, 3:4], mod_ref[0, 4:5])
    h_ref[0] = h.astype(h_ref.dtype)
    if moe:
        lane = lax.broadcasted_iota(jnp.int32, (TR, LANES), 1)
        logits = jnp.where(lane < N_EXPERTS, _dot3(h, wrt_ref[...]), -jnp.inf)
        route_ref[0] = _top2(logits)


def _out_proj(xa, att, rw, mod, g, wo_a, wo_r, nct, w_router=None):
    nb, s, d = xa.shape
    moe = w_router is not None
    row = lambda n: pl.BlockSpec((1, TR, n), lambda b, t: (b, t, 0))
    full = lambda a: pl.BlockSpec(a.shape, lambda b, t: (0, 0))
    in_specs = [row(d), row(ATTN_WIDTH), row(RWKV_WIDTH), pl.BlockSpec((1, N_MOD, d), _mod_index(nb, nct)),
                pl.BlockSpec((1, d), lambda b, t: (0, 0)), full(wo_a), full(wo_r)]
    args = [xa, att, rw, mod, g.reshape(1, d), wo_a, wo_r]
    out_specs = [row(d), row(d)]
    out_shape = [jax.ShapeDtypeStruct((nb, s, d), F32), jax.ShapeDtypeStruct((nb, s, d), F32 if moe else BF16)]
    if moe:
        wrt = jnp.pad(w_router, ((0, 0), (0, LANES - N_EXPERTS)))
        in_specs.append(full(wrt))
        args.append(wrt)
        out_specs.append(row(LANES))
        out_shape.append(jax.ShapeDtypeStruct((nb, s, LANES), F32))
    return pl.pallas_call(
        functools.partial(_out_proj_kernel, moe=moe),
        grid=(nb, s // TR),
        in_specs=in_specs, out_specs=out_specs, out_shape=out_shape,
        compiler_params=_cparams(("arbitrary", "arbitrary")),
        name="out_proj_moe" if moe else "out_proj",
    )(*args)


def _ffn_kernel(x_ref, h_ref, mod_ref, wg_ref, wu_ref, wd_ref, o_ref):
    h = h_ref[0]
    gt = _dot(h, wg_ref[...])
    up = _dot(h, wu_ref[...])
    act = (gt * _sigmoid(gt) * up).astype(BF16)
    o_ref[0] = x_ref[0] + mod_ref[0, 5:6] * _dot(act, wd_ref[...])


def _ffn_dense(xa, h, mod, wg, wu, wd, nct):
    nb, s, d = xa.shape
    row = pl.BlockSpec((1, TR, d), lambda b, t: (b, t, 0))
    resident = lambda a: pl.BlockSpec(a.shape, lambda b, t: (0, 0), pipeline_mode=pl.Buffered(1))
    return pl.pallas_call(
        _ffn_kernel,
        grid=(nb, s // TR),
        in_specs=[row, row, pl.BlockSpec((1, N_MOD, d), _mod_index(nb, nct)), resident(wg), resident(wu),
                  resident(wd)],
        out_specs=row,
        out_shape=jax.ShapeDtypeStruct((nb, s, d), F32),
        compiler_params=_cparams(("arbitrary", "arbitrary")),
        name="ffn_dense",
    )(xa, h, mod, wg, wu, wd)


def _moe_kernel(be_ref, nu_ref, x_ref, wg_ref, wu_ref, wd_ref, o_ref):
    i = pl.program_id(0)
    j = pl.program_id(1)

    @pl.when(j == 0)
    def _():
        o_ref[...] = jnp.zeros_like(o_ref)

    @pl.when(i < nu_ref[0])
    def _():
        x = x_ref[...].astype(BF16)
        gt = _dot(x, wg_ref[0])
        up = _dot(x, wu_ref[0])
        act = (gt * _sigmoid(gt) * up).astype(BF16)
        o_ref[...] += _dot(act, wd_ref[0])


def _moe_experts(xs, block_expert, n_used, wg, wu, wd):
    n_slots, d = xs.shape
    dff = wg.shape[2]
    nblk = n_slots // MOE_BM
    grid_spec = pltpu.PrefetchScalarGridSpec(
        num_scalar_prefetch=2,
        grid=(nblk, dff // MOE_TF),
        in_specs=[pl.BlockSpec((MOE_BM, d), lambda i, j, be, nu: (i, 0)),
                  pl.BlockSpec((1, d, MOE_TF), lambda i, j, be, nu: (be[i], 0, j)),
                  pl.BlockSpec((1, d, MOE_TF), lambda i, j, be, nu: (be[i], 0, j)),
                  pl.BlockSpec((1, MOE_TF, d), lambda i, j, be, nu: (be[i], j, 0))],
        out_specs=pl.BlockSpec((MOE_BM, d), lambda i, j, be, nu: (i, 0)),
    )
    return pl.pallas_call(
        _moe_kernel,
        grid_spec=grid_spec,
        out_shape=jax.ShapeDtypeStruct((n_slots, d), F32),
        compiler_params=_cparams(("arbitrary", "arbitrary")),
        name="moe_experts",
    )(block_expert, n_used, xs, wg, wu, wd)


def _row_copy(src_hbm, idx_ref, dst_ref, sem, r):
    return pltpu.make_async_copy(src_hbm.at[pl.ds(idx_ref[0, 0, r], 1), :], dst_ref.at[pl.ds(r, 1), :], sem)


def _gather_rows(src_hbm, idx_ref, dst_ref, sem):
    n = dst_ref.shape[0]

    def start(r, carry):
        _row_copy(src_hbm, idx_ref, dst_ref, sem, r).start()
        return carry

    def wait(r, carry):
        _row_copy(src_hbm, idx_ref, dst_ref, sem, r).wait()
        return carry

    lax.fori_loop(0, n, start, 0, unroll=8)
    lax.fori_loop(0, n, wait, 0, unroll=8)


def _dispatch_kernel(tok_ref, h_hbm, o_ref, sem):
    _gather_rows(h_hbm, tok_ref, o_ref, sem)


def _moe_dispatch(h, slot_token):
    t, d = h.shape
    nblk = slot_token.shape[0] // MOE_BM
    return pl.pallas_call(
        _dispatch_kernel,
        grid=(nblk,),
        in_specs=[pl.BlockSpec((1, 1, MOE_BM), lambda i: (i, 0, 0), memory_space=pltpu.SMEM),
                  pl.BlockSpec(memory_space=pl.ANY)],
        out_specs=pl.BlockSpec((MOE_BM, d), lambda i: (i, 0)),
        out_shape=jax.ShapeDtypeStruct((nblk * MOE_BM, d), h.dtype),
        scratch_shapes=[pltpu.SemaphoreType.DMA(())],
        compiler_params=_cparams(("arbitrary",)),
        name="moe_dispatch",
    )(slot_token.reshape(nblk, 1, MOE_BM), h)


def _combine_kernel(s1_ref, s2_ref, x_ref, route_ref, mod_ref, ys_hbm, o_ref, y1_ref, y2_ref, sem):
    _gather_rows(ys_hbm, s1_ref, y1_ref, sem.at[0])
    _gather_rows(ys_hbm, s2_ref, y2_ref, sem.at[1])
    route = route_ref[0]
    f = y1_ref[...] * route[:, 2:3] + y2_ref[...] * route[:, 3:4]
    o_ref[0] = x_ref[0] + mod_ref[0, 5:6] * f


def _moe_combine(xa, route, mod, ys, slot1, slot2, nct):
    nb, s, d = xa.shape
    ng = s // TR
    idx = pl.BlockSpec((1, 1, TR), lambda b, t: (b * ng + t, 0, 0), memory_space=pltpu.SMEM)
    row = lambda n: pl.BlockSpec((1, TR, n), lambda b, t: (b, t, 0))
    return pl.pallas_call(
        _combine_kernel,
        grid=(nb, ng),
        in_specs=[idx, idx, row(d), row(LANES), pl.BlockSpec((1, N_MOD, d), _mod_index(nb, nct)),
                  pl.BlockSpec(memory_space=pl.ANY)],
        out_specs=row(d),
        out_shape=jax.ShapeDtypeStruct((nb, s, d), F32),
        scratch_shapes=[pltpu.VMEM((TR, d), F32), pltpu.VMEM((TR, d), F32), pltpu.SemaphoreType.DMA((2,))],
        compiler_params=_cparams(("arbitrary", "arbitrary")),
        name="moe_combine",
    )(slot1.reshape(nb * ng, 1, TR), slot2.reshape(nb * ng, 1, TR), xa, route, mod, ys)


def _moe_ffn(xa, h, route, mod, wg, wu, wd, nct):
    nb, s, d = xa.shape
    t = nb * s
    n_assign = 2 * t
    nblk = (n_assign + N_EXPERTS * (MOE_BM - 1) + MOE_BM - 1) // MOE_BM
    n_slots = nblk * MOE_BM
    expert = route.reshape(t, LANES)[:, 0:2].astype(jnp.int32).reshape(-1)
    onehot = (expert[:, None] == jnp.arange(N_EXPERTS, dtype=jnp.int32)[None, :]).astype(jnp.int32)
    rank = jnp.sum((jnp.cumsum(onehot, axis=0) - onehot) * onehot, axis=1)
    counts = jnp.sum(onehot, axis=0)
    padded = (counts + MOE_BM - 1) // MOE_BM * MOE_BM
    pad_end = jnp.cumsum(padded)
    pad_start = pad_end - padded
    slot = pad_start[expert] + rank
    token = jnp.repeat(jnp.arange(t, dtype=jnp.int32), 2)
    slot_token = jnp.zeros((n_slots,), jnp.int32).at[slot].set(token)
    block_start = jnp.arange(nblk, dtype=jnp.int32) * MOE_BM
    block_expert = jnp.minimum(jnp.sum((pad_end[None, :] <= block_start[:, None]).astype(jnp.int32), axis=1),
                               N_EXPERTS - 1)
    n_used = (pad_end[-1] // MOE_BM).astype(jnp.int32).reshape(1)
    xs = _moe_dispatch(h.reshape(t, d), slot_token)
    ys = _moe_experts(xs, block_expert, n_used, wg, wu, wd)
    slot2 = slot.reshape(t, 2)
    return _moe_combine(xa, route, mod, ys, slot2[:, 0], slot2[:, 1], nct)


def kernel(x, c, ctx, c_ctx, ada_w, ada_b, norm1_g, norm2_g, w_in, w_out, q_gain, k_gain, shift_mu, rw_w0, rw_w2,
           rw_a0, rw_a2, rw_g2, rw_kk, rw_ka, rw_rk, rw_gn_w, rw_gn_b, ffn_wg, ffn_wu, ffn_wd, moe_router, moe_wg,
           moe_wu, moe_wd):
    nb, l, d = x.shape
    lc = ctx.shape[1]
    depth = ada_w.shape[0]
    assert lc % TR == 0 and l % TR == 0 and l % GRID_W == 0
    nct = lc // TR
    xa = jnp.concatenate([ctx, x], axis=1)

    mod_rows = -(-(nb + 1) // SUBLANES) * SUBLANES
    c_all = jnp.concatenate([c, c_ctx[None], jnp.zeros((mod_rows - nb - 1, d), F32)], axis=0)
    mod_all = _ada_all(c_all, ada_w, ada_b)[:, :nb + 1].reshape(depth, nb + 1, N_MOD, d)
    cos, slo, shi = _rope_tables(lc, l)

    for i in range(depth):
        mod = mod_all[i]
        lp = dict(mu=shift_mu[i], w0=rw_w0[i], w2=rw_w2[i], a0=rw_a0[i], a2=rw_a2[i], g2=rw_g2[i], kk=rw_kk[i],
                  ka=rw_ka[i], rk=rw_rk[i], gn_w=rw_gn_w[i], gn_b=rw_gn_b[i])
        wi = w_in[i].astype(BF16)
        pa, pr = _in_proj(xa, mod, norm1_g[i], wi[:, :ATTN_IN], wi[:, ATTN_IN:], nct)
        q, k, v = _attn_prep(pa, cos, slo, shi, q_gain[i], k_gain[i])
        att = _attention(q, k, v, nct, lc)
        o_f, bv_f = _rwkv_scan(pr, lp, nct, False)
        o_b, bv_b = _rwkv_scan(pr, lp, nct, True)
        rw = _rwkv_out(o_f, o_b, bv_f, bv_b, pr, lp, nct)
        wo = w_out[i].astype(BF16)
        j = i // 2
        if i % 2 == 0:
            xa, h = _out_proj(xa, att, rw, mod, norm2_g[i], wo[:ATTN_WIDTH], wo[ATTN_WIDTH:], nct)
            xa = _ffn_dense(xa, h, mod, ffn_wg[j].astype(BF16), ffn_wu[j].astype(BF16), ffn_wd[j].astype(BF16), nct)
        else:
            xa, h, route = _out_proj(xa, att, rw, mod, norm2_g[i], wo[:ATTN_WIDTH], wo[ATTN_WIDTH:], nct,
                                     moe_router[j])
            xa = _moe_ffn(xa, h, route, mod, moe_wg[j].astype(BF16), moe_wu[j].astype(BF16), moe_wd[j].astype(BF16),
                          nct)
    return xa[:, lc:]
```

```python
import functools
import math

import jax
import jax.numpy as jnp
from jax import lax
from jax.experimental import pallas as pl
from jax.experimental.pallas import tpu as pltpu

F32 = jnp.float32
BF16 = jnp.bfloat16
HI = lax.Precision.HIGHEST

HEAD_DIM = 64
ROPE_HALF = HEAD_DIM // 2
ROPE_THETA = 10000.0
GRID_W = 64
ATTN_HEADS = 8
ATTN_KV_HEADS = 2
ATTN_GROUP = ATTN_HEADS // ATTN_KV_HEADS
ATTN_WIDTH = ATTN_HEADS * HEAD_DIM
KV_WIDTH = ATTN_KV_HEADS * HEAD_DIM
ATTN_IN = ATTN_WIDTH + 2 * KV_WIDTH
ATTN_SCALE = HEAD_DIM ** -0.5
LOG2E = math.log2(math.e)
RWKV_HEADS = 8
RWKV_WIDTH = RWKV_HEADS * HEAD_DIM
DECAY_LORA = 64
ICLR_LORA = 64
GATE_LORA = 128
SHIFT_WIDTH = 3 * RWKV_WIDTH + 2 * DECAY_LORA + 2 * ICLR_LORA + GATE_LORA
N_EXPERTS = 8
N_MOD = 6
EPS = 1e-6
GN_EPS = 64e-5

LANES = 128
SUBLANES = 8
TR = 256
CH = 128
PAIR = LANES // HEAD_DIM
MXU_DIM = 256
ATTN_KT = 256
MOE_BM = 512
MOE_TF = 1792
VMEM_LIMIT = 56 * 1024 * 1024


def _cparams(sem):
    return pltpu.CompilerParams(dimension_semantics=sem, vmem_limit_bytes=VMEM_LIMIT)


def _dot(a, b, prec=None):
    return jnp.dot(a, b, preferred_element_type=F32, precision=prec)


def _dot_nt(a, b, prec=None):
    return lax.dot_general(a, b, (((1,), (1,)), ((), ())), preferred_element_type=F32, precision=prec)


def _dot_tn(a, b, prec=None):
    return lax.dot_general(a, b, (((0,), (0,)), ((), ())), preferred_element_type=F32, precision=prec)


def _sigmoid(x):
    return 1.0 / (1.0 + jnp.exp(-x))


def _split2(x):
    hi = x.astype(BF16)
    return hi, (x - hi.astype(F32)).astype(BF16)


def _split3(x):
    hi = x.astype(BF16)
    r = x - hi.astype(F32)
    mid = r.astype(BF16)
    return hi, mid, (r - mid.astype(F32)).astype(BF16)


def _dot3(a, b):
    a_hi, a_lo = _split2(a)
    b_hi, b_lo = _split2(b)
    return _dot(a_hi, b_hi) + (_dot(a_hi, b_lo) + _dot(a_lo, b_hi))


def _head_sums(x, scale=1.0):
    r, n = x.shape
    i = lax.broadcasted_iota(jnp.int32, (MXU_DIM, MXU_DIM), 0) // HEAD_DIM
    j = lax.broadcasted_iota(jnp.int32, (MXU_DIM, MXU_DIM), 1) // HEAD_DIM
    ones = jnp.where(i == j, scale, 0.0).astype(BF16)
    nfull = n // MXU_DIM
    parts = []
    if nfull:
        xs = jnp.concatenate([x[:, c * MXU_DIM:(c + 1) * MXU_DIM] for c in range(nfull)], axis=0)
        m = nfull * r
        s = _dot(jnp.concatenate(_split2(xs), axis=0), ones)
        s = s[:m] + s[m:]
        parts += [s[c * r:(c + 1) * r] for c in range(nfull)]
    if n % MXU_DIM:
        w = n % MXU_DIM
        s = _dot(jnp.concatenate(_split2(x[:, nfull * MXU_DIM:]), axis=0), ones[:w, :w])
        parts.append(s[:r] + s[r:])
    return jnp.concatenate(parts, axis=1)


def _ada_kernel(c_ref, w_ref, b_ref, o_ref):
    c = c_ref[...]
    sc = c * _sigmoid(c)
    o_ref[0] = _dot(sc, w_ref[0], HI) + b_ref[0]


def _ada_all(c_all, ada_w, ada_b):
    depth, d, n = ada_w.shape
    rows = c_all.shape[0]
    tn = 1536
    return pl.pallas_call(
        _ada_kernel,
        grid=(depth, n // tn),
        in_specs=[pl.BlockSpec((rows, d), lambda i, j: (0, 0)),
                  pl.BlockSpec((1, d, tn), lambda i, j: (i, 0, j)),
                  pl.BlockSpec((1, 1, tn), lambda i, j: (i, 0, j))],
        out_specs=pl.BlockSpec((1, rows, tn), lambda i, j: (i, 0, j)),
        out_shape=jax.ShapeDtypeStruct((depth, rows, n), F32),
        compiler_params=_cparams(("arbitrary", "arbitrary")),
        name="ada_mod",
    )(c_all, ada_w, ada_b.reshape(depth, 1, n))


def _norm_mod(x, g, shift, scale):
    y = x * lax.rsqrt(jnp.mean(x * x, axis=-1, keepdims=True) + EPS)
    return (y * g) * (1.0 + scale) + shift


def _in_proj_kernel(x_ref, mod_ref, g_ref, wa_ref, wr_ref, pa_ref, pr_ref):
    h = _norm_mod(x_ref[0], g_ref[...], mod_ref[0, 0:1], mod_ref[0, 1:2]).astype(BF16)
    pa_ref[0] = _dot(h, wa_ref[...])
    pr_ref[0] = _dot(h, wr_ref[...])


def _mod_index(nb, nct):
    return lambda b, t: (jnp.where(t < nct, nb, b), 0, 0)


def _in_proj(xa, mod, g, wa, wr, nct):
    nb, s, d = xa.shape
    na, nr = wa.shape[1], wr.shape[1]
    return pl.pallas_call(
        _in_proj_kernel,
        grid=(nb, s // TR),
        in_specs=[pl.BlockSpec((1, TR, d), lambda b, t: (b, t, 0)),
                  pl.BlockSpec((1, N_MOD, d), _mod_index(nb, nct)),
                  pl.BlockSpec((1, d), lambda b, t: (0, 0)),
                  pl.BlockSpec((d, na), lambda b, t: (0, 0)),
                  pl.BlockSpec((d, nr), lambda b, t: (0, 0))],
        out_specs=[pl.BlockSpec((1, TR, na), lambda b, t: (b, t, 0)),
                   pl.BlockSpec((1, TR, nr), lambda b, t: (b, t, 0))],
        out_shape=[jax.ShapeDtypeStruct((nb, s, na), F32),
                   jax.ShapeDtypeStruct((nb, s, nr), F32)],
        compiler_params=_cparams(("arbitrary", "arbitrary")),
        name="in_proj",
    )(xa, mod, g.reshape(1, d), wa, wr)


def _rope(x, cos, sin_lo, sin_hi):
    n = x.shape[-1]
    return x * cos + pltpu.roll(x, n - ROPE_HALF, 1) * sin_lo + pltpu.roll(x, ROPE_HALF, 1) * sin_hi


def _attn_prep_kernel(pa_ref, cos_ref, slo_ref, shi_ref, qg_ref, kg_ref, q_ref, k_ref, v_ref):
    pa = pa_ref[0]
    qk = pa[:, :ATTN_WIDTH + KV_WIDTH]
    v = pa[:, ATTN_WIDTH + KV_WIDTH:ATTN_IN]
    cos, slo, shi = cos_ref[...], slo_ref[...], shi_ref[...]
    inv = lax.rsqrt(_head_sums(qk * qk, 1.0 / HEAD_DIM) + EPS)
    qn = qk[:, :ATTN_WIDTH] * inv[:, :ATTN_WIDTH] * qg_ref[...]
    qr = _rope(qn, cos, slo, shi) * (ATTN_SCALE * LOG2E)
    kn = qk[:, ATTN_WIDTH:] * inv[:, ATTN_WIDTH:] * kg_ref[...]
    kr = _rope(kn, cos[:, :KV_WIDTH], slo[:, :KV_WIDTH], shi[:, :KV_WIDTH])
    lane = lax.broadcasted_iota(jnp.int32, (TR, LANES), 1)
    low = lane < HEAD_DIM
    for j in range(ATTN_KV_HEADS):
        kj = kr if j % PAIR == 0 else pltpu.roll(kr, HEAD_DIM, 1)
        vj = v if j % PAIR == 0 else pltpu.roll(v, HEAD_DIM, 1)
        k_ref[0, j] = jnp.where(low, kj, 0.0).astype(BF16)
        v_ref[0, j] = jnp.where(low, vj, jnp.where(lane == HEAD_DIM, 1.0, 0.0)).astype(BF16)
    for h in range(ATTN_HEADS):
        src = qr[:, (h // PAIR) * LANES:(h // PAIR + 1) * LANES]
        if h % PAIR:
            src = pltpu.roll(src, HEAD_DIM, 1)
        q_ref[0, h] = jnp.where(low, src, 0.0).astype(BF16)


def _attn_prep(pa, cos, slo, shi, q_gain, k_gain):
    nb, s, _ = pa.shape
    qg = jnp.tile(q_gain, ATTN_HEADS).reshape(1, ATTN_WIDTH)
    kg = jnp.tile(k_gain, ATTN_KV_HEADS).reshape(1, KV_WIDTH)
    tab = pl.BlockSpec((TR, ATTN_WIDTH), lambda t, b: (t, 0))
    return pl.pallas_call(
        _attn_prep_kernel,
        grid=(s // TR, nb),
        in_specs=[pl.BlockSpec((1, TR, ATTN_IN), lambda t, b: (b, t, 0)), tab, tab, tab,
                  pl.BlockSpec((1, ATTN_WIDTH), lambda t, b: (0, 0)),
                  pl.BlockSpec((1, KV_WIDTH), lambda t, b: (0, 0))],
        out_specs=[pl.BlockSpec((1, ATTN_HEADS, TR, LANES), lambda t, b: (b, 0, t, 0)),
                   pl.BlockSpec((1, ATTN_KV_HEADS, TR, LANES), lambda t, b: (b, 0, t, 0)),
                   pl.BlockSpec((1, ATTN_KV_HEADS, TR, LANES), lambda t, b: (b, 0, t, 0))],
        out_shape=[jax.ShapeDtypeStruct((nb, ATTN_HEADS, s, LANES), BF16),
                   jax.ShapeDtypeStruct((nb, ATTN_KV_HEADS, s, LANES), BF16),
                   jax.ShapeDtypeStruct((nb, ATTN_KV_HEADS, s, LANES), BF16)],
        compiler_params=_cparams(("arbitrary", "arbitrary")),
        name="attn_prep",
    )(pa, cos, slo, shi, qg, kg)


def _rope_tables(lc, l):
    rows = l // GRID_W
    row = jnp.repeat(jnp.arange(rows, dtype=F32), GRID_W)
    col = jnp.tile(jnp.arange(GRID_W, dtype=F32), rows)
    inv = ROPE_THETA ** (-jnp.arange(0, ROPE_HALF, 2, dtype=F32) / ROPE_HALF)
    ang = jnp.concatenate([row[:, None] * inv, col[:, None] * inv], axis=-1)
    cos = jnp.concatenate([jnp.ones((lc, ROPE_HALF), F32), jnp.cos(ang)], axis=0)
    sin = jnp.concatenate([jnp.zeros((lc, ROPE_HALF), F32), jnp.sin(ang)], axis=0)
    zero = jnp.zeros_like(sin)
    head = lambda lo, hi: jnp.tile(jnp.concatenate([lo, hi], axis=-1), (1, ATTN_HEADS))
    return head(cos, cos), head(-sin, zero), head(zero, sin)


def _attn_kernel(q_ref, k_ref, v_ref, o_ref, s_ref, *, nct, lc):
    t = pl.program_id(2)
    n_all = k_ref.shape[2]
    low = lax.broadcasted_iota(jnp.int32, (TR, LANES), 1) < HEAD_DIM

    def run(n_keys):
        tiles = [slice(j * ATTN_KT, (j + 1) * ATTN_KT) for j in range(n_keys // ATTN_KT)]
        mx, acc = [None] * ATTN_GROUP, [None] * ATTN_GROUP
        m = [None] * ATTN_GROUP
        for stage in range(ATTN_GROUP + 1):
            g1, g2 = stage, stage - 1
            if g1 < ATTN_GROUP:
                mx[g1] = jnp.full((TR, LANES), -jnp.inf, F32)
            if g2 >= 0:
                m[g2] = jnp.max(mx[g2], axis=-1, keepdims=True)
                acc[g2] = jnp.zeros((TR, LANES), F32)
            for ks in tiles:
                if g1 < ATTN_GROUP:
                    s = _dot_nt(q_ref[0, g1], k_ref[0, 0, ks, :])
                    s_ref[g1, :, ks] = s
                    for c in range(ATTN_KT // LANES):
                        mx[g1] = jnp.maximum(mx[g1], s[:, c * LANES:(c + 1) * LANES])
                if g2 >= 0:
                    p = jnp.exp2(s_ref[g2, :, ks] - m[g2]).astype(BF16)
                    acc[g2] = acc[g2] + _dot(p, v_ref[0, 0, ks, :])
        out = [a / a[:, HEAD_DIM:HEAD_DIM + 1] for a in acc]
        for pr in range(ATTN_GROUP // PAIR):
            hi = pltpu.roll(out[PAIR * pr + 1], HEAD_DIM, 1)
            o_ref[0, :, pr * LANES:(pr + 1) * LANES] = jnp.where(low, out[PAIR * pr], hi).astype(BF16)

    @pl.when(t < nct)
    def _():
        run(lc)

    @pl.when(t >= nct)
    def _():
        run(n_all)


def _attention(q, k, v, nct, lc):
    nb, _, s, _ = q.shape
    gw = ATTN_GROUP * HEAD_DIM
    kv_spec = pl.BlockSpec((1, 1, s, LANES), lambda b, j, t: (b, j, 0, 0))
    return pl.pallas_call(
        functools.partial(_attn_kernel, nct=nct, lc=lc),
        grid=(nb, ATTN_KV_HEADS, s // TR),
        in_specs=[pl.BlockSpec((1, ATTN_GROUP, TR, LANES), lambda b, j, t: (b, j, t, 0)), kv_spec, kv_spec],
        out_specs=pl.BlockSpec((1, TR, gw), lambda b, j, t: (b, t, j)),
        out_shape=jax.ShapeDtypeStruct((nb, s, ATTN_WIDTH), BF16),
        scratch_shapes=[pltpu.VMEM((ATTN_GROUP, TR, s), F32)],
        compiler_params=_cparams(("arbitrary", "arbitrary", "arbitrary")),
        name="attention",
    )(q, k, v)


def _token_shift(p, prev_row, next_row, mu):
    n = p.shape[0]
    row = lax.broadcasted_iota(jnp.int32, p.shape, 0)
    prev = jnp.where(row == 0, prev_row, pltpu.roll(p, 1, 0))
    nxt = jnp.where(row == n - 1, next_row, pltpu.roll(p, n - 1, 0))
    return p + mu * (0.5 * (prev + nxt) - p)


def _halo_rows(prev_ref, next_ref, tile, nct, ng):
    first = jnp.logical_or(tile == 0, tile == nct)
    last = jnp.logical_or(tile == nct - 1, tile == ng - 1)
    prev_row = jnp.where(first, 0.0, prev_ref[0, SUBLANES - 1:SUBLANES, :])
    next_row = jnp.where(last, 0.0, next_ref[0, 0:1, :])
    return prev_row, next_row


def _rwkv_tile(rev, g, nct, ng):
    if not rev:
        return g
    return jnp.where(g < nct, nct - 1 - g, ng - 1 - (g - nct))


def _block_diag(x):
    z = jnp.zeros((CH, CH), x.dtype)
    return jnp.concatenate([jnp.concatenate([x[:, :CH], z], axis=1), jnp.concatenate([z, x[:, CH:]], axis=1)], axis=0)


def _head_rows(x):
    h0 = (lax.broadcasted_iota(jnp.int32, x.shape, 1) & (LANES - 1)) < HEAD_DIM
    return jnp.concatenate([jnp.where(h0, x, 0.0), jnp.where(h0, 0.0, x)], axis=0).astype(BF16)


def _rwkv_dir_kernel(p_ref, prev_ref, next_ref, mu_ref, kkw_ref, ka_ref, rk_ref, w0_ref, a0_ref, w2_ref, a2_ref,
                     o_ref, bv_ref, s_ref, *, rev, nct, ng):
    g = pl.program_id(1)
    tile = _rwkv_tile(rev, g, nct, ng)
    w = RWKV_WIDTH

    @pl.when(g == 0)
    def _():
        s_ref[...] = jnp.zeros_like(s_ref)

    prev_row, next_row = _halo_rows(prev_ref, next_ref, tile, nct, ng)
    ps = _token_shift(p_ref[0], prev_row, next_row, mu_ref[...])
    r, k, v = ps[:, :w], ps[:, w:2 * w], ps[:, 2 * w:3 * w]
    wd = ps[:, 3 * w:3 * w + 2 * DECAY_LORA]
    ad = ps[:, 3 * w + 2 * DECAY_LORA:3 * w + 2 * DECAY_LORA + 2 * ICLR_LORA]
    kk = k * kkw_ref[...]
    kk = kk * lax.rsqrt(_head_sums(kk * kk) + 1e-12)
    x = w0_ref[...] + _dot3(jnp.tanh(wd), w2_ref[...])
    lw = (-math.exp(-0.5)) * _sigmoid(x)
    a = _sigmoid(a0_ref[...] + _dot3(ad, a2_ref[...]))
    k_d = k * (1.0 + (a - 1.0) * ka_ref[...])
    bv_ref[0] = _head_sums(r * k_d * rk_ref[...]) * v
    beta = kk * a

    ti = lax.broadcasted_iota(jnp.int32, (CH, PAIR * CH), 0)
    tj = lax.broadcasted_iota(jnp.int32, (CH, PAIR * CH), 1) & (CH - 1)
    prec2 = (tj > ti) if rev else (tj < ti)
    eye2 = ti == tj
    incl2 = jnp.logical_or(prec2, eye2)
    late, early = (tj, ti) if rev else (ti, tj)
    join = [jnp.logical_and(jnp.logical_and((ti ^ tj) < 2 * b, (late & b) != 0), (early & b) == 0)
            for b in (2 ** m for m in range(int(math.log2(CH))))]
    si = lax.broadcasted_iota(jnp.int32, (CH, CH), 0)
    sj = lax.broadcasted_iota(jnp.int32, (CH, CH), 1)
    eye = si == sj
    tri = jnp.where((sj >= si) if rev else (sj <= si), 1.0, 0.0).astype(BF16)
    same_head = (si // HEAD_DIM) == (sj // HEAD_DIM)
    n_chunks = TR // CH
    n_pairs = RWKV_HEADS // PAIR
    chunk_order = list(range(n_chunks - 1, -1, -1) if rev else range(n_chunks))
    units = [(ci, pp) for ci in chunk_order for pp in range(n_pairs)]
    each = lambda f, *lists: [f(*args) for args in zip(*lists)]

    cum = {}
    for ci in chunk_order:
        cs = _dot(tri, jnp.concatenate(_split3(lw[ci * CH:(ci + 1) * CH]), axis=1))
        cum[ci] = cs[:, :w] + cs[:, w:2 * w] + cs[:, 2 * w:]
    blk = lambda x: [x[ci * CH:(ci + 1) * CH, pp * LANES:(pp + 1) * LANES] for ci, pp in units]
    rx, vx, kap, lwx, kx, bet = blk(r), blk(v), blk(kk), blk(lw), blk(k_d), blk(beta)
    cumi = [cum[ci][:, pp * LANES:(pp + 1) * LANES] for ci, pp in units]
    tot = each(lambda c: c[0:1, :] if rev else c[CH - 1:CH, :], cumi)
    cc = each(lambda c: c - c[CH // 2:CH // 2 + 1, :], cumi)
    e_neg = each(lambda c: jnp.exp(-c), cc)
    e_end = each(lambda t, c: jnp.exp(t - c), tot, cumi)
    kap_t = each(lambda x, c, l: x * jnp.exp(c - l), kap, cc, lwx)
    r_t = each(lambda x, c: x * jnp.exp(c), rx, cc)
    kap_0 = each(lambda x, c, l: x * jnp.exp(c - l), kap, cumi, lwx)
    r_0 = each(lambda x, c: x * jnp.exp(c), rx, cumi)
    k_t = each(lambda x, e: x * e, kx, e_neg)
    bet_t = each(lambda x, e: x * e, bet, e_neg)
    k_e = each(lambda x, e: x * e, kx, e_end)
    bet_e = each(lambda x, e: x * e, bet, e_end)
    p_end = each(jnp.exp, tot)
    qs = each(lambda a, b: jnp.concatenate([a, b], axis=0).astype(BF16), kap_t, r_t)
    s_b = each(lambda q, x: _dot_nt(q, _head_rows(x)), qs, bet_t)
    s_k = each(lambda q, x: _dot_nt(q, _head_rows(x)), qs, k_t)
    a_b = each(lambda s: jnp.where(prec2, s[:CH], 0.0), s_b)
    a_k = each(lambda s: jnp.where(prec2, s[:CH], 0.0).astype(BF16), s_k)
    b_b = each(lambda s: jnp.where(incl2, s[CH:], 0.0).astype(BF16), s_b)
    b_k = each(lambda s: jnp.where(incl2, s[CH:], 0.0).astype(BF16), s_k)
    ts = each(lambda a: jnp.where(eye2, 1.0, jnp.where(join[0], -a, 0.0)), a_b)
    for lvl in range(1, len(join)):
        q_b = each(lambda a: _block_diag(jnp.where(join[lvl], a, 0.0).astype(BF16)), a_b)
        t_b = each(lambda t: t.astype(BF16), ts)
        dq = each(lambda t, q: _dot(t, q).astype(BF16), t_b, q_b)
        ts = each(lambda t, e, tb: t - _dot(e, _block_diag(tb)), ts, dq, t_b)
    t_b = each(lambda t: t.astype(BF16), ts)
    v_h = each(_head_rows, vx)
    av = each(_dot, a_k, v_h)
    rhs = each(lambda k0, a: _head_rows(jnp.concatenate([k0, a], axis=1)), kap_0, av)
    wz = each(_dot, t_b, rhs)
    kq = each(lambda x: x[:, :LANES], wz)
    z0 = each(lambda x: x[:, LANES:], wz)
    bz = each(lambda b, x: _dot(b, _head_rows(x)), b_b, wz)
    o0k = each(_dot, b_k, v_h)
    r_new = each(lambda x, b: x - b[:, :LANES], r_0, bz)
    o0 = each(lambda a, b: a - b[:, LANES:], o0k, bz)
    m_c = each(lambda p, b, q: jnp.where(eye, p, 0.0) - jnp.where(same_head, _dot_tn(b, q), 0.0), p_end, bet_e, kq)
    g_c = each(lambda ke, be, x, z: jnp.where(same_head, _dot_tn(jnp.concatenate([ke, be], axis=0),
                                                                  jnp.concatenate([x, -z], axis=0)), 0.0),
               k_e, bet_e, vx, z0)

    states = [s_ref[pp] for pp in range(n_pairs)]
    outs = {}
    for u, (ci, pp) in enumerate(units):
        st = states[pp]
        outs[ci, pp] = _dot3(r_new[u], st) + o0[u]
        states[pp] = _dot3(m_c[u], st) + g_c[u]
    for ci in chunk_order:
        o_ref[0, ci * CH:(ci + 1) * CH, :] = jnp.concatenate([outs[ci, pp] for pp in range(n_pairs)], axis=1)
    for pp in range(n_pairs):
        s_ref[pp] = states[pp]


def _rwkv_scan(pr, lp, nct, rev):
    nb, s, pw = pr.shape
    ng = s // TR
    w = RWKV_WIDTH
    hb = TR // SUBLANES
    d = int(rev)
    tile = functools.partial(_rwkv_tile, rev, nct=nct, ng=ng)
    n_row_blocks = s // SUBLANES
    main = pl.BlockSpec((1, TR, pw), lambda b, g: (b, tile(g), 0))
    prev = pl.BlockSpec((1, SUBLANES, pw), lambda b, g: (b, jnp.maximum(tile(g) * hb - 1, 0), 0))
    nxt = pl.BlockSpec((1, SUBLANES, pw), lambda b, g: (b, jnp.minimum((tile(g) + 1) * hb, n_row_blocks - 1), 0))
    vec = lambda n: pl.BlockSpec((1, n), lambda b, g: (0, 0))
    mat = pl.BlockSpec((2 * DECAY_LORA, w), lambda b, g: (0, 0))
    out = pl.BlockSpec((1, TR, w), lambda b, g: (b, tile(g), 0))
    zeros = jnp.zeros((DECAY_LORA, w), F32)
    pad = lambda m: jnp.concatenate([zeros, m] if rev else [m, zeros])
    return pl.pallas_call(
        functools.partial(_rwkv_dir_kernel, rev=rev, nct=nct, ng=ng),
        grid=(nb, ng),
        in_specs=[main, prev, nxt, vec(pw), vec(w), vec(w), vec(w), vec(w), vec(w), mat, mat],
        out_specs=[out, out],
        out_shape=[jax.ShapeDtypeStruct((nb, s, w), F32), jax.ShapeDtypeStruct((nb, s, w), F32)],
        scratch_shapes=[pltpu.VMEM((RWKV_HEADS // PAIR, LANES, LANES), F32)],
        compiler_params=_cparams(("arbitrary", "arbitrary")),
        name="rwkv_scan_bwd" if rev else "rwkv_scan_fwd",
    )(pr, pr, pr, lp['mu'].reshape(1, pw), lp['kk'].reshape(1, w), lp['ka'].reshape(1, w), lp['rk'].reshape(1, w),
      lp['w0'][d].reshape(1, w), lp['a0'][d].reshape(1, w), pad(lp['w2'][d]), pad(lp['a2'][d]))


def _rwkv_out_kernel(of_ref, ob_ref, bf_ref, bb_ref, gd_ref, gprev_ref, gnext_ref, mu_ref, g2_ref, gw_ref, gb_ref,
                     y_ref, *, nct, ng):
    tile = pl.program_id(1)
    o = of_ref[0] + ob_ref[0]
    cen = o - _head_sums(o, 1.0 / HEAD_DIM)
    var = _head_sums(cen * cen, 1.0 / HEAD_DIM)
    y = cen * lax.rsqrt(var + GN_EPS) * gw_ref[...] + gb_ref[...] + (bf_ref[0] + bb_ref[0])
    prev_row, next_row = _halo_rows(gprev_ref, gnext_ref, tile, nct, ng)
    gd = _token_shift(gd_ref[0], prev_row, next_row, mu_ref[...])
    gate = _dot3(_sigmoid(gd), g2_ref[...])
    y_ref[0] = (y * gate).astype(BF16)


def _rwkv_out(o_f, o_b, bv_f, bv_b, pr, lp, nct):
    nb, s, w = o_f.shape
    ng = s // TR
    hb = TR // SUBLANES
    gcol = (SHIFT_WIDTH - GATE_LORA) // GATE_LORA
    n_row_blocks = s // SUBLANES
    row = pl.BlockSpec((1, TR, w), lambda b, t: (b, t, 0))
    vec = lambda n: pl.BlockSpec((1, n), lambda b, t: (0, 0))
    return pl.pallas_call(
        functools.partial(_rwkv_out_kernel, nct=nct, ng=ng),
        grid=(nb, ng),
        in_specs=[row, row, row, row,
                  pl.BlockSpec((1, TR, GATE_LORA), lambda b, t: (b, t, gcol)),
                  pl.BlockSpec((1, SUBLANES, GATE_LORA), lambda b, t: (b, jnp.maximum(t * hb - 1, 0), gcol)),
                  pl.BlockSpec((1, SUBLANES, GATE_LORA),
                               lambda b, t: (b, jnp.minimum((t + 1) * hb, n_row_blocks - 1), gcol)),
                  vec(GATE_LORA), pl.BlockSpec((GATE_LORA, w), lambda b, t: (0, 0)), vec(w), vec(w)],
        out_specs=row,
        out_shape=jax.ShapeDtypeStruct((nb, s, w), BF16),
        compiler_params=_cparams(("arbitrary", "arbitrary")),
        name="rwkv_out",
    )(o_f, o_b, bv_f, bv_b, pr, pr, pr, lp['mu'][SHIFT_WIDTH - GATE_LORA:].reshape(1, GATE_LORA), lp['g2'],
      lp['gn_w'].reshape(1, w), lp['gn_b'].reshape(1, w))


def _top2(logits):
    lane = lax.broadcasted_iota(jnp.int32, logits.shape, 1)
    v1 = jnp.max(logits, axis=-1, keepdims=True)
    i1 = jnp.min(jnp.where(logits == v1, lane, LANES), axis=-1, keepdims=True)
    rest = jnp.where(lane == i1, -jnp.inf, logits)
    v2 = jnp.max(rest, axis=-1, keepdims=True)
    i2 = jnp.min(jnp.where(rest == v2, lane, LANES), axis=-1, keepdims=True)
    e = jnp.exp(v2 - v1)
    g1 = 1.0 / (1.0 + e)
    g2 = e / (1.0 + e)
    out = jnp.where(lane == 0, i1.astype(F32), 0.0)
    out = jnp.where(lane == 1, i2.astype(F32), out)
    out = jnp.where(lane == 2, g1, out)
    return jnp.where(lane == 3, g2, out)


def _out_proj_kernel(x_ref, att_ref, rw_ref, mod_ref, g_ref, wa_ref, wr_ref, *rest, moe):
    if moe:
        wrt_ref, xo_ref, h_ref, route_ref = rest
    else:
        xo_ref, h_ref = rest
    mix = _dot(att_ref[0], wa_ref[...]) + _dot(rw_ref[0], wr_ref[...])
    x = x_ref[0] + mod_ref[0, 2:3] * mix
    xo_ref[0] = x
    h = _norm_mod(x, g_ref[...], mod_ref[0, 3:4], mod_ref[0, 4:5])
    h_ref[0] = h.astype(h_ref.dtype)
    if moe:
        lane = lax.broadcasted_iota(jnp.int32, (TR, LANES), 1)
        logits = jnp.where(lane < N_EXPERTS, _dot3(h, wrt_ref[...]), -jnp.inf)
        route_ref[0] = _top2(logits)


def _out_proj(xa, att, rw, mod, g, wo_a, wo_r, nct, w_router=None):
    nb, s, d = xa.shape
    moe = w_router is not None
    row = lambda n: pl.BlockSpec((1, TR, n), lambda b, t: (b, t, 0))
    full = lambda a: pl.BlockSpec(a.shape, lambda b, t: (0, 0))
    in_specs = [row(d), row(ATTN_WIDTH), row(RWKV_WIDTH), pl.BlockSpec((1, N_MOD, d), _mod_index(nb, nct)),
                pl.BlockSpec((1, d), lambda b, t: (0, 0)), full(wo_a), full(wo_r)]
    args = [xa, att, rw, mod, g.reshape(1, d), wo_a, wo_r]
    out_specs = [row(d), row(d)]
    out_shape = [jax.ShapeDtypeStruct((nb, s, d), F32), jax.ShapeDtypeStruct((nb, s, d), F32 if moe else BF16)]
    if moe:
        wrt = jnp.pad(w_router, ((0, 0), (0, LANES - N_EXPERTS)))
        in_specs.append(full(wrt))
        args.append(wrt)
        out_specs.append(row(LANES))
        out_shape.append(jax.ShapeDtypeStruct((nb, s, LANES), F32))
    return pl.pallas_call(
        functools.partial(_out_proj_kernel, moe=moe),
        grid=(nb, s // TR),
        in_specs=in_specs, out_specs=out_specs, out_shape=out_shape,
        compiler_params=_cparams(("arbitrary", "arbitrary")),
        name="out_proj_moe" if moe else "out_proj",
    )(*args)


def _ffn_kernel(x_ref, h_ref, mod_ref, wg_ref, wu_ref, wd_ref, o_ref):
    h = h_ref[0]
    gt = _dot(h, wg_ref[...])
    up = _dot(h, wu_ref[...])
    act = (gt * _sigmoid(gt) * up).astype(BF16)
    o_ref[0] = x_ref[0] + mod_ref[0, 5:6] * _dot(act, wd_ref[...])


def _ffn_dense(xa, h, mod, wg, wu, wd, nct):
    nb, s, d = xa.shape
    row = pl.BlockSpec((1, TR, d), lambda b, t: (b, t, 0))
    resident = lambda a: pl.BlockSpec(a.shape, lambda b, t: (0, 0), pipeline_mode=pl.Buffered(1))
    return pl.pallas_call(
        _ffn_kernel,
        grid=(nb, s // TR),
        in_specs=[row, row, pl.BlockSpec((1, N_MOD, d), _mod_index(nb, nct)), resident(wg), resident(wu),
                  resident(wd)],
        out_specs=row,
        out_shape=jax.ShapeDtypeStruct((nb, s, d), F32),
        compiler_params=_cparams(("arbitrary", "arbitrary")),
        name="ffn_dense",
    )(xa, h, mod, wg, wu, wd)


def _moe_kernel(be_ref, nu_ref, x_ref, wg_ref, wu_ref, wd_ref, o_ref):
    i = pl.program_id(0)
    j = pl.program_id(1)

    @pl.when(j == 0)
    def _():
        o_ref[...] = jnp.zeros_like(o_ref)

    @pl.when(i < nu_ref[0])
    def _():
        x = x_ref[...].astype(BF16)
        gt = _dot(x, wg_ref[0])
        up = _dot(x, wu_ref[0])
        act = (gt * _sigmoid(gt) * up).astype(BF16)
        o_ref[...] += _dot(act, wd_ref[0])


def _moe_experts(xs, block_expert, n_used, wg, wu, wd):
    n_slots, d = xs.shape
    dff = wg.shape[2]
    nblk = n_slots // MOE_BM
    grid_spec = pltpu.PrefetchScalarGridSpec(
        num_scalar_prefetch=2,
        grid=(nblk, dff // MOE_TF),
        in_specs=[pl.BlockSpec((MOE_BM, d), lambda i, j, be, nu: (i, 0)),
                  pl.BlockSpec((1, d, MOE_TF), lambda i, j, be, nu: (be[i], 0, j)),
                  pl.BlockSpec((1, d, MOE_TF), lambda i, j, be, nu: (be[i], 0, j)),
                  pl.BlockSpec((1, MOE_TF, d), lambda i, j, be, nu: (be[i], j, 0))],
        out_specs=pl.BlockSpec((MOE_BM, d), lambda i, j, be, nu: (i, 0)),
    )
    return pl.pallas_call(
        _moe_kernel,
        grid_spec=grid_spec,
        out_shape=jax.ShapeDtypeStruct((n_slots, d), F32),
        compiler_params=_cparams(("arbitrary", "arbitrary")),
        name="moe_experts",
    )(block_expert, n_used, xs, wg, wu, wd)


def _row_copy(src_hbm, idx_ref, dst_ref, sem, r):
    return pltpu.make_async_copy(src_hbm.at[pl.ds(idx_ref[0, 0, r], 1), :], dst_ref.at[pl.ds(r, 1), :], sem)


def _gather_rows(src_hbm, idx_ref, dst_ref, sem):
    n = dst_ref.shape[0]

    def start(r, carry):
        _row_copy(src_hbm, idx_ref, dst_ref, sem, r).start()
        return carry

    def wait(r, carry):
        _row_copy(src_hbm, idx_ref, dst_ref, sem, r).wait()
        return carry

    lax.fori_loop(0, n, start, 0, unroll=8)
    lax.fori_loop(0, n, wait, 0, unroll=8)


def _dispatch_kernel(tok_ref, h_hbm, o_ref, sem):
    _gather_rows(h_hbm, tok_ref, o_ref, sem)


def _moe_dispatch(h, slot_token):
    t, d = h.shape
    nblk = slot_token.shape[0] // MOE_BM
    return pl.pallas_call(
        _dispatch_kernel,
        grid=(nblk,),
        in_specs=[pl.BlockSpec((1, 1, MOE_BM), lambda i: (i, 0, 0), memory_space=pltpu.SMEM),
                  pl.BlockSpec(memory_space=pl.ANY)],
        out_specs=pl.BlockSpec((MOE_BM, d), lambda i: (i, 0)),
        out_shape=jax.ShapeDtypeStruct((nblk * MOE_BM, d), h.dtype),
        scratch_shapes=[pltpu.SemaphoreType.DMA(())],
        compiler_params=_cparams(("arbitrary",)),
        name="moe_dispatch",
    )(slot_token.reshape(nblk, 1, MOE_BM), h)


def _combine_kernel(s1_ref, s2_ref, x_ref, route_ref, mod_ref, ys_hbm, o_ref, y1_ref, y2_ref, sem):
    _gather_rows(ys_hbm, s1_ref, y1_ref, sem.at[0])
    _gather_rows(ys_hbm, s2_ref, y2_ref, sem.at[1])
    route = route_ref[0]
    f = y1_ref[...] * route[:, 2:3] + y2_ref[...] * route[:, 3:4]
    o_ref[0] = x_ref[0] + mod_ref[0, 5:6] * f


def _moe_combine(xa, route, mod, ys, slot1, slot2, nct):
    nb, s, d = xa.shape
    ng = s // TR
    idx = pl.BlockSpec((1, 1, TR), lambda b, t: (b * ng + t, 0, 0), memory_space=pltpu.SMEM)
    row = lambda n: pl.BlockSpec((1, TR, n), lambda b, t: (b, t, 0))
    return pl.pallas_call(
        _combine_kernel,
        grid=(nb, ng),
        in_specs=[idx, idx, row(d), row(LANES), pl.BlockSpec((1, N_MOD, d), _mod_index(nb, nct)),
                  pl.BlockSpec(memory_space=pl.ANY)],
        out_specs=row(d),
        out_shape=jax.ShapeDtypeStruct((nb, s, d), F32),
        scratch_shapes=[pltpu.VMEM((TR, d), F32), pltpu.VMEM((TR, d), F32), pltpu.SemaphoreType.DMA((2,))],
        compiler_params=_cparams(("arbitrary", "arbitrary")),
        name="moe_combine",
    )(slot1.reshape(nb * ng, 1, TR), slot2.reshape(nb * ng, 1, TR), xa, route, mod, ys)


def _moe_ffn(xa, h, route, mod, wg, wu, wd, nct):
    nb, s, d = xa.shape
    t = nb * s
    n_assign = 2 * t
    nblk = (n_assign + N_EXPERTS * (MOE_BM - 1) + MOE_BM - 1) // MOE_BM
    n_slots = nblk * MOE_BM
    expert = route.reshape(t, LANES)[:, 0:2].astype(jnp.int32).reshape(-1)
    onehot = (expert[:, None] == jnp.arange(N_EXPERTS, dtype=jnp.int32)[None, :]).astype(jnp.int32)
    rank = jnp.sum((jnp.cumsum(onehot, axis=0) - onehot) * onehot, axis=1)
    counts = jnp.sum(onehot, axis=0)
    padded = (counts + MOE_BM - 1) // MOE_BM * MOE_BM
    pad_end = jnp.cumsum(padded)
    pad_start = pad_end - padded
    slot = pad_start[expert] + rank
    token = jnp.repeat(jnp.arange(t, dtype=jnp.int32), 2)
    slot_token = jnp.zeros((n_slots,), jnp.int32).at[slot].set(token)
    block_start = jnp.arange(nblk, dtype=jnp.int32) * MOE_BM
    block_expert = jnp.minimum(jnp.sum((pad_end[None, :] <= block_start[:, None]).astype(jnp.int32), axis=1),
                               N_EXPERTS - 1)
    n_used = (pad_end[-1] // MOE_BM).astype(jnp.int32).reshape(1)
    xs = _moe_dispatch(h.reshape(t, d), slot_token)
    ys = _moe_experts(xs, block_expert, n_used, wg, wu, wd)
    slot2 = slot.reshape(t, 2)
    return _moe_combine(xa, route, mod, ys, slot2[:, 0], slot2[:, 1], nct)


def kernel(x, c, ctx, c_ctx, ada_w, ada_b, norm1_g, norm2_g, w_in, w_out, q_gain, k_gain, shift_mu, rw_w0, rw_w2,
           rw_a0, rw_a2, rw_g2, rw_kk, rw_ka, rw_rk, rw_gn_w, rw_gn_b, ffn_wg, ffn_wu, ffn_wd, moe_router, moe_wg,
           moe_wu, moe_wd):
    nb, l, d = x.shape
    lc = ctx.shape[1]
    depth = ada_w.shape[0]
    assert lc % TR == 0 and l % TR == 0 and l % GRID_W == 0
    nct = lc // TR
    xa = jnp.concatenate([ctx, x], axis=1)

    mod_rows = -(-(nb + 1) // SUBLANES) * SUBLANES
    c_all = jnp.concatenate([c, c_ctx[None], jnp.zeros((mod_rows - nb - 1, d), F32)], axis=0)
    mod_all = _ada_all(c_all, ada_w, ada_b)[:, :nb + 1].reshape(depth, nb + 1, N_MOD, d)
    cos, slo, shi = _rope_tables(lc, l)

    for i in range(depth):
        mod = mod_all[i]
        lp = dict(mu=shift_mu[i], w0=rw_w0[i], w2=rw_w2[i], a0=rw_a0[i], a2=rw_a2[i], g2=rw_g2[i], kk=rw_kk[i],
                  ka=rw_ka[i], rk=rw_rk[i], gn_w=rw_gn_w[i], gn_b=rw_gn_b[i])
        wi = w_in[i].astype(BF16)
        pa, pr = _in_proj(xa, mod, norm1_g[i], wi[:, :ATTN_IN], wi[:, ATTN_IN:], nct)
        q, k, v = _attn_prep(pa, cos, slo, shi, q_gain[i], k_gain[i])
        att = _attention(q, k, v, nct, lc)
        o_f, bv_f = _rwkv_scan(pr, lp, nct, False)
        o_b, bv_b = _rwkv_scan(pr, lp, nct, True)
        rw = _rwkv_out(o_f, o_b, bv_f, bv_b, pr, lp, nct)
        wo = w_out[i].astype(BF16)
        j = i // 2
        if i % 2 == 0:
            xa, h = _out_proj(xa, att, rw, mod, norm2_g[i], wo[:ATTN_WIDTH], wo[ATTN_WIDTH:], nct)
            xa = _ffn_dense(xa, h, mod, ffn_wg[j].astype(BF16), ffn_wu[j].astype(BF16), ffn_wd[j].astype(BF16), nct)
        else:
            xa, h, route = _out_proj(xa, att, rw, mod, norm2_g[i], wo[:ATTN_WIDTH], wo[ATTN_WIDTH:], nct,
                                     moe_router[j])
            xa = _moe_ffn(xa, h, route, mod, moe_wg[j].astype(BF16), moe_wu[j].astype(BF16), moe_wd[j].astype(BF16),
                          nct)
    return xa[:, lc:]
```

```python
import functools
import math

import jax
import jax.numpy as jnp
from jax import lax
from jax.experimental import pallas as pl
from jax.experimental.pallas import tpu as pltpu

F32 = jnp.float32
BF16 = jnp.bfloat16
HI = lax.Precision.HIGHEST

HEAD_DIM = 64
ROPE_HALF = HEAD_DIM // 2
ROPE_THETA = 10000.0
GRID_W = 64
ATTN_HEADS = 8
ATTN_KV_HEADS = 2
ATTN_GROUP = ATTN_HEADS // ATTN_KV_HEADS
ATTN_WIDTH = ATTN_HEADS * HEAD_DIM
KV_WIDTH = ATTN_KV_HEADS * HEAD_DIM
ATTN_IN = ATTN_WIDTH + 2 * KV_WIDTH
ATTN_SCALE = HEAD_DIM ** -0.5
LOG2E = math.log2(math.e)
RWKV_HEADS = 8
RWKV_WIDTH = RWKV_HEADS * HEAD_DIM
DECAY_LORA = 64
ICLR_LORA = 64
GATE_LORA = 128
SHIFT_WIDTH = 3 * RWKV_WIDTH + 2 * DECAY_LORA + 2 * ICLR_LORA + GATE_LORA
N_EXPERTS = 8
N_MOD = 6
EPS = 1e-6
GN_EPS = 64e-5

LANES = 128
SUBLANES = 8
TR = 256
CH = 128
PAIR = LANES // HEAD_DIM
MXU_DIM = 256
ATTN_KT = 256
MOE_BM = 512
MOE_TF = 1792
VMEM_LIMIT = 56 * 1024 * 1024


def _cparams(sem):
    return pltpu.CompilerParams(dimension_semantics=sem, vmem_limit_bytes=VMEM_LIMIT)


def _dot(a, b, prec=None):
    return jnp.dot(a, b, preferred_element_type=F32, precision=prec)


def _dot_nt(a, b, prec=None):
    return lax.dot_general(a, b, (((1,), (1,)), ((), ())), preferred_element_type=F32, precision=prec)


def _dot_tn(a, b, prec=None):
    return lax.dot_general(a, b, (((0,), (0,)), ((), ())), preferred_element_type=F32, precision=prec)


def _sigmoid(x):
    return 1.0 / (1.0 + jnp.exp(-x))


def _split2(x):
    hi = x.astype(BF16)
    return hi, (x - hi.astype(F32)).astype(BF16)


def _split3(x):
    hi = x.astype(BF16)
    r = x - hi.astype(F32)
    mid = r.astype(BF16)
    return hi, mid, (r - mid.astype(F32)).astype(BF16)


def _dot3(a, b):
    a_hi, a_lo = _split2(a)
    b_hi, b_lo = _split2(b)
    return _dot(a_hi, b_hi) + (_dot(a_hi, b_lo) + _dot(a_lo, b_hi))


def _head_sums(x, scale=1.0):
    r, n = x.shape
    i = lax.broadcasted_iota(jnp.int32, (MXU_DIM, MXU_DIM), 0) // HEAD_DIM
    j = lax.broadcasted_iota(jnp.int32, (MXU_DIM, MXU_DIM), 1) // HEAD_DIM
    ones = jnp.where(i == j, scale, 0.0).astype(BF16)
    nfull = n // MXU_DIM
    parts = []
    if nfull:
        xs = jnp.concatenate([x[:, c * MXU_DIM:(c + 1) * MXU_DIM] for c in range(nfull)], axis=0)
        m = nfull * r
        s = _dot(jnp.concatenate(_split2(xs), axis=0), ones)
        s = s[:m] + s[m:]
        parts += [s[c * r:(c + 1) * r] for c in range(nfull)]
    if n % MXU_DIM:
        w = n % MXU_DIM
        s = _dot(jnp.concatenate(_split2(x[:, nfull * MXU_DIM:]), axis=0), ones[:w, :w])
        parts.append(s[:r] + s[r:])
    return jnp.concatenate(parts, axis=1)


def _ada_kernel(c_ref, w_ref, b_ref, o_ref):
    c = c_ref[...]
    sc = c * _sigmoid(c)
    o_ref[0] = _dot(sc, w_ref[0], HI) + b_ref[0]


def _ada_all(c_all, ada_w, ada_b):
    depth, d, n = ada_w.shape
    rows = c_all.shape[0]
    tn = 1536
    return pl.pallas_call(
        _ada_kernel,
        grid=(depth, n // tn),
        in_specs=[pl.BlockSpec((rows, d), lambda i, j: (0, 0)),
                  pl.BlockSpec((1, d, tn), lambda i, j: (i, 0, j)),
                  pl.BlockSpec((1, 1, tn), lambda i, j: (i, 0, j))],
        out_specs=pl.BlockSpec((1, rows, tn), lambda i, j: (i, 0, j)),
        out_shape=jax.ShapeDtypeStruct((depth, rows, n), F32),
        compiler_params=_cparams(("arbitrary", "arbitrary")),
        name="ada_mod",
    )(c_all, ada_w, ada_b.reshape(depth, 1, n))


def _norm_mod(x, g, shift, scale):
    y = x * lax.rsqrt(jnp.mean(x * x, axis=-1, keepdims=True) + EPS)
    return (y * g) * (1.0 + scale) + shift


def _in_proj_kernel(x_ref, mod_ref, g_ref, wa_ref, wr_ref, pa_ref, pr_ref):
    h = _norm_mod(x_ref[0], g_ref[...], mod_ref[0, 0:1], mod_ref[0, 1:2]).astype(BF16)
    pa_ref[0] = _dot(h, wa_ref[...])
    pr_ref[0] = _dot(h, wr_ref[...])


def _mod_index(nb, nct):
    return lambda b, t: (jnp.where(t < nct, nb, b), 0, 0)


def _in_proj(xa, mod, g, wa, wr, nct):
    nb, s, d = xa.shape
    na, nr = wa.shape[1], wr.shape[1]
    return pl.pallas_call(
        _in_proj_kernel,
        grid=(nb, s // TR),
        in_specs=[pl.BlockSpec((1, TR, d), lambda b, t: (b, t, 0)),
                  pl.BlockSpec((1, N_MOD, d), _mod_index(nb, nct)),
                  pl.BlockSpec((1, d), lambda b, t: (0, 0)),
                  pl.BlockSpec((d, na), lambda b, t: (0, 0)),
                  pl.BlockSpec((d, nr), lambda b, t: (0, 0))],
        out_specs=[pl.BlockSpec((1, TR, na), lambda b, t: (b, t, 0)),
                   pl.BlockSpec((1, TR, nr), lambda b, t: (b, t, 0))],
        out_shape=[jax.ShapeDtypeStruct((nb, s, na), F32),
                   jax.ShapeDtypeStruct((nb, s, nr), F32)],
        compiler_params=_cparams(("arbitrary", "arbitrary")),
        name="in_proj",
    )(xa, mod, g.reshape(1, d), wa, wr)


def _rope(x, cos, sin_lo, sin_hi):
    n = x.shape[-1]
    return x * cos + pltpu.roll(x, n - ROPE_HALF, 1) * sin_lo + pltpu.roll(x, ROPE_HALF, 1) * sin_hi


def _attn_prep_kernel(pa_ref, cos_ref, slo_ref, shi_ref, qg_ref, kg_ref, q_ref, k_ref, v_ref):
    pa = pa_ref[0]
    qk = pa[:, :ATTN_WIDTH + KV_WIDTH]
    v = pa[:, ATTN_WIDTH + KV_WIDTH:ATTN_IN]
    cos, slo, shi = cos_ref[...], slo_ref[...], shi_ref[...]
    inv = lax.rsqrt(_head_sums(qk * qk, 1.0 / HEAD_DIM) + EPS)
    qn = qk[:, :ATTN_WIDTH] * inv[:, :ATTN_WIDTH] * qg_ref[...]
    qr = _rope(qn, cos, slo, shi) * (ATTN_SCALE * LOG2E)
    kn = qk[:, ATTN_WIDTH:] * inv[:, ATTN_WIDTH:] * kg_ref[...]
    kr = _rope(kn, cos[:, :KV_WIDTH], slo[:, :KV_WIDTH], shi[:, :KV_WIDTH])
    lane = lax.broadcasted_iota(jnp.int32, (TR, LANES), 1)
    low = lane < HEAD_DIM
    for j in range(ATTN_KV_HEADS):
        kj = kr if j % PAIR == 0 else pltpu.roll(kr, HEAD_DIM, 1)
        vj = v if j % PAIR == 0 else pltpu.roll(v, HEAD_DIM, 1)
        k_ref[0, j] = jnp.where(low, kj, 0.0).astype(BF16)
        v_ref[0, j] = jnp.where(low, vj, jnp.where(lane == HEAD_DIM, 1.0, 0.0)).astype(BF16)
    for h in range(ATTN_HEADS):
        src = qr[:, (h // PAIR) * LANES:(h // PAIR + 1) * LANES]
        if h % PAIR:
            src = pltpu.roll(src, HEAD_DIM, 1)
        q_ref[0, h] = jnp.where(low, src, 0.0).astype(BF16)


def _attn_prep(pa, cos, slo, shi, q_gain, k_gain):
    nb, s, _ = pa.shape
    qg = jnp.tile(q_gain, ATTN_HEADS).reshape(1, ATTN_WIDTH)
    kg = jnp.tile(k_gain, ATTN_KV_HEADS).reshape(1, KV_WIDTH)
    tab = pl.BlockSpec((TR, ATTN_WIDTH), lambda t, b: (t, 0))
    return pl.pallas_call(
        _attn_prep_kernel,
        grid=(s // TR, nb),
        in_specs=[pl.BlockSpec((1, TR, ATTN_IN), lambda t, b: (b, t, 0)), tab, tab, tab,
                  pl.BlockSpec((1, ATTN_WIDTH), lambda t, b: (0, 0)),
                  pl.BlockSpec((1, KV_WIDTH), lambda t, b: (0, 0))],
        out_specs=[pl.BlockSpec((1, ATTN_HEADS, TR, LANES), lambda t, b: (b, 0, t, 0)),
                   pl.BlockSpec((1, ATTN_KV_HEADS, TR, LANES), lambda t, b: (b, 0, t, 0)),
                   pl.BlockSpec((1, ATTN_KV_HEADS, TR, LANES), lambda t, b: (b, 0, t, 0))],
        out_shape=[jax.ShapeDtypeStruct((nb, ATTN_HEADS, s, LANES), BF16),
                   jax.ShapeDtypeStruct((nb, ATTN_KV_HEADS, s, LANES), BF16),
                   jax.ShapeDtypeStruct((nb, ATTN_KV_HEADS, s, LANES), BF16)],
        compiler_params=_cparams(("arbitrary", "arbitrary")),
        name="attn_prep",
    )(pa, cos, slo, shi, qg, kg)


def _rope_tables(lc, l):
    rows = l // GRID_W
    row = jnp.repeat(jnp.arange(rows, dtype=F32), GRID_W)
    col = jnp.tile(jnp.arange(GRID_W, dtype=F32), rows)
    inv = ROPE_THETA ** (-jnp.arange(0, ROPE_HALF, 2, dtype=F32) / ROPE_HALF)
    ang = jnp.concatenate([row[:, None] * inv, col[:, None] * inv], axis=-1)
    cos = jnp.concatenate([jnp.ones((lc, ROPE_HALF), F32), jnp.cos(ang)], axis=0)
    sin = jnp.concatenate([jnp.zeros((lc, ROPE_HALF), F32), jnp.sin(ang)], axis=0)
    zero = jnp.zeros_like(sin)
    head = lambda lo, hi: jnp.tile(jnp.concatenate([lo, hi], axis=-1), (1, ATTN_HEADS))
    return head(cos, cos), head(-sin, zero), head(zero, sin)


def _attn_kernel(q_ref, k_ref, v_ref, o_ref, s_ref, *, nct, lc):
    t = pl.program_id(2)
    n_all = k_ref.shape[2]
    low = lax.broadcasted_iota(jnp.int32, (TR, LANES), 1) < HEAD_DIM

    def run(n_keys):
        tiles = [slice(j * ATTN_KT, (j + 1) * ATTN_KT) for j in range(n_keys // ATTN_KT)]
        mx, acc = [None] * ATTN_GROUP, [None] * ATTN_GROUP
        m = [None] * ATTN_GROUP
        for stage in range(ATTN_GROUP + 1):
            g1, g2 = stage, stage - 1
            if g1 < ATTN_GROUP:
                mx[g1] = jnp.full((TR, LANES), -jnp.inf, F32)
            if g2 >= 0:
                m[g2] = jnp.max(mx[g2], axis=-1, keepdims=True)
                acc[g2] = jnp.zeros((TR, LANES), F32)
            for ks in tiles:
                if g1 < ATTN_GROUP:
                    s = _dot_nt(q_ref[0, g1], k_ref[0, 0, ks, :])
                    s_ref[g1, :, ks] = s
                    for c in range(ATTN_KT // LANES):
                        mx[g1] = jnp.maximum(mx[g1], s[:, c * LANES:(c + 1) * LANES])
                if g2 >= 0:
                    p = jnp.exp2(s_ref[g2, :, ks] - m[g2]).astype(BF16)
                    acc[g2] = acc[g2] + _dot(p, v_ref[0, 0, ks, :])
        out = [a / a[:, HEAD_DIM:HEAD_DIM + 1] for a in acc]
        for pr in range(ATTN_GROUP // PAIR):
            hi = pltpu.roll(out[PAIR * pr + 1], HEAD_DIM, 1)
            o_ref[0, :, pr * LANES:(pr + 1) * LANES] = jnp.where(low, out[PAIR * pr], hi).astype(BF16)

    @pl.when(t < nct)
    def _():
        run(lc)

    @pl.when(t >= nct)
    def _():
        run(n_all)


def _attention(q, k, v, nct, lc):
    nb, _, s, _ = q.shape
    gw = ATTN_GROUP * HEAD_DIM
    kv_spec = pl.BlockSpec((1, 1, s, LANES), lambda b, j, t: (b, j, 0, 0))
    return pl.pallas_call(
        functools.partial(_attn_kernel, nct=nct, lc=lc),
        grid=(nb, ATTN_KV_HEADS, s // TR),
        in_specs=[pl.BlockSpec((1, ATTN_GROUP, TR, LANES), lambda b, j, t: (b, j, t, 0)), kv_spec, kv_spec],
        out_specs=pl.BlockSpec((1, TR, gw), lambda b, j, t: (b, t, j)),
        out_shape=jax.ShapeDtypeStruct((nb, s, ATTN_WIDTH), BF16),
        scratch_shapes=[pltpu.VMEM((ATTN_GROUP, TR, s), F32)],
        compiler_params=_cparams(("arbitrary", "arbitrary", "arbitrary")),
        name="attention",
    )(q, k, v)


def _token_shift(p, prev_row, next_row, mu):
    n = p.shape[0]
    row = lax.broadcasted_iota(jnp.int32, p.shape, 0)
    prev = jnp.where(row == 0, prev_row, pltpu.roll(p, 1, 0))
    nxt = jnp.where(row == n - 1, next_row, pltpu.roll(p, n - 1, 0))
    return p + mu * (0.5 * (prev + nxt) - p)


def _halo_rows(prev_ref, next_ref, tile, nct, ng):
    first = jnp.logical_or(tile == 0, tile == nct)
    last = jnp.logical_or(tile == nct - 1, tile == ng - 1)
    prev_row = jnp.where(first, 0.0, prev_ref[0, SUBLANES - 1:SUBLANES, :])
    next_row = jnp.where(last, 0.0, next_ref[0, 0:1, :])
    return prev_row, next_row


def _rwkv_tile(rev, g, nct, ng):
    if not rev:
        return g
    return jnp.where(g < nct, nct - 1 - g, ng - 1 - (g - nct))


def _block_diag(x):
    z = jnp.zeros((CH, CH), x.dtype)
    return jnp.concatenate([jnp.concatenate([x[:, :CH], z], axis=1), jnp.concatenate([z, x[:, CH:]], axis=1)], axis=0)


def _head_rows(x):
    h0 = (lax.broadcasted_iota(jnp.int32, x.shape, 1) & (LANES - 1)) < HEAD_DIM
    return jnp.concatenate([jnp.where(h0, x, 0.0), jnp.where(h0, 0.0, x)], axis=0).astype(BF16)


def _rwkv_dir_kernel(p_ref, prev_ref, next_ref, mu_ref, kkw_ref, ka_ref, rk_ref, w0_ref, a0_ref, w2_ref, a2_ref,
                     o_ref, bv_ref, s_ref, *, rev, nct, ng):
    g = pl.program_id(1)
    tile = _rwkv_tile(rev, g, nct, ng)
    w = RWKV_WIDTH

    @pl.when(g == 0)
    def _():
        s_ref[...] = jnp.zeros_like(s_ref)

    prev_row, next_row = _halo_rows(prev_ref, next_ref, tile, nct, ng)
    ps = _token_shift(p_ref[0], prev_row, next_row, mu_ref[...])
    r, k, v = ps[:, :w], ps[:, w:2 * w], ps[:, 2 * w:3 * w]
    wd = ps[:, 3 * w:3 * w + 2 * DECAY_LORA]
    ad = ps[:, 3 * w + 2 * DECAY_LORA:3 * w + 2 * DECAY_LORA + 2 * ICLR_LORA]
    kk = k * kkw_ref[...]
    kk = kk * lax.rsqrt(_head_sums(kk * kk) + 1e-12)
    x = w0_ref[...] + _dot3(jnp.tanh(wd), w2_ref[...])
    lw = (-math.exp(-0.5)) * _sigmoid(x)
    a = _sigmoid(a0_ref[...] + _dot3(ad, a2_ref[...]))
    k_d = k * (1.0 + (a - 1.0) * ka_ref[...])
    bv_ref[0] = _head_sums(r * k_d * rk_ref[...]) * v
    beta = kk * a

    ti = lax.broadcasted_iota(jnp.int32, (CH, PAIR * CH), 0)
    tj = lax.broadcasted_iota(jnp.int32, (CH, PAIR * CH), 1) & (CH - 1)
    prec2 = (tj > ti) if rev else (tj < ti)
    eye2 = ti == tj
    incl2 = jnp.logical_or(prec2, eye2)
    late, early = (tj, ti) if rev else (ti, tj)
    join = [jnp.logical_and(jnp.logical_and((ti ^ tj) < 2 * b, (late & b) != 0), (early & b) == 0)
            for b in (2 ** m for m in range(int(math.log2(CH))))]
    si = lax.broadcasted_iota(jnp.int32, (CH, CH), 0)
    sj = lax.broadcasted_iota(jnp.int32, (CH, CH), 1)
    eye = si == sj
    tri = jnp.where((sj >= si) if rev else (sj <= si), 1.0, 0.0).astype(BF16)
    same_head = (si // HEAD_DIM) == (sj // HEAD_DIM)
    n_chunks = TR // CH
    n_pairs = RWKV_HEADS // PAIR
    chunk_order = list(range(n_chunks - 1, -1, -1) if rev else range(n_chunks))
    units = [(ci, pp) for ci in chunk_order for pp in range(n_pairs)]
    each = lambda f, *lists: [f(*args) for args in zip(*lists)]

    cum = {}
    for ci in chunk_order:
        cs = _dot(tri, jnp.concatenate(_split3(lw[ci * CH:(ci + 1) * CH]), axis=1))
        cum[ci] = cs[:, :w] + cs[:, w:2 * w] + cs[:, 2 * w:]
    blk = lambda x: [x[ci * CH:(ci + 1) * CH, pp * LANES:(pp + 1) * LANES] for ci, pp in units]
    rx, vx, kap, lwx, kx, bet = blk(r), blk(v), blk(kk), blk(lw), blk(k_d), blk(beta)
    cumi = [cum[ci][:, pp * LANES:(pp + 1) * LANES] for ci, pp in units]
    tot = each(lambda c: c[0:1, :] if rev else c[CH - 1:CH, :], cumi)
    cc = each(lambda c: c - c[CH // 2:CH // 2 + 1, :], cumi)
    e_neg = each(lambda c: jnp.exp(-c), cc)
    e_end = each(lambda t, c: jnp.exp(t - c), tot, cumi)
    kap_t = each(lambda x, c, l: x * jnp.exp(c - l), kap, cc, lwx)
    r_t = each(lambda x, c: x * jnp.exp(c), rx, cc)
    kap_0 = each(lambda x, c, l: x * jnp.exp(c - l), kap, cumi, lwx)
    r_0 = each(lambda x, c: x * jnp.exp(c), rx, cumi)
    k_t = each(lambda x, e: x * e, kx, e_neg)
    bet_t = each(lambda x, e: x * e, bet, e_neg)
    k_e = each(lambda x, e: x * e, kx, e_end)
    bet_e = each(lambda x, e: x * e, bet, e_end)
    p_end = each(jnp.exp, tot)
    qs = each(lambda a, b: jnp.concatenate([a, b], axis=0).astype(BF16), kap_t, r_t)
    s_b = each(lambda q, x: _dot_nt(q, _head_rows(x)), qs, bet_t)
    s_k = each(lambda q, x: _dot_nt(q, _head_rows(x)), qs, k_t)
    a_b = each(lambda s: jnp.where(prec2, s[:CH], 0.0), s_b)
    a_k = each(lambda s: jnp.where(prec2, s[:CH], 0.0).astype(BF16), s_k)
    b_b = each(lambda s: jnp.where(incl2, s[CH:], 0.0).astype(BF16), s_b)
    b_k = each(lambda s: jnp.where(incl2, s[CH:], 0.0).astype(BF16), s_k)
    ts = each(lambda a: jnp.where(eye2, 1.0, jnp.where(join[0], -a, 0.0)), a_b)
    for lvl in range(1, len(join)):
        q_b = each(lambda a: _block_diag(jnp.where(join[lvl], a, 0.0).astype(BF16)), a_b)
        t_b = each(lambda t: t.astype(BF16), ts)
        dq = each(lambda t, q: _dot(t, q).astype(BF16), t_b, q_b)
        ts = each(lambda t, e, tb: t - _dot(e, _block_diag(tb)), ts, dq, t_b)
    t_b = each(lambda t: t.astype(BF16), ts)
    v_h = each(_head_rows, vx)
    av = each(_dot, a_k, v_h)
    rhs = each(lambda k0, a: _head_rows(jnp.concatenate([k0, a], axis=1)), kap_0, av)
    wz = each(_dot, t_b, rhs)
    kq = each(lambda x: x[:, :LANES], wz)
    z0 = each(lambda x: x[:, LANES:], wz)
    bz = each(lambda b, x: _dot(b, _head_rows(x)), b_b, wz)
    o0k = each(_dot, b_k, v_h)
    r_new = each(lambda x, b: x - b[:, :LANES], r_0, bz)
    o0 = each(lambda a, b: a - b[:, LANES:], o0k, bz)
    m_c = each(lambda p, b, q: jnp.where(eye, p, 0.0) - jnp.where(same_head, _dot_tn(b, q), 0.0), p_end, bet_e, kq)
    g_c = each(lambda ke, be, x, z: jnp.where(same_head, _dot_tn(jnp.concatenate([ke, be], axis=0),
                                                                  jnp.concatenate([x, -z], axis=0)), 0.0),
               k_e, bet_e, vx, z0)

    states = [s_ref[pp] for pp in range(n_pairs)]
    outs = {}
    for u, (ci, pp) in enumerate(units):
        st = states[pp]
        outs[ci, pp] = _dot3(r_new[u], st) + o0[u]
        states[pp] = _dot3(m_c[u], st) + g_c[u]
    for ci in chunk_order:
        o_ref[0, ci * CH:(ci + 1) * CH, :] = jnp.concatenate([outs[ci, pp] for pp in range(n_pairs)], axis=1)
    for pp in range(n_pairs):
        s_ref[pp] = states[pp]


def _rwkv_scan(pr, lp, nct, rev):
    nb, s, pw = pr.shape
    ng = s // TR
    w = RWKV_WIDTH
    hb = TR // SUBLANES
    d = int(rev)
    tile = functools.partial(_rwkv_tile, rev, nct=nct, ng=ng)
    n_row_blocks = s // SUBLANES
    main = pl.BlockSpec((1, TR, pw), lambda b, g: (b, tile(g), 0))
    prev = pl.BlockSpec((1, SUBLANES, pw), lambda b, g: (b, jnp.maximum(tile(g) * hb - 1, 0), 0))
    nxt = pl.BlockSpec((1, SUBLANES, pw), lambda b, g: (b, jnp.minimum((tile(g) + 1) * hb, n_row_blocks - 1), 0))
    vec = lambda n: pl.BlockSpec((1, n), lambda b, g: (0, 0))
    mat = pl.BlockSpec((2 * DECAY_LORA, w), lambda b, g: (0, 0))
    out = pl.BlockSpec((1, TR, w), lambda b, g: (b, tile(g), 0))
    zeros = jnp.zeros((DECAY_LORA, w), F32)
    pad = lambda m: jnp.concatenate([zeros, m] if rev else [m, zeros])
    return pl.pallas_call(
        functools.partial(_rwkv_dir_kernel, rev=rev, nct=nct, ng=ng),
        grid=(nb, ng),
        in_specs=[main, prev, nxt, vec(pw), vec(w), vec(w), vec(w), vec(w), vec(w), mat, mat],
        out_specs=[out, out],
        out_shape=[jax.ShapeDtypeStruct((nb, s, w), F32), jax.ShapeDtypeStruct((nb, s, w), F32)],
        scratch_shapes=[pltpu.VMEM((RWKV_HEADS // PAIR, LANES, LANES), F32)],
        compiler_params=_cparams(("arbitrary", "arbitrary")),
        name="rwkv_scan_bwd" if rev else "rwkv_scan_fwd",
    )(pr, pr, pr, lp['mu'].reshape(1, pw), lp['kk'].reshape(1, w), lp['ka'].reshape(1, w), lp['rk'].reshape(1, w),
      lp['w0'][d].reshape(1, w), lp['a0'][d].reshape(1, w), pad(lp['w2'][d]), pad(lp['a2'][d]))


def _rwkv_out_kernel(of_ref, ob_ref, bf_ref, bb_ref, gd_ref, gprev_ref, gnext_ref, mu_ref, g2_ref, gw_ref, gb_ref,
                     y_ref, *, nct, ng):
    tile = pl.program_id(1)
    o = of_ref[0] + ob_ref[0]
    cen = o - _head_sums(o, 1.0 / HEAD_DIM)
    var = _head_sums(cen * cen, 1.0 / HEAD_DIM)
    y = cen * lax.rsqrt(var + GN_EPS) * gw_ref[...] + gb_ref[...] + (bf_ref[0] + bb_ref[0])
    prev_row, next_row = _halo_rows(gprev_ref, gnext_ref, tile, nct, ng)
    gd = _token_shift(gd_ref[0], prev_row, next_row, mu_ref[...])
    gate = _dot3(_sigmoid(gd), g2_ref[...])
    y_ref[0] = (y * gate).astype(BF16)


def _rwkv_out(o_f, o_b, bv_f, bv_b, pr, lp, nct):
    nb, s, w = o_f.shape
    ng = s // TR
    hb = TR // SUBLANES
    gcol = (SHIFT_WIDTH - GATE_LORA) // GATE_LORA
    n_row_blocks = s // SUBLANES
    row = pl.BlockSpec((1, TR, w), lambda b, t: (b, t, 0))
    vec = lambda n: pl.BlockSpec((1, n), lambda b, t: (0, 0))
    return pl.pallas_call(
        functools.partial(_rwkv_out_kernel, nct=nct, ng=ng),
        grid=(nb, ng),
        in_specs=[row, row, row, row,
                  pl.BlockSpec((1, TR, GATE_LORA), lambda b, t: (b, t, gcol)),
                  pl.BlockSpec((1, SUBLANES, GATE_LORA), lambda b, t: (b, jnp.maximum(t * hb - 1, 0), gcol)),
                  pl.BlockSpec((1, SUBLANES, GATE_LORA),
                               lambda b, t: (b, jnp.minimum((t + 1) * hb, n_row_blocks - 1), gcol)),
                  vec(GATE_LORA), pl.BlockSpec((GATE_LORA, w), lambda b, t: (0, 0)), vec(w), vec(w)],
        out_specs=row,
        out_shape=jax.ShapeDtypeStruct((nb, s, w), BF16),
        compiler_params=_cparams(("arbitrary", "arbitrary")),
        name="rwkv_out",
    )(o_f, o_b, bv_f, bv_b, pr, pr, pr, lp['mu'][SHIFT_WIDTH - GATE_LORA:].reshape(1, GATE_LORA), lp['g2'],
      lp['gn_w'].reshape(1, w), lp['gn_b'].reshape(1, w))


def _top2(logits):
    lane = lax.broadcasted_iota(jnp.int32, logits.shape, 1)
    v1 = jnp.max(logits, axis=-1, keepdims=True)
    i1 = jnp.min(jnp.where(logits == v1, lane, LANES), axis=-1, keepdims=True)
    rest = jnp.where(lane == i1, -jnp.inf, logits)
    v2 = jnp.max(rest, axis=-1, keepdims=True)
    i2 = jnp.min(jnp.where(rest == v2, lane, LANES), axis=-1, keepdims=True)
    e = jnp.exp(v2 - v1)
    g1 = 1.0 / (1.0 + e)
    g2 = e / (1.0 + e)
    out = jnp.where(lane == 0, i1.astype(F32), 0.0)
    out = jnp.where(lane == 1, i2.astype(F32), out)
    out = jnp.where(lane == 2, g1, out)
    return jnp.where(lane == 3, g2, out)


def _out_proj_kernel(x_ref, att_ref, rw_ref, mod_ref, g_ref, wa_ref, wr_ref, *rest, moe):
    if moe:
        wrt_ref, xo_ref, h_ref, route_ref = rest
    else:
        xo_ref, h_ref = rest
    mix = _dot(att_ref[0], wa_ref[...]) + _dot(rw_ref[0], wr_ref[...])
    x = x_ref[0] + mod_ref[0, 2:3] * mix
    xo_ref[0] = x
    h = _norm_mod(x, g_ref[...], mod_ref[0, 3:4], mod_ref[0, 4:5])
    h_ref[0] = h.astype(h_ref.dtype)
    if moe:
        lane = lax.broadcasted_iota(jnp.int32, (TR, LANES), 1)
        logits = jnp.where(lane < N_EXPERTS, _dot3(h, wrt_ref[...]), -jnp.inf)
        route_ref[0] = _top2(logits)


def _out_proj(xa, att, rw, mod, g, wo_a, wo_r, nct, w_router=None):
    nb, s, d = xa.shape
    moe = w_router is not None
    row = lambda n: pl.BlockSpec((1, TR, n), lambda b, t: (b, t, 0))
    full = lambda a: pl.BlockSpec(a.shape, lambda b, t: (0, 0))
    in_specs = [row(d), row(ATTN_WIDTH), row(RWKV_WIDTH), pl.BlockSpec((1, N_MOD, d), _mod_index(nb, nct)),
                pl.BlockSpec((1, d), lambda b, t: (0, 0)), full(wo_a), full(wo_r)]
    args = [xa, att, rw, mod, g.reshape(1, d), wo_a, wo_r]
    out_specs = [row(d), row(d)]
    out_shape = [jax.ShapeDtypeStruct((nb, s, d), F32), jax.ShapeDtypeStruct((nb, s, d), F32 if moe else BF16)]
    if moe:
        wrt = jnp.pad(w_router, ((0, 0), (0, LANES - N_EXPERTS)))
        in_specs.append(full(wrt))
        args.append(wrt)
        out_specs.append(row(LANES))
        out_shape.append(jax.ShapeDtypeStruct((nb, s, LANES), F32))
    return pl.pallas_call(
        functools.partial(_out_proj_kernel, moe=moe),
        grid=(nb, s // TR),
        in_specs=in_specs, out_specs=out_specs, out_shape=out_shape,
        compiler_params=_cparams(("arbitrary", "arbitrary")),
        name="out_proj_moe" if moe else "out_proj",
    )(*args)


def _ffn_kernel(x_ref, h_ref, mod_ref, wg_ref, wu_ref, wd_ref, o_ref):
    h = h_ref[0]
    gt = _dot(h, wg_ref[...])
    up = _dot(h, wu_ref[...])
    act = (gt * _sigmoid(gt) * up).astype(BF16)
    o_ref[0] = x_ref[0] + mod_ref[0, 5:6] * _dot(act, wd_ref[...])


def _ffn_dense(xa, h, mod, wg, wu, wd, nct):
    nb, s, d = xa.shape
    row = pl.BlockSpec((1, TR, d), lambda b, t: (b, t, 0))
    resident = lambda a: pl.BlockSpec(a.shape, lambda b, t: (0, 0), pipeline_mode=pl.Buffered(1))
    return pl.pallas_call(
        _ffn_kernel,
        grid=(nb, s // TR),
        in_specs=[row, row, pl.BlockSpec((1, N_MOD, d), _mod_index(nb, nct)), resident(wg), resident(wu),
                  resident(wd)],
        out_specs=row,
        out_shape=jax.ShapeDtypeStruct((nb, s, d), F32),
        compiler_params=_cparams(("arbitrary", "arbitrary")),
        name="ffn_dense",
    )(xa, h, mod, wg, wu, wd)


def _row_copy(src_hbm, idx_ref, dst_ref, sem, r):
    return pltpu.make_async_copy(src_hbm.at[pl.ds(idx_ref[0, 0, r], 1), :], dst_ref.at[pl.ds(r, 1), :], sem)


def _moe_kernel(be_ref, nu_ref, tok_ref, tok_next_ref, h_hbm, wg_ref, wu_ref, wd_ref, o_ref, x_buf, sem):
    i = pl.program_id(0)
    j = pl.program_id(1)
    nblk = pl.num_programs(0)
    last_j = pl.num_programs(1) - 1
    used = i < nu_ref[0]
    slot = i % 2
    cur, nxt = x_buf.at[slot], x_buf.at[1 - slot]
    rows = cur.shape[0]

    def wait_rows(idx_ref, dst, s):
        def body(r, carry):
            _row_copy(h_hbm, idx_ref, dst, s, r).wait()
            return carry
        lax.fori_loop(0, rows, body, 0, unroll=8)

    @pl.when(j == 0)
    def _():
        o_ref[...] = jnp.zeros_like(o_ref)

        @pl.when(i == 0)
        def _():
            def body(r, carry):
                _row_copy(h_hbm, tok_ref, cur, sem.at[slot], r).start()
                return carry
            lax.fori_loop(0, rows, body, 0, unroll=8)

        @pl.when(jnp.logical_or(i == 0, i - 1 < nu_ref[0]))
        def _():
            wait_rows(tok_ref, cur, sem.at[slot])

    def compute():
        x = cur[...].astype(BF16)
        gt = _dot(x, wg_ref[0])
        up = _dot(x, wu_ref[0])
        act = (gt * _sigmoid(gt) * up).astype(BF16)
        o_ref[...] += _dot(act, wd_ref[0])

    @pl.when(jnp.logical_and(used, j != last_j))
    def _():
        compute()

    @pl.when(jnp.logical_and(used, j == last_j))
    def _():
        compute()
        for r in range(rows):
            _row_copy(h_hbm, tok_next_ref, nxt, sem.at[1 - slot], r).start()

        @pl.when(i == nblk - 1)
        def _():
            wait_rows(tok_next_ref, nxt, sem.at[1 - slot])


def _moe_experts(h, slot_token, block_expert, n_used, wg, wu, wd):
    t, d = h.shape
    dff = wg.shape[2]
    nblk = slot_token.shape[0] // MOE_BM
    tok = slot_token.reshape(nblk, 1, MOE_BM)
    grid_spec = pltpu.PrefetchScalarGridSpec(
        num_scalar_prefetch=2,
        grid=(nblk, dff // MOE_TF),
        in_specs=[pl.BlockSpec((1, 1, MOE_BM), lambda i, j, be, nu: (i, 0, 0), memory_space=pltpu.SMEM),
                  pl.BlockSpec((1, 1, MOE_BM), lambda i, j, be, nu: (jnp.minimum(i + 1, nblk - 1), 0, 0),
                               memory_space=pltpu.SMEM),
                  pl.BlockSpec(memory_space=pl.ANY),
                  pl.BlockSpec((1, d, MOE_TF), lambda i, j, be, nu: (be[i], 0, j)),
                  pl.BlockSpec((1, d, MOE_TF), lambda i, j, be, nu: (be[i], 0, j)),
                  pl.BlockSpec((1, MOE_TF, d), lambda i, j, be, nu: (be[i], j, 0))],
        out_specs=pl.BlockSpec((MOE_BM, d), lambda i, j, be, nu: (i, 0)),
        scratch_shapes=[pltpu.VMEM((2, MOE_BM, d), F32), pltpu.SemaphoreType.DMA((2,))],
    )
    return pl.pallas_call(
        _moe_kernel,
        grid_spec=grid_spec,
        out_shape=jax.ShapeDtypeStruct((nblk * MOE_BM, d), F32),
        compiler_params=_cparams(("arbitrary", "arbitrary")),
        name="moe_experts",
    )(block_expert, n_used, tok, tok, h, wg, wu, wd)


def _gather_rows(src_hbm, idx_ref, dst_ref, sem):
    n = dst_ref.shape[0]

    def start(r, carry):
        _row_copy(src_hbm, idx_ref, dst_ref, sem, r).start()
        return carry

    def wait(r, carry):
        _row_copy(src_hbm, idx_ref, dst_ref, sem, r).wait()
        return carry

    lax.fori_loop(0, n, start, 0, unroll=8)
    lax.fori_loop(0, n, wait, 0, unroll=8)


def _combine_kernel(s1_ref, s2_ref, x_ref, route_ref, mod_ref, ys_hbm, o_ref, y1_ref, y2_ref, sem):
    _gather_rows(ys_hbm, s1_ref, y1_ref, sem.at[0])
    _gather_rows(ys_hbm, s2_ref, y2_ref, sem.at[1])
    route = route_ref[0]
    f = y1_ref[...] * route[:, 2:3] + y2_ref[...] * route[:, 3:4]
    o_ref[0] = x_ref[0] + mod_ref[0, 5:6] * f


def _moe_combine(xa, route, mod, ys, slot1, slot2, nct):
    nb, s, d = xa.shape
    ng = s // TR
    idx = pl.BlockSpec((1, 1, TR), lambda b, t: (b * ng + t, 0, 0), memory_space=pltpu.SMEM)
    row = lambda n: pl.BlockSpec((1, TR, n), lambda b, t: (b, t, 0))
    return pl.pallas_call(
        _combine_kernel,
        grid=(nb, ng),
        in_specs=[idx, idx, row(d), row(LANES), pl.BlockSpec((1, N_MOD, d), _mod_index(nb, nct)),
                  pl.BlockSpec(memory_space=pl.ANY)],
        out_specs=row(d),
        out_shape=jax.ShapeDtypeStruct((nb, s, d), F32),
        scratch_shapes=[pltpu.VMEM((TR, d), F32), pltpu.VMEM((TR, d), F32), pltpu.SemaphoreType.DMA((2,))],
        compiler_params=_cparams(("arbitrary", "arbitrary")),
        name="moe_combine",
    )(slot1.reshape(nb * ng, 1, TR), slot2.reshape(nb * ng, 1, TR), xa, route, mod, ys)


def _moe_ffn(xa, h, route, mod, wg, wu, wd, nct):
    nb, s, d = xa.shape
    t = nb * s
    n_assign = 2 * t
    nblk = (n_assign + N_EXPERTS * (MOE_BM - 1) + MOE_BM - 1) // MOE_BM
    n_slots = nblk * MOE_BM
    expert = route.reshape(t, LANES)[:, 0:2].astype(jnp.int32).reshape(-1)
    onehot = (expert[:, None] == jnp.arange(N_EXPERTS, dtype=jnp.int32)[None, :]).astype(jnp.int32)
    rank = jnp.sum((jnp.cumsum(onehot, axis=0) - onehot) * onehot, axis=1)
    counts = jnp.sum(onehot, axis=0)
    padded = (counts + MOE_BM - 1) // MOE_BM * MOE_BM
    pad_end = jnp.cumsum(padded)
    pad_start = pad_end - padded
    slot = pad_start[expert] + rank
    token = jnp.repeat(jnp.arange(t, dtype=jnp.int32), 2)
    slot_token = jnp.zeros((n_slots,), jnp.int32).at[slot].set(token)
    block_start = jnp.arange(nblk, dtype=jnp.int32) * MOE_BM
    block_expert = jnp.minimum(jnp.sum((pad_end[None, :] <= block_start[:, None]).astype(jnp.int32), axis=1),
                               N_EXPERTS - 1)
    n_used = (pad_end[-1] // MOE_BM).astype(jnp.int32).reshape(1)
    ys = _moe_experts(h.reshape(t, d), slot_token, block_expert, n_used, wg, wu, wd)
    slot2 = slot.reshape(t, 2)
    return _moe_combine(xa, route, mod, ys, slot2[:, 0], slot2[:, 1], nct)


def kernel(x, c, ctx, c_ctx, ada_w, ada_b, norm1_g, norm2_g, w_in, w_out, q_gain, k_gain, shift_mu, rw_w0, rw_w2,
           rw_a0, rw_a2, rw_g2, rw_kk, rw_ka, rw_rk, rw_gn_w, rw_gn_b, ffn_wg, ffn_wu, ffn_wd, moe_router, moe_wg,
           moe_wu, moe_wd):
    nb, l, d = x.shape
    lc = ctx.shape[1]
    depth = ada_w.shape[0]
    assert lc % TR == 0 and l % TR == 0 and l % GRID_W == 0
    nct = lc // TR
    xa = jnp.concatenate([ctx, x], axis=1)

    mod_rows = -(-(nb + 1) // SUBLANES) * SUBLANES
    c_all = jnp.concatenate([c, c_ctx[None], jnp.zeros((mod_rows - nb - 1, d), F32)], axis=0)
    mod_all = _ada_all(c_all, ada_w, ada_b)[:, :nb + 1].reshape(depth, nb + 1, N_MOD, d)
    cos, slo, shi = _rope_tables(lc, l)

    for i in range(depth):
        mod = mod_all[i]
        lp = dict(mu=shift_mu[i], w0=rw_w0[i], w2=rw_w2[i], a0=rw_a0[i], a2=rw_a2[i], g2=rw_g2[i], kk=rw_kk[i],
                  ka=rw_ka[i], rk=rw_rk[i], gn_w=rw_gn_w[i], gn_b=rw_gn_b[i])
        wi = w_in[i].astype(BF16)
        pa, pr = _in_proj(xa, mod, norm1_g[i], wi[:, :ATTN_IN], wi[:, ATTN_IN:], nct)
        q, k, v = _attn_prep(pa, cos, slo, shi, q_gain[i], k_gain[i])
        att = _attention(q, k, v, nct, lc)
        o_f, bv_f = _rwkv_scan(pr, lp, nct, False)
        o_b, bv_b = _rwkv_scan(pr, lp, nct, True)
        rw = _rwkv_out(o_f, o_b, bv_f, bv_b, pr, lp, nct)
        wo = w_out[i].astype(BF16)
        j = i // 2
        if i % 2 == 0:
            xa, h = _out_proj(xa, att, rw, mod, norm2_g[i], wo[:ATTN_WIDTH], wo[ATTN_WIDTH:], nct)
            xa = _ffn_dense(xa, h, mod, ffn_wg[j].astype(BF16), ffn_wu[j].astype(BF16), ffn_wd[j].astype(BF16), nct)
        else:
            xa, h, route = _out_proj(xa, att, rw, mod, norm2_g[i], wo[:ATTN_WIDTH], wo[ATTN_WIDTH:], nct,
                                     moe_router[j])
            xa = _moe_ffn(xa, h, route, mod, moe_wg[j].astype(BF16), moe_wu[j].astype(BF16), moe_wd[j].astype(BF16),
                          nct)
    return xa[:, lc:]
```

```python
import functools
import math

import jax
import jax.numpy as jnp
from jax import lax
from jax.experimental import pallas as pl
from jax.experimental.pallas import tpu as pltpu

F32 = jnp.float32
BF16 = jnp.bfloat16
HI = lax.Precision.HIGHEST

HEAD_DIM = 64
ROPE_HALF = HEAD_DIM // 2
ROPE_THETA = 10000.0
GRID_W = 64
ATTN_HEADS = 8
ATTN_KV_HEADS = 2
ATTN_GROUP = ATTN_HEADS // ATTN_KV_HEADS
ATTN_WIDTH = ATTN_HEADS * HEAD_DIM
KV_WIDTH = ATTN_KV_HEADS * HEAD_DIM
ATTN_IN = ATTN_WIDTH + 2 * KV_WIDTH
ATTN_SCALE = HEAD_DIM ** -0.5
LOG2E = math.log2(math.e)
RWKV_HEADS = 8
RWKV_WIDTH = RWKV_HEADS * HEAD_DIM
DECAY_LORA = 64
ICLR_LORA = 64
GATE_LORA = 128
SHIFT_WIDTH = 3 * RWKV_WIDTH + 2 * DECAY_LORA + 2 * ICLR_LORA + GATE_LORA
N_EXPERTS = 8
N_MOD = 6
EPS = 1e-6
GN_EPS = 64e-5

LANES = 128
SUBLANES = 8
TR = 256
CH = 128
PAIR = LANES // HEAD_DIM
MXU_DIM = 256
ATTN_KT = 256
MOE_BM = 512
MOE_TF = 1792
VMEM_LIMIT = 56 * 1024 * 1024


def _cparams(sem):
    return pltpu.CompilerParams(dimension_semantics=sem, vmem_limit_bytes=VMEM_LIMIT)


def _dot(a, b, prec=None):
    return jnp.dot(a, b, preferred_element_type=F32, precision=prec)


def _dot_nt(a, b, prec=None):
    return lax.dot_general(a, b, (((1,), (1,)), ((), ())), preferred_element_type=F32, precision=prec)


def _dot_tn(a, b, prec=None):
    return lax.dot_general(a, b, (((0,), (0,)), ((), ())), preferred_element_type=F32, precision=prec)


def _sigmoid(x):
    return 1.0 / (1.0 + jnp.exp(-x))


def _split2(x):
    hi = x.astype(BF16)
    return hi, (x - hi.astype(F32)).astype(BF16)


def _split3(x):
    hi = x.astype(BF16)
    r = x - hi.astype(F32)
    mid = r.astype(BF16)
    return hi, mid, (r - mid.astype(F32)).astype(BF16)


def _dot3(a, b):
    a_hi, a_lo = _split2(a)
    b_hi, b_lo = _split2(b)
    return _dot(a_hi, b_hi) + (_dot(a_hi, b_lo) + _dot(a_lo, b_hi))


def _head_sums(x, scale=1.0):
    r, n = x.shape
    i = lax.broadcasted_iota(jnp.int32, (MXU_DIM, MXU_DIM), 0) // HEAD_DIM
    j = lax.broadcasted_iota(jnp.int32, (MXU_DIM, MXU_DIM), 1) // HEAD_DIM
    ones = jnp.where(i == j, scale, 0.0).astype(BF16)
    nfull = n // MXU_DIM
    parts = []
    if nfull:
        xs = jnp.concatenate([x[:, c * MXU_DIM:(c + 1) * MXU_DIM] for c in range(nfull)], axis=0)
        m = nfull * r
        s = _dot(jnp.concatenate(_split2(xs), axis=0), ones)
        s = s[:m] + s[m:]
        parts += [s[c * r:(c + 1) * r] for c in range(nfull)]
    if n % MXU_DIM:
        w = n % MXU_DIM
        s = _dot(jnp.concatenate(_split2(x[:, nfull * MXU_DIM:]), axis=0), ones[:w, :w])
        parts.append(s[:r] + s[r:])
    return jnp.concatenate(parts, axis=1)


def _ada_kernel(c_ref, w_ref, b_ref, o_ref):
    c = c_ref[...]
    sc = c * _sigmoid(c)
    o_ref[0] = _dot(sc, w_ref[0], HI) + b_ref[0]


def _ada_all(c_all, ada_w, ada_b):
    depth, d, n = ada_w.shape
    rows = c_all.shape[0]
    tn = 1536
    return pl.pallas_call(
        _ada_kernel,
        grid=(depth, n // tn),
        in_specs=[pl.BlockSpec((rows, d), lambda i, j: (0, 0)),
                  pl.BlockSpec((1, d, tn), lambda i, j: (i, 0, j)),
                  pl.BlockSpec((1, 1, tn), lambda i, j: (i, 0, j))],
        out_specs=pl.BlockSpec((1, rows, tn), lambda i, j: (i, 0, j)),
        out_shape=jax.ShapeDtypeStruct((depth, rows, n), F32),
        compiler_params=_cparams(("arbitrary", "arbitrary")),
        name="ada_mod",
    )(c_all, ada_w, ada_b.reshape(depth, 1, n))


def _norm_mod(x, g, shift, scale):
    y = x * lax.rsqrt(jnp.mean(x * x, axis=-1, keepdims=True) + EPS)
    return (y * g) * (1.0 + scale) + shift


def _in_proj_kernel(x_ref, mod_ref, g_ref, wa_ref, wr_ref, pa_ref, pr_ref):
    h = _norm_mod(x_ref[0], g_ref[...], mod_ref[0, 0:1], mod_ref[0, 1:2]).astype(BF16)
    pa_ref[0] = _dot(h, wa_ref[...])
    pr_ref[0] = _dot(h, wr_ref[...])


def _mod_index(nb, nct):
    return lambda b, t: (jnp.where(t < nct, nb, b), 0, 0)


def _in_proj(xa, mod, g, wa, wr, nct):
    nb, s, d = xa.shape
    na, nr = wa.shape[1], wr.shape[1]
    return pl.pallas_call(
        _in_proj_kernel,
        grid=(nb, s // TR),
        in_specs=[pl.BlockSpec((1, TR, d), lambda b, t: (b, t, 0)),
                  pl.BlockSpec((1, N_MOD, d), _mod_index(nb, nct)),
                  pl.BlockSpec((1, d), lambda b, t: (0, 0)),
                  pl.BlockSpec((d, na), lambda b, t: (0, 0)),
                  pl.BlockSpec((d, nr), lambda b, t: (0, 0))],
        out_specs=[pl.BlockSpec((1, TR, na), lambda b, t: (b, t, 0)),
                   pl.BlockSpec((1, TR, nr), lambda b, t: (b, t, 0))],
        out_shape=[jax.ShapeDtypeStruct((nb, s, na), F32),
                   jax.ShapeDtypeStruct((nb, s, nr), F32)],
        compiler_params=_cparams(("arbitrary", "arbitrary")),
        name="in_proj",
    )(xa, mod, g.reshape(1, d), wa, wr)


def _rope(x, cos, sin_lo, sin_hi):
    n = x.shape[-1]
    return x * cos + pltpu.roll(x, n - ROPE_HALF, 1) * sin_lo + pltpu.roll(x, ROPE_HALF, 1) * sin_hi


def _attn_prep_kernel(pa_ref, cos_ref, slo_ref, shi_ref, qg_ref, kg_ref, q_ref, k_ref, v_ref):
    pa = pa_ref[0]
    qk = pa[:, :ATTN_WIDTH + KV_WIDTH]
    v = pa[:, ATTN_WIDTH + KV_WIDTH:ATTN_IN]
    cos, slo, shi = cos_ref[...], slo_ref[...], shi_ref[...]
    inv = lax.rsqrt(_head_sums(qk * qk, 1.0 / HEAD_DIM) + EPS)
    qn = qk[:, :ATTN_WIDTH] * inv[:, :ATTN_WIDTH] * qg_ref[...]
    qr = _rope(qn, cos, slo, shi) * (ATTN_SCALE * LOG2E)
    kn = qk[:, ATTN_WIDTH:] * inv[:, ATTN_WIDTH:] * kg_ref[...]
    kr = _rope(kn, cos[:, :KV_WIDTH], slo[:, :KV_WIDTH], shi[:, :KV_WIDTH])
    lane = lax.broadcasted_iota(jnp.int32, (TR, LANES), 1)
    low = lane < HEAD_DIM
    for j in range(ATTN_KV_HEADS):
        kj = kr if j % PAIR == 0 else pltpu.roll(kr, HEAD_DIM, 1)
        vj = v if j % PAIR == 0 else pltpu.roll(v, HEAD_DIM, 1)
        k_ref[0, j] = jnp.where(low, kj, 0.0).astype(BF16)
        v_ref[0, j] = jnp.where(low, vj, jnp.where(lane == HEAD_DIM, 1.0, 0.0)).astype(BF16)
    for h in range(ATTN_HEADS):
        src = qr[:, (h // PAIR) * LANES:(h // PAIR + 1) * LANES]
        if h % PAIR:
            src = pltpu.roll(src, HEAD_DIM, 1)
        q_ref[0, h] = jnp.where(low, src, 0.0).astype(BF16)


def _attn_prep(pa, cos, slo, shi, q_gain, k_gain):
    nb, s, _ = pa.shape
    qg = jnp.tile(q_gain, ATTN_HEADS).reshape(1, ATTN_WIDTH)
    kg = jnp.tile(k_gain, ATTN_KV_HEADS).reshape(1, KV_WIDTH)
    tab = pl.BlockSpec((TR, ATTN_WIDTH), lambda t, b: (t, 0))
    return pl.pallas_call(
        _attn_prep_kernel,
        grid=(s // TR, nb),
        in_specs=[pl.BlockSpec((1, TR, ATTN_IN), lambda t, b: (b, t, 0)), tab, tab, tab,
                  pl.BlockSpec((1, ATTN_WIDTH), lambda t, b: (0, 0)),
                  pl.BlockSpec((1, KV_WIDTH), lambda t, b: (0, 0))],
        out_specs=[pl.BlockSpec((1, ATTN_HEADS, TR, LANES), lambda t, b: (b, 0, t, 0)),
                   pl.BlockSpec((1, ATTN_KV_HEADS, TR, LANES), lambda t, b: (b, 0, t, 0)),
                   pl.BlockSpec((1, ATTN_KV_HEADS, TR, LANES), lambda t, b: (b, 0, t, 0))],
        out_shape=[jax.ShapeDtypeStruct((nb, ATTN_HEADS, s, LANES), BF16),
                   jax.ShapeDtypeStruct((nb, ATTN_KV_HEADS, s, LANES), BF16),
                   jax.ShapeDtypeStruct((nb, ATTN_KV_HEADS, s, LANES), BF16)],
        compiler_params=_cparams(("arbitrary", "arbitrary")),
        name="attn_prep",
    )(pa, cos, slo, shi, qg, kg)


def _rope_tables(lc, l):
    rows = l // GRID_W
    row = jnp.repeat(jnp.arange(rows, dtype=F32), GRID_W)
    col = jnp.tile(jnp.arange(GRID_W, dtype=F32), rows)
    inv = ROPE_THETA ** (-jnp.arange(0, ROPE_HALF, 2, dtype=F32) / ROPE_HALF)
    ang = jnp.concatenate([row[:, None] * inv, col[:, None] * inv], axis=-1)
    cos = jnp.concatenate([jnp.ones((lc, ROPE_HALF), F32), jnp.cos(ang)], axis=0)
    sin = jnp.concatenate([jnp.zeros((lc, ROPE_HALF), F32), jnp.sin(ang)], axis=0)
    zero = jnp.zeros_like(sin)
    head = lambda lo, hi: jnp.tile(jnp.concatenate([lo, hi], axis=-1), (1, ATTN_HEADS))
    return head(cos, cos), head(-sin, zero), head(zero, sin)


def _attn_kernel(q_ref, k_ref, v_ref, o_ref, s_ref, *, nct, lc):
    t = pl.program_id(2)
    n_all = k_ref.shape[2]
    low = lax.broadcasted_iota(jnp.int32, (TR, LANES), 1) < HEAD_DIM

    def run(n_keys):
        tiles = [slice(j * ATTN_KT, (j + 1) * ATTN_KT) for j in range(n_keys // ATTN_KT)]
        mx, acc = [None] * ATTN_GROUP, [None] * ATTN_GROUP
        m = [None] * ATTN_GROUP
        for stage in range(ATTN_GROUP + 1):
            g1, g2 = stage, stage - 1
            if g1 < ATTN_GROUP:
                mx[g1] = jnp.full((TR, LANES), -jnp.inf, F32)
            if g2 >= 0:
                m[g2] = jnp.max(mx[g2], axis=-1, keepdims=True)
                acc[g2] = jnp.zeros((TR, LANES), F32)
            for ks in tiles:
                if g1 < ATTN_GROUP:
                    s = _dot_nt(q_ref[0, g1], k_ref[0, 0, ks, :])
                    s_ref[g1, :, ks] = s
                    for c in range(ATTN_KT // LANES):
                        mx[g1] = jnp.maximum(mx[g1], s[:, c * LANES:(c + 1) * LANES])
                if g2 >= 0:
                    p = jnp.exp2(s_ref[g2, :, ks] - m[g2]).astype(BF16)
                    acc[g2] = acc[g2] + _dot(p, v_ref[0, 0, ks, :])
        out = [a / a[:, HEAD_DIM:HEAD_DIM + 1] for a in acc]
        for pr in range(ATTN_GROUP // PAIR):
            hi = pltpu.roll(out[PAIR * pr + 1], HEAD_DIM, 1)
            o_ref[0, :, pr * LANES:(pr + 1) * LANES] = jnp.where(low, out[PAIR * pr], hi).astype(BF16)

    @pl.when(t < nct)
    def _():
        run(lc)

    @pl.when(t >= nct)
    def _():
        run(n_all)


def _attention(q, k, v, nct, lc):
    nb, _, s, _ = q.shape
    gw = ATTN_GROUP * HEAD_DIM
    kv_spec = pl.BlockSpec((1, 1, s, LANES), lambda b, j, t: (b, j, 0, 0))
    return pl.pallas_call(
        functools.partial(_attn_kernel, nct=nct, lc=lc),
        grid=(nb, ATTN_KV_HEADS, s // TR),
        in_specs=[pl.BlockSpec((1, ATTN_GROUP, TR, LANES), lambda b, j, t: (b, j, t, 0)), kv_spec, kv_spec],
        out_specs=pl.BlockSpec((1, TR, gw), lambda b, j, t: (b, t, j)),
        out_shape=jax.ShapeDtypeStruct((nb, s, ATTN_WIDTH), BF16),
        scratch_shapes=[pltpu.VMEM((ATTN_GROUP, TR, s), F32)],
        compiler_params=_cparams(("arbitrary", "arbitrary", "arbitrary")),
        name="attention",
    )(q, k, v)


def _token_shift(p, prev_row, next_row, mu):
    n = p.shape[0]
    row = lax.broadcasted_iota(jnp.int32, p.shape, 0)
    prev = jnp.where(row == 0, prev_row, pltpu.roll(p, 1, 0))
    nxt = jnp.where(row == n - 1, next_row, pltpu.roll(p, n - 1, 0))
    return p + mu * (0.5 * (prev + nxt) - p)


def _halo_rows(prev_ref, next_ref, tile, nct, ng):
    first = jnp.logical_or(tile == 0, tile == nct)
    last = jnp.logical_or(tile == nct - 1, tile == ng - 1)
    prev_row = jnp.where(first, 0.0, prev_ref[0, SUBLANES - 1:SUBLANES, :])
    next_row = jnp.where(last, 0.0, next_ref[0, 0:1, :])
    return prev_row, next_row


def _rwkv_tile(rev, g, nct, ng):
    if not rev:
        return g
    return jnp.where(g < nct, nct - 1 - g, ng - 1 - (g - nct))


def _block_diag(x):
    z = jnp.zeros((CH, CH), x.dtype)
    return jnp.concatenate([jnp.concatenate([x[:, :CH], z], axis=1), jnp.concatenate([z, x[:, CH:]], axis=1)], axis=0)


def _head_rows(x):
    h0 = (lax.broadcasted_iota(jnp.int32, x.shape, 1) & (LANES - 1)) < HEAD_DIM
    return jnp.concatenate([jnp.where(h0, x, 0.0), jnp.where(h0, 0.0, x)], axis=0).astype(BF16)


def _rwkv_dir_kernel(p_ref, prev_ref, next_ref, mu_ref, kkw_ref, ka_ref, rk_ref, w0_ref, a0_ref, w2_ref, a2_ref,
                     o_ref, bv_ref, s_ref, *, rev, nct, ng):
    g = pl.program_id(1)
    tile = _rwkv_tile(rev, g, nct, ng)
    w = RWKV_WIDTH

    @pl.when(g == 0)
    def _():
        s_ref[...] = jnp.zeros_like(s_ref)

    prev_row, next_row = _halo_rows(prev_ref, next_ref, tile, nct, ng)
    ps = _token_shift(p_ref[0], prev_row, next_row, mu_ref[...])
    r, k, v = ps[:, :w], ps[:, w:2 * w], ps[:, 2 * w:3 * w]
    wd = ps[:, 3 * w:3 * w + 2 * DECAY_LORA]
    ad = ps[:, 3 * w + 2 * DECAY_LORA:3 * w + 2 * DECAY_LORA + 2 * ICLR_LORA]
    kk = k * kkw_ref[...]
    kk = kk * lax.rsqrt(_head_sums(kk * kk) + 1e-12)
    x = w0_ref[...] + _dot3(jnp.tanh(wd), w2_ref[...])
    lw = (-math.exp(-0.5)) * _sigmoid(x)
    a = _sigmoid(a0_ref[...] + _dot(ad.astype(BF16), a2_ref[...].astype(BF16)))
    k_d = k * (1.0 + (a - 1.0) * ka_ref[...])
    bv_ref[0] = (_head_sums(r * k_d * rk_ref[...]) * v).astype(bv_ref.dtype)
    beta = kk * a

    ti = lax.broadcasted_iota(jnp.int32, (CH, PAIR * CH), 0)
    tj = lax.broadcasted_iota(jnp.int32, (CH, PAIR * CH), 1) & (CH - 1)
    prec2 = (tj > ti) if rev else (tj < ti)
    eye2 = ti == tj
    incl2 = jnp.logical_or(prec2, eye2)
    late, early = (tj, ti) if rev else (ti, tj)
    join = [jnp.logical_and(jnp.logical_and((ti ^ tj) < 2 * b, (late & b) != 0), (early & b) == 0)
            for b in (2 ** m for m in range(int(math.log2(CH))))]
    si = lax.broadcasted_iota(jnp.int32, (CH, CH), 0)
    sj = lax.broadcasted_iota(jnp.int32, (CH, CH), 1)
    eye = si == sj
    tri = jnp.where((sj >= si) if rev else (sj <= si), 1.0, 0.0).astype(BF16)
    same_head = (si // HEAD_DIM) == (sj // HEAD_DIM)
    n_chunks = TR // CH
    n_pairs = RWKV_HEADS // PAIR
    chunk_order = list(range(n_chunks - 1, -1, -1) if rev else range(n_chunks))
    units = [(ci, pp) for ci in chunk_order for pp in range(n_pairs)]
    each = lambda f, *lists: [f(*args) for args in zip(*lists)]

    cum = {}
    for ci in chunk_order:
        cs = _dot(tri, jnp.concatenate(_split3(lw[ci * CH:(ci + 1) * CH]), axis=1))
        cum[ci] = cs[:, :w] + cs[:, w:2 * w] + cs[:, 2 * w:]
    blk = lambda x: [x[ci * CH:(ci + 1) * CH, pp * LANES:(pp + 1) * LANES] for ci, pp in units]
    rx, vx, kap, lwx, kx, bet = blk(r), blk(v), blk(kk), blk(lw), blk(k_d), blk(beta)
    cumi = [cum[ci][:, pp * LANES:(pp + 1) * LANES] for ci, pp in units]
    tot = each(lambda c: c[0:1, :] if rev else c[CH - 1:CH, :], cumi)
    cc = each(lambda c: c - c[CH // 2:CH // 2 + 1, :], cumi)
    e_neg = each(lambda c: jnp.exp(-c), cc)
    e_end = each(lambda t, c: jnp.exp(t - c), tot, cumi)
    kap_t = each(lambda x, c, l: x * jnp.exp(c - l), kap, cc, lwx)
    r_t = each(lambda x, c: x * jnp.exp(c), rx, cc)
    kap_0 = each(lambda x, c, l: x * jnp.exp(c - l), kap, cumi, lwx)
    r_0 = each(lambda x, c: x * jnp.exp(c), rx, cumi)
    k_t = each(lambda x, e: x * e, kx, e_neg)
    bet_t = each(lambda x, e: x * e, bet, e_neg)
    k_e = each(lambda x, e: x * e, kx, e_end)
    bet_e = each(lambda x, e: x * e, bet, e_end)
    p_end = each(jnp.exp, tot)
    qs = each(lambda a, b: jnp.concatenate([a, b], axis=0).astype(BF16), kap_t, r_t)
    s_b = each(lambda q, x: _dot_nt(q, _head_rows(x)), qs, bet_t)
    s_k = each(lambda q, x: _dot_nt(q, _head_rows(x)), qs, k_t)
    a_b = each(lambda s: jnp.where(prec2, s[:CH], 0.0), s_b)
    a_k = each(lambda s: jnp.where(prec2, s[:CH], 0.0).astype(BF16), s_k)
    b_b = each(lambda s: jnp.where(incl2, s[CH:], 0.0).astype(BF16), s_b)
    b_k = each(lambda s: jnp.where(incl2, s[CH:], 0.0).astype(BF16), s_k)
    ts = each(lambda a: jnp.where(eye2, 1.0, jnp.where(join[0], -a, 0.0)), a_b)
    for lvl in range(1, len(join)):
        q_b = each(lambda a: _block_diag(jnp.where(join[lvl], a, 0.0).astype(BF16)), a_b)
        t_b = each(lambda t: t.astype(BF16), ts)
        dq = each(lambda t, q: _dot(t, q).astype(BF16), t_b, q_b)
        ts = each(lambda t, e, tb: t - _dot(e, _block_diag(tb)), ts, dq, t_b)
    t_b = each(lambda t: t.astype(BF16), ts)
    v_h = each(_head_rows, vx)
    av = each(_dot, a_k, v_h)
    rhs = each(lambda k0, a: _head_rows(jnp.concatenate([k0, a], axis=1)), kap_0, av)
    wz = each(_dot, t_b, rhs)
    kq = each(lambda x: x[:, :LANES], wz)
    z0 = each(lambda x: x[:, LANES:], wz)
    bz = each(lambda b, x: _dot(b, _head_rows(x)), b_b, wz)
    o0k = each(_dot, b_k, v_h)
    r_new = each(lambda x, b: x - b[:, :LANES], r_0, bz)
    o0 = each(lambda a, b: a - b[:, LANES:], o0k, bz)
    m_c = each(lambda p, b, q: jnp.where(eye, p, 0.0) - jnp.where(same_head, _dot_tn(b, q), 0.0), p_end, bet_e, kq)
    g_c = each(lambda ke, be, x, z: jnp.where(same_head, _dot_tn(jnp.concatenate([ke, be], axis=0),
                                                                  jnp.concatenate([x, -z], axis=0)), 0.0),
               k_e, bet_e, vx, z0)

    states = [s_ref[pp] for pp in range(n_pairs)]
    outs = {}
    lhs = each(lambda a, b: jnp.concatenate([a, b], axis=0).astype(BF16), r_new, m_c)
    for u, (ci, pp) in enumerate(units):
        st_hi, st_lo = _split2(states[pp])
        res = _dot(lhs[u], st_hi) + _dot(lhs[u], st_lo)
        outs[ci, pp] = res[:CH] + o0[u]
        states[pp] = res[CH:] + g_c[u]
    for ci in chunk_order:
        o_ref[0, ci * CH:(ci + 1) * CH, :] = jnp.concatenate([outs[ci, pp] for pp in range(n_pairs)],
                                                             axis=1).astype(o_ref.dtype)
    for pp in range(n_pairs):
        s_ref[pp] = states[pp]


def _rwkv_scan(pr, lp, nct, rev):
    nb, s, pw = pr.shape
    ng = s // TR
    w = RWKV_WIDTH
    hb = TR // SUBLANES
    d = int(rev)
    tile = functools.partial(_rwkv_tile, rev, nct=nct, ng=ng)
    n_row_blocks = s // SUBLANES
    main = pl.BlockSpec((1, TR, pw), lambda b, g: (b, tile(g), 0))
    prev = pl.BlockSpec((1, SUBLANES, pw), lambda b, g: (b, jnp.maximum(tile(g) * hb - 1, 0), 0))
    nxt = pl.BlockSpec((1, SUBLANES, pw), lambda b, g: (b, jnp.minimum((tile(g) + 1) * hb, n_row_blocks - 1), 0))
    vec = lambda n: pl.BlockSpec((1, n), lambda b, g: (0, 0))
    mat = pl.BlockSpec((2 * DECAY_LORA, w), lambda b, g: (0, 0))
    out = pl.BlockSpec((1, TR, w), lambda b, g: (b, tile(g), 0))
    zeros = jnp.zeros((DECAY_LORA, w), F32)
    pad = lambda m: jnp.concatenate([zeros, m] if rev else [m, zeros])
    return pl.pallas_call(
        functools.partial(_rwkv_dir_kernel, rev=rev, nct=nct, ng=ng),
        grid=(nb, ng),
        in_specs=[main, prev, nxt, vec(pw), vec(w), vec(w), vec(w), vec(w), vec(w), mat, mat],
        out_specs=[out, out],
        out_shape=[jax.ShapeDtypeStruct((nb, s, w), BF16), jax.ShapeDtypeStruct((nb, s, w), BF16)],
        scratch_shapes=[pltpu.VMEM((RWKV_HEADS // PAIR, LANES, LANES), F32)],
        compiler_params=_cparams(("arbitrary", "arbitrary")),
        name="rwkv_scan_bwd" if rev else "rwkv_scan_fwd",
    )(pr, pr, pr, lp['mu'].reshape(1, pw), lp['kk'].reshape(1, w), lp['ka'].reshape(1, w), lp['rk'].reshape(1, w),
      lp['w0'][d].reshape(1, w), lp['a0'][d].reshape(1, w), pad(lp['w2'][d]), pad(lp['a2'][d]))


def _rwkv_out_kernel(of_ref, ob_ref, bf_ref, bb_ref, gd_ref, gprev_ref, gnext_ref, mu_ref, g2_ref, gw_ref, gb_ref,
                     y_ref, *, nct, ng):
    tile = pl.program_id(1)
    o = of_ref[0].astype(F32) + ob_ref[0].astype(F32)
    cen = o - _head_sums(o, 1.0 / HEAD_DIM)
    var = _head_sums(cen * cen, 1.0 / HEAD_DIM)
    y = cen * lax.rsqrt(var + GN_EPS) * gw_ref[...] + gb_ref[...] + (bf_ref[0].astype(F32) + bb_ref[0].astype(F32))
    prev_row, next_row = _halo_rows(gprev_ref, gnext_ref, tile, nct, ng)
    gd = _token_shift(gd_ref[0], prev_row, next_row, mu_ref[...])
    gate = _dot(_sigmoid(gd).astype(BF16), g2_ref[...].astype(BF16))
    y_ref[0] = (y * gate).astype(BF16)


def _rwkv_out(o_f, o_b, bv_f, bv_b, pr, lp, nct):
    nb, s, w = o_f.shape
    ng = s // TR
    hb = TR // SUBLANES
    gcol = (SHIFT_WIDTH - GATE_LORA) // GATE_LORA
    n_row_blocks = s // SUBLANES
    row = pl.BlockSpec((1, TR, w), lambda b, t: (b, t, 0))
    vec = lambda n: pl.BlockSpec((1, n), lambda b, t: (0, 0))
    return pl.pallas_call(
        functools.partial(_rwkv_out_kernel, nct=nct, ng=ng),
        grid=(nb, ng),
        in_specs=[row, row, row, row,
                  pl.BlockSpec((1, TR, GATE_LORA), lambda b, t: (b, t, gcol)),
                  pl.BlockSpec((1, SUBLANES, GATE_LORA), lambda b, t: (b, jnp.maximum(t * hb - 1, 0), gcol)),
                  pl.BlockSpec((1, SUBLANES, GATE_LORA),
                               lambda b, t: (b, jnp.minimum((t + 1) * hb, n_row_blocks - 1), gcol)),
                  vec(GATE_LORA), pl.BlockSpec((GATE_LORA, w), lambda b, t: (0, 0)), vec(w), vec(w)],
        out_specs=row,
        out_shape=jax.ShapeDtypeStruct((nb, s, w), BF16),
        compiler_params=_cparams(("arbitrary", "arbitrary")),
        name="rwkv_out",
    )(o_f, o_b, bv_f, bv_b, pr, pr, pr, lp['mu'][SHIFT_WIDTH - GATE_LORA:].reshape(1, GATE_LORA), lp['g2'],
      lp['gn_w'].reshape(1, w), lp['gn_b'].reshape(1, w))


def _top2(logits):
    lane = lax.broadcasted_iota(jnp.int32, logits.shape, 1)
    v1 = jnp.max(logits, axis=-1, keepdims=True)
    i1 = jnp.min(jnp.where(logits == v1, lane, LANES), axis=-1, keepdims=True)
    rest = jnp.where(lane == i1, -jnp.inf, logits)
    v2 = jnp.max(rest, axis=-1, keepdims=True)
    i2 = jnp.min(jnp.where(rest == v2, lane, LANES), axis=-1, keepdims=True)
    e = jnp.exp(v2 - v1)
    g1 = 1.0 / (1.0 + e)
    g2 = e / (1.0 + e)
    out = jnp.where(lane == 0, i1.astype(F32), 0.0)
    out = jnp.where(lane == 1, i2.astype(F32), out)
    out = jnp.where(lane == 2, g1, out)
    return jnp.where(lane == 3, g2, out)


def _out_proj_kernel(x_ref, att_ref, rw_ref, mod_ref, g_ref, wa_ref, wr_ref, *rest, moe):
    if moe:
        wrt_ref, xo_ref, h_ref, route_ref = rest
    else:
        xo_ref, h_ref = rest
    mix = _dot(att_ref[0], wa_ref[...]) + _dot(rw_ref[0], wr_ref[...])
    x = x_ref[0] + mod_ref[0, 2:3] * mix
    xo_ref[0] = x
    h = _norm_mod(x, g_ref[...], mod_ref[0, 3:4], mod_ref[0, 4:5])
    h_ref[0] = h.astype(h_ref.dtype)
    if moe:
        lane = lax.broadcasted_iota(jnp.int32, (TR, LANES), 1)
        logits = jnp.where(lane < N_EXPERTS, _dot3(h, wrt_ref[...]), -jnp.inf)
        route_ref[0] = _top2(logits)


def _out_proj(xa, att, rw, mod, g, wo_a, wo_r, nct, w_router=None):
    nb, s, d = xa.shape
    moe = w_router is not None
    row = lambda n: pl.BlockSpec((1, TR, n), lambda b, t: (b, t, 0))
    full = lambda a: pl.BlockSpec(a.shape, lambda b, t: (0, 0))
    in_specs = [row(d), row(ATTN_WIDTH), row(RWKV_WIDTH), pl.BlockSpec((1, N_MOD, d), _mod_index(nb, nct)),
                pl.BlockSpec((1, d), lambda b, t: (0, 0)), full(wo_a), full(wo_r)]
    args = [xa, att, rw, mod, g.reshape(1, d), wo_a, wo_r]
    out_specs = [row(d), row(d)]
    out_shape = [jax.ShapeDtypeStruct((nb, s, d), F32), jax.ShapeDtypeStruct((nb, s, d), F32 if moe else BF16)]
    if moe:
        wrt = jnp.pad(w_router, ((0, 0), (0, LANES - N_EXPERTS)))
        in_specs.append(full(wrt))
        args.append(wrt)
        out_specs.append(row(LANES))
        out_shape.append(jax.ShapeDtypeStruct((nb, s, LANES), F32))
    return pl.pallas_call(
        functools.partial(_out_proj_kernel, moe=moe),
        grid=(nb, s // TR),
        in_specs=in_specs, out_specs=out_specs, out_shape=out_shape,
        compiler_params=_cparams(("arbitrary", "arbitrary")),
        name="out_proj_moe" if moe else "out_proj",
    )(*args)


def _ffn_kernel(x_ref, h_ref, mod_ref, wg_ref, wu_ref, wd_ref, o_ref):
    h = h_ref[0]
    gt = _dot(h, wg_ref[...])
    up = _dot(h, wu_ref[...])
    act = (gt * _sigmoid(gt) * up).astype(BF16)
    o_ref[0] = x_ref[0] + mod_ref[0, 5:6] * _dot(act, wd_ref[...])


def _ffn_dense(xa, h, mod, wg, wu, wd, nct):
    nb, s, d = xa.shape
    row = pl.BlockSpec((1, TR, d), lambda b, t: (b, t, 0))
    resident = lambda a: pl.BlockSpec(a.shape, lambda b, t: (0, 0), pipeline_mode=pl.Buffered(1))
    return pl.pallas_call(
        _ffn_kernel,
        grid=(nb, s // TR),
        in_specs=[row, row, pl.BlockSpec((1, N_MOD, d), _mod_index(nb, nct)), resident(wg), resident(wu),
                  resident(wd)],
        out_specs=row,
        out_shape=jax.ShapeDtypeStruct((nb, s, d), F32),
        compiler_params=_cparams(("arbitrary", "arbitrary")),
        name="ffn_dense",
    )(xa, h, mod, wg, wu, wd)


def _row_copy(src_hbm, idx_ref, dst_ref, sem, r):
    return pltpu.make_async_copy(src_hbm.at[pl.ds(idx_ref[0, 0, r], 1), :], dst_ref.at[pl.ds(r, 1), :], sem)


def _moe_kernel(be_ref, nu_ref, tok_ref, tok_next_ref, h_hbm, wg_ref, wu_ref, wd_ref, o_ref, x_buf, sem):
    i = pl.program_id(0)
    j = pl.program_id(1)
    nblk = pl.num_programs(0)
    last_j = pl.num_programs(1) - 1
    used = i < nu_ref[0]
    slot = i % 2
    cur, nxt = x_buf.at[slot], x_buf.at[1 - slot]
    rows = cur.shape[0]

    def wait_rows(idx_ref, dst, s):
        def body(r, carry):
            _row_copy(h_hbm, idx_ref, dst, s, r).wait()
            return carry
        lax.fori_loop(0, rows, body, 0, unroll=8)

    @pl.when(j == 0)
    def _():
        o_ref[...] = jnp.zeros_like(o_ref)

        @pl.when(i == 0)
        def _():
            def body(r, carry):
                _row_copy(h_hbm, tok_ref, cur, sem.at[slot], r).start()
                return carry
            lax.fori_loop(0, rows, body, 0, unroll=8)

        @pl.when(jnp.logical_or(i == 0, i - 1 < nu_ref[0]))
        def _():
            wait_rows(tok_ref, cur, sem.at[slot])

    def compute():
        x = cur[...].astype(BF16)
        gt = _dot(x, wg_ref[0])
        up = _dot(x, wu_ref[0])
        act = (gt * _sigmoid(gt) * up).astype(BF16)
        o_ref[...] += _dot(act, wd_ref[0])

    @pl.when(jnp.logical_and(used, j != last_j))
    def _():
        compute()

    @pl.when(jnp.logical_and(used, j == last_j))
    def _():
        compute()
        for r in range(rows):
            _row_copy(h_hbm, tok_next_ref, nxt, sem.at[1 - slot], r).start()

        @pl.when(i == nblk - 1)
        def _():
            wait_rows(tok_next_ref, nxt, sem.at[1 - slot])


def _moe_experts(h, slot_token, block_expert, n_used, wg, wu, wd):
    t, d = h.shape
    dff = wg.shape[2]
    nblk = slot_token.shape[0] // MOE_BM
    tok = slot_token.reshape(nblk, 1, MOE_BM)
    grid_spec = pltpu.PrefetchScalarGridSpec(
        num_scalar_prefetch=2,
        grid=(nblk, dff // MOE_TF),
        in_specs=[pl.BlockSpec((1, 1, MOE_BM), lambda i, j, be, nu: (i, 0, 0), memory_space=pltpu.SMEM),
                  pl.BlockSpec((1, 1, MOE_BM), lambda i, j, be, nu: (jnp.minimum(i + 1, nblk - 1), 0, 0),
                               memory_space=pltpu.SMEM),
                  pl.BlockSpec(memory_space=pl.ANY),
                  pl.BlockSpec((1, d, MOE_TF), lambda i, j, be, nu: (be[i], 0, j)),
                  pl.BlockSpec((1, d, MOE_TF), lambda i, j, be, nu: (be[i], 0, j)),
                  pl.BlockSpec((1, MOE_TF, d), lambda i, j, be, nu: (be[i], j, 0))],
        out_specs=pl.BlockSpec((MOE_BM, d), lambda i, j, be, nu: (i, 0)),
        scratch_shapes=[pltpu.VMEM((2, MOE_BM, d), F32), pltpu.SemaphoreType.DMA((2,))],
    )
    return pl.pallas_call(
        _moe_kernel,
        grid_spec=grid_spec,
        out_shape=jax.ShapeDtypeStruct((nblk * MOE_BM, d), F32),
        compiler_params=_cparams(("arbitrary", "arbitrary")),
        name="moe_experts",
    )(block_expert, n_used, tok, tok, h, wg, wu, wd)


def _gather_rows(src_hbm, idx_ref, dst_ref, sem):
    n = dst_ref.shape[0]

    def start(r, carry):
        _row_copy(src_hbm, idx_ref, dst_ref, sem, r).start()
        return carry

    def wait(r, carry):
        _row_copy(src_hbm, idx_ref, dst_ref, sem, r).wait()
        return carry

    lax.fori_loop(0, n, start, 0, unroll=8)
    lax.fori_loop(0, n, wait, 0, unroll=8)


def _combine_kernel(s1_ref, s2_ref, x_ref, route_ref, mod_ref, ys_hbm, o_ref, y1_ref, y2_ref, sem):
    _gather_rows(ys_hbm, s1_ref, y1_ref, sem.at[0])
    _gather_rows(ys_hbm, s2_ref, y2_ref, sem.at[1])
    route = route_ref[0]
    f = y1_ref[...] * route[:, 2:3] + y2_ref[...] * route[:, 3:4]
    o_ref[0] = x_ref[0] + mod_ref[0, 5:6] * f


def _moe_combine(xa, route, mod, ys, slot1, slot2, nct):
    nb, s, d = xa.shape
    ng = s // TR
    idx = pl.BlockSpec((1, 1, TR), lambda b, t: (b * ng + t, 0, 0), memory_space=pltpu.SMEM)
    row = lambda n: pl.BlockSpec((1, TR, n), lambda b, t: (b, t, 0))
    return pl.pallas_call(
        _combine_kernel,
        grid=(nb, ng),
        in_specs=[idx, idx, row(d), row(LANES), pl.BlockSpec((1, N_MOD, d), _mod_index(nb, nct)),
                  pl.BlockSpec(memory_space=pl.ANY)],
        out_specs=row(d),
        out_shape=jax.ShapeDtypeStruct((nb, s, d), F32),
        scratch_shapes=[pltpu.VMEM((TR, d), F32), pltpu.VMEM((TR, d), F32), pltpu.SemaphoreType.DMA((2,))],
        compiler_params=_cparams(("arbitrary", "arbitrary")),
        name="moe_combine",
    )(slot1.reshape(nb * ng, 1, TR), slot2.reshape(nb * ng, 1, TR), xa, route, mod, ys)


def _moe_ffn(xa, h, route, mod, wg, wu, wd, nct):
    nb, s, d = xa.shape
    t = nb * s
    n_assign = 2 * t
    nblk = (n_assign + N_EXPERTS * (MOE_BM - 1) + MOE_BM - 1) // MOE_BM
    n_slots = nblk * MOE_BM
    expert = route.reshape(t, LANES)[:, 0:2].astype(jnp.int32).reshape(-1)
    onehot = (expert[:, None] == jnp.arange(N_EXPERTS, dtype=jnp.int32)[None, :]).astype(jnp.int32)
    rank = jnp.sum((jnp.cumsum(onehot, axis=0) - onehot) * onehot, axis=1)
    counts = jnp.sum(onehot, axis=0)
    padded = (counts + MOE_BM - 1) // MOE_BM * MOE_BM
    pad_end = jnp.cumsum(padded)
    pad_start = pad_end - padded
    slot = pad_start[expert] + rank
    token = jnp.repeat(jnp.arange(t, dtype=jnp.int32), 2)
    slot_token = jnp.zeros((n_slots,), jnp.int32).at[slot].set(token)
    block_start = jnp.arange(nblk, dtype=jnp.int32) * MOE_BM
    block_expert = jnp.minimum(jnp.sum((pad_end[None, :] <= block_start[:, None]).astype(jnp.int32), axis=1),
                               N_EXPERTS - 1)
    n_used = (pad_end[-1] // MOE_BM).astype(jnp.int32).reshape(1)
    ys = _moe_experts(h.reshape(t, d), slot_token, block_expert, n_used, wg, wu, wd)
    slot2 = slot.reshape(t, 2)
    return _moe_combine(xa, route, mod, ys, slot2[:, 0], slot2[:, 1], nct)


def kernel(x, c, ctx, c_ctx, ada_w, ada_b, norm1_g, norm2_g, w_in, w_out, q_gain, k_gain, shift_mu, rw_w0, rw_w2,
           rw_a0, rw_a2, rw_g2, rw_kk, rw_ka, rw_rk, rw_gn_w, rw_gn_b, ffn_wg, ffn_wu, ffn_wd, moe_router, moe_wg,
           moe_wu, moe_wd):
    nb, l, d = x.shape
    lc = ctx.shape[1]
    depth = ada_w.shape[0]
    assert lc % TR == 0 and l % TR == 0 and l % GRID_W == 0 and TR % ATTN_KT == 0 and TR % CH == 0
    nct = lc // TR
    xa = jnp.concatenate([ctx, x], axis=1)

    mod_rows = -(-(nb + 1) // SUBLANES) * SUBLANES
    c_all = jnp.concatenate([c, c_ctx[None], jnp.zeros((mod_rows - nb - 1, d), F32)], axis=0)
    mod_all = _ada_all(c_all, ada_w, ada_b)[:, :nb + 1].reshape(depth, nb + 1, N_MOD, d)
    cos, slo, shi = _rope_tables(lc, l)

    for i in range(depth):
        mod = mod_all[i]
        lp = dict(mu=shift_mu[i], w0=rw_w0[i], w2=rw_w2[i], a0=rw_a0[i], a2=rw_a2[i], g2=rw_g2[i], kk=rw_kk[i],
                  ka=rw_ka[i], rk=rw_rk[i], gn_w=rw_gn_w[i], gn_b=rw_gn_b[i])
        wi = w_in[i].astype(BF16)
        pa, pr = _in_proj(xa, mod, norm1_g[i], wi[:, :ATTN_IN], wi[:, ATTN_IN:], nct)
        q, k, v = _attn_prep(pa, cos, slo, shi, q_gain[i], k_gain[i])
        att = _attention(q, k, v, nct, lc)
        o_f, bv_f = _rwkv_scan(pr, lp, nct, False)
        o_b, bv_b = _rwkv_scan(pr, lp, nct, True)
        rw = _rwkv_out(o_f, o_b, bv_f, bv_b, pr, lp, nct)
        wo = w_out[i].astype(BF16)
        j = i // 2
        if i % 2 == 0:
            xa, h = _out_proj(xa, att, rw, mod, norm2_g[i], wo[:ATTN_WIDTH], wo[ATTN_WIDTH:], nct)
            xa = _ffn_dense(xa, h, mod, ffn_wg[j].astype(BF16), ffn_wu[j].astype(BF16), ffn_wd[j].astype(BF16), nct)
        else:
            xa, h, route = _out_proj(xa, att, rw, mod, norm2_g[i], wo[:ATTN_WIDTH], wo[ATTN_WIDTH:], nct,
                                     moe_router[j])
            xa = _moe_ffn(xa, h, route, mod, moe_wg[j].astype(BF16), moe_wu[j].astype(BF16), moe_wd[j].astype(BF16),
                          nct)
    return xa[:, lc:]
```

```python
import functools
import math

import jax
import jax.numpy as jnp
from jax import lax
from jax.experimental import pallas as pl
from jax.experimental.pallas import tpu as pltpu

F32 = jnp.float32
BF16 = jnp.bfloat16
HI = lax.Precision.HIGHEST

HEAD_DIM = 64
ROPE_HALF = HEAD_DIM // 2
ROPE_THETA = 10000.0
GRID_W = 64
ATTN_HEADS = 8
ATTN_KV_HEADS = 2
ATTN_GROUP = ATTN_HEADS // ATTN_KV_HEADS
ATTN_WIDTH = ATTN_HEADS * HEAD_DIM
KV_WIDTH = ATTN_KV_HEADS * HEAD_DIM
ATTN_IN = ATTN_WIDTH + 2 * KV_WIDTH
ATTN_SCALE = HEAD_DIM ** -0.5
LOG2E = math.log2(math.e)
RWKV_HEADS = 8
RWKV_WIDTH = RWKV_HEADS * HEAD_DIM
DECAY_LORA = 64
ICLR_LORA = 64
GATE_LORA = 128
SHIFT_WIDTH = 3 * RWKV_WIDTH + 2 * DECAY_LORA + 2 * ICLR_LORA + GATE_LORA
N_EXPERTS = 8
N_MOD = 6
EPS = 1e-6
GN_EPS = 64e-5

LANES = 128
SUBLANES = 8
TR = 256
CH = 128
PAIR = LANES // HEAD_DIM
MXU_DIM = 256
ATTN_KT = 256
MOE_BM = 512
MOE_TF = 1792
VMEM_LIMIT = 56 * 1024 * 1024


def _cparams(sem):
    return pltpu.CompilerParams(dimension_semantics=sem, vmem_limit_bytes=VMEM_LIMIT)


def _dot(a, b, prec=None):
    return jnp.dot(a, b, preferred_element_type=F32, precision=prec)


def _dot_nt(a, b, prec=None):
    return lax.dot_general(a, b, (((1,), (1,)), ((), ())), preferred_element_type=F32, precision=prec)


def _dot_tn(a, b, prec=None):
    return lax.dot_general(a, b, (((0,), (0,)), ((), ())), preferred_element_type=F32, precision=prec)


def _sigmoid(x):
    return 1.0 / (1.0 + jnp.exp(-x))


def _split2(x):
    hi = x.astype(BF16)
    return hi, (x - hi.astype(F32)).astype(BF16)


def _split3(x):
    hi = x.astype(BF16)
    r = x - hi.astype(F32)
    mid = r.astype(BF16)
    return hi, mid, (r - mid.astype(F32)).astype(BF16)


def _dot3(a, b):
    a_hi, a_lo = _split2(a)
    b_hi, b_lo = _split2(b)
    return _dot(a_hi, b_hi) + (_dot(a_hi, b_lo) + _dot(a_lo, b_hi))


def _head_sums(x, scale=1.0):
    r, n = x.shape
    i = lax.broadcasted_iota(jnp.int32, (MXU_DIM, MXU_DIM), 0) // HEAD_DIM
    j = lax.broadcasted_iota(jnp.int32, (MXU_DIM, MXU_DIM), 1) // HEAD_DIM
    ones = jnp.where(i == j, scale, 0.0).astype(BF16)
    nfull = n // MXU_DIM
    parts = []
    if nfull:
        xs = jnp.concatenate([x[:, c * MXU_DIM:(c + 1) * MXU_DIM] for c in range(nfull)], axis=0)
        m = nfull * r
        s = _dot(jnp.concatenate(_split2(xs), axis=0), ones)
        s = s[:m] + s[m:]
        parts += [s[c * r:(c + 1) * r] for c in range(nfull)]
    if n % MXU_DIM:
        w = n % MXU_DIM
        s = _dot(jnp.concatenate(_split2(x[:, nfull * MXU_DIM:]), axis=0), ones[:w, :w])
        parts.append(s[:r] + s[r:])
    return jnp.concatenate(parts, axis=1)


def _ada_kernel(c_ref, w_ref, b_ref, o_ref):
    c = c_ref[...]
    sc = c * _sigmoid(c)
    o_ref[0] = _dot(sc, w_ref[0], HI) + b_ref[0]


def _ada_all(c_all, ada_w, ada_b):
    depth, d, n = ada_w.shape
    rows = c_all.shape[0]
    tn = 1536
    return pl.pallas_call(
        _ada_kernel,
        grid=(depth, n // tn),
        in_specs=[pl.BlockSpec((rows, d), lambda i, j: (0, 0)),
                  pl.BlockSpec((1, d, tn), lambda i, j: (i, 0, j)),
                  pl.BlockSpec((1, 1, tn), lambda i, j: (i, 0, j))],
        out_specs=pl.BlockSpec((1, rows, tn), lambda i, j: (i, 0, j)),
        out_shape=jax.ShapeDtypeStruct((depth, rows, n), F32),
        compiler_params=_cparams(("arbitrary", "arbitrary")),
        name="ada_mod",
    )(c_all, ada_w, ada_b.reshape(depth, 1, n))


def _norm_mod(x, g, shift, scale):
    y = x * lax.rsqrt(jnp.mean(x * x, axis=-1, keepdims=True) + EPS)
    return (y * g) * (1.0 + scale) + shift


def _in_proj_kernel(x_ref, mod_ref, g_ref, wa_ref, wr_ref, pa_ref, pr_ref):
    h = _norm_mod(x_ref[0], g_ref[...], mod_ref[0, 0:1], mod_ref[0, 1:2]).astype(BF16)
    pa_ref[0] = _dot(h, wa_ref[...])
    pr_ref[0] = _dot(h, wr_ref[...])


def _mod_index(nb, nct):
    return lambda b, t: (jnp.where(t < nct, nb, b), 0, 0)


def _in_proj(xa, mod, g, wa, wr, nct):
    nb, s, d = xa.shape
    na, nr = wa.shape[1], wr.shape[1]
    return pl.pallas_call(
        _in_proj_kernel,
        grid=(nb, s // TR),
        in_specs=[pl.BlockSpec((1, TR, d), lambda b, t: (b, t, 0)),
                  pl.BlockSpec((1, N_MOD, d), _mod_index(nb, nct)),
                  pl.BlockSpec((1, d), lambda b, t: (0, 0)),
                  pl.BlockSpec((d, na), lambda b, t: (0, 0)),
                  pl.BlockSpec((d, nr), lambda b, t: (0, 0))],
        out_specs=[pl.BlockSpec((1, TR, na), lambda b, t: (b, t, 0)),
                   pl.BlockSpec((1, TR, nr), lambda b, t: (b, t, 0))],
        out_shape=[jax.ShapeDtypeStruct((nb, s, na), F32),
                   jax.ShapeDtypeStruct((nb, s, nr), F32)],
        compiler_params=_cparams(("arbitrary", "arbitrary")),
        name="in_proj",
    )(xa, mod, g.reshape(1, d), wa, wr)


def _rope(x, cos, sin_lo, sin_hi):
    n = x.shape[-1]
    return x * cos + pltpu.roll(x, n - ROPE_HALF, 1) * sin_lo + pltpu.roll(x, ROPE_HALF, 1) * sin_hi


def _attn_prep_kernel(pa_ref, cos_ref, slo_ref, shi_ref, qg_ref, kg_ref, q_ref, k_ref, v_ref):
    pa = pa_ref[0]
    qk = pa[:, :ATTN_WIDTH + KV_WIDTH]
    v = pa[:, ATTN_WIDTH + KV_WIDTH:ATTN_IN]
    cos, slo, shi = cos_ref[...], slo_ref[...], shi_ref[...]
    inv = lax.rsqrt(_head_sums(qk * qk, 1.0 / HEAD_DIM) + EPS)
    qn = qk[:, :ATTN_WIDTH] * inv[:, :ATTN_WIDTH] * qg_ref[...]
    qr = _rope(qn, cos, slo, shi) * (ATTN_SCALE * LOG2E)
    kn = qk[:, ATTN_WIDTH:] * inv[:, ATTN_WIDTH:] * kg_ref[...]
    kr = _rope(kn, cos[:, :KV_WIDTH], slo[:, :KV_WIDTH], shi[:, :KV_WIDTH])
    lane = lax.broadcasted_iota(jnp.int32, (TR, LANES), 1)
    low = lane < HEAD_DIM
    for j in range(ATTN_KV_HEADS):
        kj = kr if j % PAIR == 0 else pltpu.roll(kr, HEAD_DIM, 1)
        vj = v if j % PAIR == 0 else pltpu.roll(v, HEAD_DIM, 1)
        k_ref[0, j] = jnp.where(low, kj, 0.0).astype(BF16)
        v_ref[0, j] = jnp.transpose(jnp.where(low, vj, jnp.where(lane == HEAD_DIM, 1.0, 0.0))).astype(BF16)
    for h in range(ATTN_HEADS):
        src = qr[:, (h // PAIR) * LANES:(h // PAIR + 1) * LANES]
        if h % PAIR:
            src = pltpu.roll(src, HEAD_DIM, 1)
        q_ref[0, h] = jnp.where(low, src, 0.0).astype(BF16)


def _attn_prep(pa, cos, slo, shi, q_gain, k_gain):
    nb, s, _ = pa.shape
    qg = jnp.tile(q_gain, ATTN_HEADS).reshape(1, ATTN_WIDTH)
    kg = jnp.tile(k_gain, ATTN_KV_HEADS).reshape(1, KV_WIDTH)
    tab = pl.BlockSpec((TR, ATTN_WIDTH), lambda t, b: (t, 0))
    return pl.pallas_call(
        _attn_prep_kernel,
        grid=(s // TR, nb),
        in_specs=[pl.BlockSpec((1, TR, ATTN_IN), lambda t, b: (b, t, 0)), tab, tab, tab,
                  pl.BlockSpec((1, ATTN_WIDTH), lambda t, b: (0, 0)),
                  pl.BlockSpec((1, KV_WIDTH), lambda t, b: (0, 0))],
        out_specs=[pl.BlockSpec((1, ATTN_HEADS, TR, LANES), lambda t, b: (b, 0, t, 0)),
                   pl.BlockSpec((1, ATTN_KV_HEADS, TR, LANES), lambda t, b: (b, 0, t, 0)),
                   pl.BlockSpec((1, ATTN_KV_HEADS, LANES, TR), lambda t, b: (b, 0, 0, t))],
        out_shape=[jax.ShapeDtypeStruct((nb, ATTN_HEADS, s, LANES), BF16),
                   jax.ShapeDtypeStruct((nb, ATTN_KV_HEADS, s, LANES), BF16),
                   jax.ShapeDtypeStruct((nb, ATTN_KV_HEADS, LANES, s), BF16)],
        compiler_params=_cparams(("arbitrary", "arbitrary")),
        name="attn_prep",
    )(pa, cos, slo, shi, qg, kg)


def _rope_tables(lc, l):
    rows = l // GRID_W
    row = jnp.repeat(jnp.arange(rows, dtype=F32), GRID_W)
    col = jnp.tile(jnp.arange(GRID_W, dtype=F32), rows)
    inv = ROPE_THETA ** (-jnp.arange(0, ROPE_HALF, 2, dtype=F32) / ROPE_HALF)
    ang = jnp.concatenate([row[:, None] * inv, col[:, None] * inv], axis=-1)
    cos = jnp.concatenate([jnp.ones((lc, ROPE_HALF), F32), jnp.cos(ang)], axis=0)
    sin = jnp.concatenate([jnp.zeros((lc, ROPE_HALF), F32), jnp.sin(ang)], axis=0)
    zero = jnp.zeros_like(sin)
    head = lambda lo, hi: jnp.tile(jnp.concatenate([lo, hi], axis=-1), (1, ATTN_HEADS))
    return head(cos, cos), head(-sin, zero), head(zero, sin)


def _attn_kernel(q_ref, k_ref, vt_ref, o_ref, s_ref, *, nct, lc):
    t = pl.program_id(2)
    n_all = k_ref.shape[2]
    low = lax.broadcasted_iota(jnp.int32, (TR, LANES), 1) < HEAD_DIM

    def run(n_keys):
        tiles = [slice(j * ATTN_KT, (j + 1) * ATTN_KT) for j in range(n_keys // ATTN_KT)]
        mx, acc = [None] * ATTN_GROUP, [None] * ATTN_GROUP
        m = [None] * ATTN_GROUP
        for stage in range(ATTN_GROUP + 1):
            g1, g2 = stage, stage - 1
            if g1 < ATTN_GROUP:
                mx[g1] = jnp.full((ATTN_KT, TR), -jnp.inf, F32)
            if g2 >= 0:
                m[g2] = jnp.max(mx[g2], axis=0, keepdims=True)
                acc[g2] = jnp.zeros((LANES, TR), F32)
            for ks in tiles:
                if g1 < ATTN_GROUP:
                    st = _dot_nt(k_ref[0, 0, ks, :], q_ref[0, g1])
                    s_ref[g1, ks, :] = st
                    mx[g1] = jnp.maximum(mx[g1], st)
                if g2 >= 0:
                    pt = jnp.exp2(s_ref[g2, ks, :] - m[g2]).astype(BF16)
                    acc[g2] = acc[g2] + _dot(vt_ref[0, 0, :, ks], pt)
        out = [jnp.transpose(a / a[HEAD_DIM:HEAD_DIM + 1, :]) for a in acc]
        for pr in range(ATTN_GROUP // PAIR):
            hi = pltpu.roll(out[PAIR * pr + 1], HEAD_DIM, 1)
            o_ref[0, :, pr * LANES:(pr + 1) * LANES] = jnp.where(low, out[PAIR * pr], hi).astype(BF16)

    @pl.when(t < nct)
    def _():
        run(lc)

    @pl.when(t >= nct)
    def _():
        run(n_all)


def _attention(q, k, v, nct, lc):
    nb, _, s, _ = q.shape
    gw = ATTN_GROUP * HEAD_DIM
    k_spec = pl.BlockSpec((1, 1, s, LANES), lambda b, j, t: (b, j, 0, 0))
    vt_spec = pl.BlockSpec((1, 1, LANES, s), lambda b, j, t: (b, j, 0, 0))
    return pl.pallas_call(
        functools.partial(_attn_kernel, nct=nct, lc=lc),
        grid=(nb, ATTN_KV_HEADS, s // TR),
        in_specs=[pl.BlockSpec((1, ATTN_GROUP, TR, LANES), lambda b, j, t: (b, j, t, 0)), k_spec, vt_spec],
        out_specs=pl.BlockSpec((1, TR, gw), lambda b, j, t: (b, t, j)),
        out_shape=jax.ShapeDtypeStruct((nb, s, ATTN_WIDTH), BF16),
        scratch_shapes=[pltpu.VMEM((ATTN_GROUP, s, TR), F32)],
        compiler_params=_cparams(("arbitrary", "arbitrary", "arbitrary")),
        name="attention",
    )(q, k, v)


def _token_shift(p, prev_row, next_row, mu):
    n = p.shape[0]
    row = lax.broadcasted_iota(jnp.int32, p.shape, 0)
    prev = jnp.where(row == 0, prev_row, pltpu.roll(p, 1, 0))
    nxt = jnp.where(row == n - 1, next_row, pltpu.roll(p, n - 1, 0))
    return p + mu * (0.5 * (prev + nxt) - p)


def _halo_rows(prev_ref, next_ref, tile, nct, ng):
    first = jnp.logical_or(tile == 0, tile == nct)
    last = jnp.logical_or(tile == nct - 1, tile == ng - 1)
    prev_row = jnp.where(first, 0.0, prev_ref[0, SUBLANES - 1:SUBLANES, :])
    next_row = jnp.where(last, 0.0, next_ref[0, 0:1, :])
    return prev_row, next_row


def _rwkv_tile(rev, g, nct, ng):
    if not rev:
        return g
    return jnp.where(g < nct, nct - 1 - g, ng - 1 - (g - nct))


def _block_diag(x):
    z = jnp.zeros((CH, CH), x.dtype)
    return jnp.concatenate([jnp.concatenate([x[:, :CH], z], axis=1), jnp.concatenate([z, x[:, CH:]], axis=1)], axis=0)


def _head_rows(x):
    h0 = (lax.broadcasted_iota(jnp.int32, x.shape, 1) & (LANES - 1)) < HEAD_DIM
    return jnp.concatenate([jnp.where(h0, x, 0.0), jnp.where(h0, 0.0, x)], axis=0).astype(BF16)


def _rwkv_dir_kernel(p_ref, prev_ref, next_ref, mu_ref, kkw_ref, ka_ref, rk_ref, w0_ref, a0_ref, w2_ref, a2_ref,
                     o_ref, bv_ref, s_ref, *, rev, nct, ng):
    g = pl.program_id(1)
    tile = _rwkv_tile(rev, g, nct, ng)
    w = RWKV_WIDTH

    @pl.when(g == 0)
    def _():
        s_ref[...] = jnp.zeros_like(s_ref)

    prev_row, next_row = _halo_rows(prev_ref, next_ref, tile, nct, ng)
    ps = _token_shift(p_ref[0], prev_row, next_row, mu_ref[...])
    r, k, v = ps[:, :w], ps[:, w:2 * w], ps[:, 2 * w:3 * w]
    wd = ps[:, 3 * w:3 * w + 2 * DECAY_LORA]
    ad = ps[:, 3 * w + 2 * DECAY_LORA:3 * w + 2 * DECAY_LORA + 2 * ICLR_LORA]
    kk = k * kkw_ref[...]
    kk = kk * lax.rsqrt(_head_sums(kk * kk) + 1e-12)
    x = w0_ref[...] + _dot3(jnp.tanh(wd), w2_ref[...])
    lw = (-math.exp(-0.5)) * _sigmoid(x)
    a = _sigmoid(a0_ref[...] + _dot(ad.astype(BF16), a2_ref[...].astype(BF16)))
    k_d = k * (1.0 + (a - 1.0) * ka_ref[...])
    bv_ref[0] = (_head_sums(r * k_d * rk_ref[...]) * v).astype(bv_ref.dtype)
    beta = kk * a

    ti = lax.broadcasted_iota(jnp.int32, (CH, PAIR * CH), 0)
    tj = lax.broadcasted_iota(jnp.int32, (CH, PAIR * CH), 1) & (CH - 1)
    prec2 = (tj > ti) if rev else (tj < ti)
    eye2 = ti == tj
    incl2 = jnp.logical_or(prec2, eye2)
    late, early = (tj, ti) if rev else (ti, tj)
    join = [jnp.logical_and(jnp.logical_and((ti ^ tj) < 2 * b, (late & b) != 0), (early & b) == 0)
            for b in (2 ** m for m in range(int(math.log2(CH))))]
    si = lax.broadcasted_iota(jnp.int32, (CH, CH), 0)
    sj = lax.broadcasted_iota(jnp.int32, (CH, CH), 1)
    eye = si == sj
    tri = jnp.where((sj >= si) if rev else (sj <= si), 1.0, 0.0).astype(BF16)
    same_head = (si // HEAD_DIM) == (sj // HEAD_DIM)
    n_chunks = TR // CH
    n_pairs = RWKV_HEADS // PAIR
    chunk_order = list(range(n_chunks - 1, -1, -1) if rev else range(n_chunks))
    units = [(ci, pp) for ci in chunk_order for pp in range(n_pairs)]
    each = lambda f, *lists: [f(*args) for args in zip(*lists)]

    cum = {}
    for ci in chunk_order:
        cs = _dot(tri, jnp.concatenate(_split3(lw[ci * CH:(ci + 1) * CH]), axis=1))
        cum[ci] = cs[:, :w] + cs[:, w:2 * w] + cs[:, 2 * w:]
    blk = lambda x: [x[ci * CH:(ci + 1) * CH, pp * LANES:(pp + 1) * LANES] for ci, pp in units]
    rx, vx, kap, lwx, kx, bet = blk(r), blk(v), blk(kk), blk(lw), blk(k_d), blk(beta)
    cumi = [cum[ci][:, pp * LANES:(pp + 1) * LANES] for ci, pp in units]
    tot = each(lambda c: c[0:1, :] if rev else c[CH - 1:CH, :], cumi)
    cc = each(lambda c: c - c[CH // 2:CH // 2 + 1, :], cumi)
    e_neg = each(lambda c: jnp.exp(-c), cc)
    e_end = each(lambda t, c: jnp.exp(t - c), tot, cumi)
    kap_t = each(lambda x, c, l: x * jnp.exp(c - l), kap, cc, lwx)
    r_t = each(lambda x, c: x * jnp.exp(c), rx, cc)
    kap_0 = each(lambda x, c, l: x * jnp.exp(c - l), kap, cumi, lwx)
    r_0 = each(lambda x, c: x * jnp.exp(c), rx, cumi)
    k_t = each(lambda x, e: x * e, kx, e_neg)
    bet_t = each(lambda x, e: x * e, bet, e_neg)
    k_e = each(lambda x, e: x * e, kx, e_end)
    bet_e = each(lambda x, e: x * e, bet, e_end)
    p_end = each(jnp.exp, tot)
    qs = each(lambda a, b: jnp.concatenate([a, b], axis=0).astype(BF16), kap_t, r_t)
    s_b = each(lambda q, x: _dot_nt(q, _head_rows(x)), qs, bet_t)
    s_k = each(lambda q, x: _dot_nt(q, _head_rows(x)), qs, k_t)
    a_b = each(lambda s: jnp.where(prec2, s[:CH], 0.0), s_b)
    a_k = each(lambda s: jnp.where(prec2, s[:CH], 0.0).astype(BF16), s_k)
    b_b = each(lambda s: jnp.where(incl2, s[CH:], 0.0).astype(BF16), s_b)
    b_k = each(lambda s: jnp.where(incl2, s[CH:], 0.0).astype(BF16), s_k)
    ts = each(lambda a: jnp.where(eye2, 1.0, jnp.where(join[0], -a, 0.0)), a_b)
    for lvl in range(1, len(join)):
        q_b = each(lambda a: _block_diag(jnp.where(join[lvl], a, 0.0).astype(BF16)), a_b)
        t_b = each(lambda t: t.astype(BF16), ts)
        dq = each(lambda t, q: _dot(t, q).astype(BF16), t_b, q_b)
        ts = each(lambda t, e, tb: t - _dot(e, _block_diag(tb)), ts, dq, t_b)
    t_b = each(lambda t: t.astype(BF16), ts)
    v_h = each(_head_rows, vx)
    av = each(_dot, a_k, v_h)
    rhs = each(lambda k0, a: _head_rows(jnp.concatenate([k0, a], axis=1)), kap_0, av)
    wz = each(_dot, t_b, rhs)
    kq = each(lambda x: x[:, :LANES], wz)
    z0 = each(lambda x: x[:, LANES:], wz)
    bz = each(lambda b, x: _dot(b, _head_rows(x)), b_b, wz)
    o0k = each(_dot, b_k, v_h)
    r_new = each(lambda x, b: x - b[:, :LANES], r_0, bz)
    o0 = each(lambda a, b: a - b[:, LANES:], o0k, bz)
    m_c = each(lambda p, b, q: jnp.where(eye, p, 0.0) - jnp.where(same_head, _dot_tn(b, q), 0.0), p_end, bet_e, kq)
    g_c = each(lambda ke, be, x, z: jnp.where(same_head, _dot_tn(jnp.concatenate([ke, be], axis=0),
                                                                  jnp.concatenate([x, -z], axis=0)), 0.0),
               k_e, bet_e, vx, z0)

    states = [s_ref[pp] for pp in range(n_pairs)]
    outs = {}
    lhs = each(lambda a, b: jnp.concatenate([a, b], axis=0).astype(BF16), r_new, m_c)
    for u, (ci, pp) in enumerate(units):
        st_hi, st_lo = _split2(states[pp])
        res = _dot(lhs[u], st_hi) + _dot(lhs[u], st_lo)
        outs[ci, pp] = res[:CH] + o0[u]
        states[pp] = res[CH:] + g_c[u]
    for ci in chunk_order:
        o_ref[0, ci * CH:(ci + 1) * CH, :] = jnp.concatenate([outs[ci, pp] for pp in range(n_pairs)],
                                                             axis=1).astype(o_ref.dtype)
    for pp in range(n_pairs):
        s_ref[pp] = states[pp]


def _rwkv_scan(pr, lp, nct, rev):
    nb, s, pw = pr.shape
    ng = s // TR
    w = RWKV_WIDTH
    hb = TR // SUBLANES
    d = int(rev)
    tile = functools.partial(_rwkv_tile, rev, nct=nct, ng=ng)
    n_row_blocks = s // SUBLANES
    main = pl.BlockSpec((1, TR, pw), lambda b, g: (b, tile(g), 0))
    prev = pl.BlockSpec((1, SUBLANES, pw), lambda b, g: (b, jnp.maximum(tile(g) * hb - 1, 0), 0))
    nxt = pl.BlockSpec((1, SUBLANES, pw), lambda b, g: (b, jnp.minimum((tile(g) + 1) * hb, n_row_blocks - 1), 0))
    vec = lambda n: pl.BlockSpec((1, n), lambda b, g: (0, 0))
    mat = pl.BlockSpec((2 * DECAY_LORA, w), lambda b, g: (0, 0))
    out = pl.BlockSpec((1, TR, w), lambda b, g: (b, tile(g), 0))
    zeros = jnp.zeros((DECAY_LORA, w), F32)
    pad = lambda m: jnp.concatenate([zeros, m] if rev else [m, zeros])
    return pl.pallas_call(
        functools.partial(_rwkv_dir_kernel, rev=rev, nct=nct, ng=ng),
        grid=(nb, ng),
        in_specs=[main, prev, nxt, vec(pw), vec(w), vec(w), vec(w), vec(w), vec(w), mat, mat],
        out_specs=[out, out],
        out_shape=[jax.ShapeDtypeStruct((nb, s, w), BF16), jax.ShapeDtypeStruct((nb, s, w), BF16)],
        scratch_shapes=[pltpu.VMEM((RWKV_HEADS // PAIR, LANES, LANES), F32)],
        compiler_params=_cparams(("arbitrary", "arbitrary")),
        name="rwkv_scan_bwd" if rev else "rwkv_scan_fwd",
    )(pr, pr, pr, lp['mu'].reshape(1, pw), lp['kk'].reshape(1, w), lp['ka'].reshape(1, w), lp['rk'].reshape(1, w),
      lp['w0'][d].reshape(1, w), lp['a0'][d].reshape(1, w), pad(lp['w2'][d]), pad(lp['a2'][d]))


def _rwkv_out_kernel(of_ref, ob_ref, bf_ref, bb_ref, gd_ref, gprev_ref, gnext_ref, mu_ref, g2_ref, gw_ref, gb_ref,
                     y_ref, *, nct, ng):
    tile = pl.program_id(1)
    o = of_ref[0].astype(F32) + ob_ref[0].astype(F32)
    cen = o - _head_sums(o, 1.0 / HEAD_DIM)
    var = _head_sums(cen * cen, 1.0 / HEAD_DIM)
    y = cen * lax.rsqrt(var + GN_EPS) * gw_ref[...] + gb_ref[...] + (bf_ref[0].astype(F32) + bb_ref[0].astype(F32))
    prev_row, next_row = _halo_rows(gprev_ref, gnext_ref, tile, nct, ng)
    gd = _token_shift(gd_ref[0], prev_row, next_row, mu_ref[...])
    gate = _dot(_sigmoid(gd).astype(BF16), g2_ref[...].astype(BF16))
    y_ref[0] = (y * gate).astype(BF16)


def _rwkv_out(o_f, o_b, bv_f, bv_b, pr, lp, nct):
    nb, s, w = o_f.shape
    ng = s // TR
    hb = TR // SUBLANES
    gcol = (SHIFT_WIDTH - GATE_LORA) // GATE_LORA
    n_row_blocks = s // SUBLANES
    row = pl.BlockSpec((1, TR, w), lambda b, t: (b, t, 0))
    vec = lambda n: pl.BlockSpec((1, n), lambda b, t: (0, 0))
    return pl.pallas_call(
        functools.partial(_rwkv_out_kernel, nct=nct, ng=ng),
        grid=(nb, ng),
        in_specs=[row, row, row, row,
                  pl.BlockSpec((1, TR, GATE_LORA), lambda b, t: (b, t, gcol)),
                  pl.BlockSpec((1, SUBLANES, GATE_LORA), lambda b, t: (b, jnp.maximum(t * hb - 1, 0), gcol)),
                  pl.BlockSpec((1, SUBLANES, GATE_LORA),
                               lambda b, t: (b, jnp.minimum((t + 1) * hb, n_row_blocks - 1), gcol)),
                  vec(GATE_LORA), pl.BlockSpec((GATE_LORA, w), lambda b, t: (0, 0)), vec(w), vec(w)],
        out_specs=row,
        out_shape=jax.ShapeDtypeStruct((nb, s, w), BF16),
        compiler_params=_cparams(("arbitrary", "arbitrary")),
        name="rwkv_out",
    )(o_f, o_b, bv_f, bv_b, pr, pr, pr, lp['mu'][SHIFT_WIDTH - GATE_LORA:].reshape(1, GATE_LORA), lp['g2'],
      lp['gn_w'].reshape(1, w), lp['gn_b'].reshape(1, w))


def _top2(logits):
    lane = lax.broadcasted_iota(jnp.int32, logits.shape, 1)
    v1 = jnp.max(logits, axis=-1, keepdims=True)
    i1 = jnp.min(jnp.where(logits == v1, lane, LANES), axis=-1, keepdims=True)
    rest = jnp.where(lane == i1, -jnp.inf, logits)
    v2 = jnp.max(rest, axis=-1, keepdims=True)
    i2 = jnp.min(jnp.where(rest == v2, lane, LANES), axis=-1, keepdims=True)
    e = jnp.exp(v2 - v1)
    g1 = 1.0 / (1.0 + e)
    g2 = e / (1.0 + e)
    out = jnp.where(lane == 0, i1.astype(F32), 0.0)
    out = jnp.where(lane == 1, i2.astype(F32), out)
    out = jnp.where(lane == 2, g1, out)
    return jnp.where(lane == 3, g2, out)


def _out_proj_kernel(x_ref, att_ref, rw_ref, mod_ref, g_ref, wa_ref, wr_ref, *rest, moe):
    if moe:
        wrt_ref, xo_ref, h_ref, route_ref = rest
    else:
        xo_ref, h_ref = rest
    mix = _dot(att_ref[0], wa_ref[...]) + _dot(rw_ref[0], wr_ref[...])
    x = x_ref[0] + mod_ref[0, 2:3] * mix
    xo_ref[0] = x
    h = _norm_mod(x, g_ref[...], mod_ref[0, 3:4], mod_ref[0, 4:5])
    h_ref[0] = h.astype(h_ref.dtype)
    if moe:
        lane = lax.broadcasted_iota(jnp.int32, (TR, LANES), 1)
        logits = jnp.where(lane < N_EXPERTS, _dot3(h, wrt_ref[...]), -jnp.inf)
        route_ref[0] = _top2(logits)


def _out_proj(xa, att, rw, mod, g, wo_a, wo_r, nct, w_router=None):
    nb, s, d = xa.shape
    moe = w_router is not None
    row = lambda n: pl.BlockSpec((1, TR, n), lambda b, t: (b, t, 0))
    full = lambda a: pl.BlockSpec(a.shape, lambda b, t: (0, 0))
    in_specs = [row(d), row(ATTN_WIDTH), row(RWKV_WIDTH), pl.BlockSpec((1, N_MOD, d), _mod_index(nb, nct)),
                pl.BlockSpec((1, d), lambda b, t: (0, 0)), full(wo_a), full(wo_r)]
    args = [xa, att, rw, mod, g.reshape(1, d), wo_a, wo_r]
    out_specs = [row(d), row(d)]
    out_shape = [jax.ShapeDtypeStruct((nb, s, d), F32), jax.ShapeDtypeStruct((nb, s, d), F32 if moe else BF16)]
    if moe:
        wrt = jnp.pad(w_router, ((0, 0), (0, LANES - N_EXPERTS)))
        in_specs.append(full(wrt))
        args.append(wrt)
        out_specs.append(row(LANES))
        out_shape.append(jax.ShapeDtypeStruct((nb, s, LANES), F32))
    return pl.pallas_call(
        functools.partial(_out_proj_kernel, moe=moe),
        grid=(nb, s // TR),
        in_specs=in_specs, out_specs=out_specs, out_shape=out_shape,
        compiler_params=_cparams(("arbitrary", "arbitrary")),
        name="out_proj_moe" if moe else "out_proj",
    )(*args)


def _ffn_kernel(x_ref, h_ref, mod_ref, wg_ref, wu_ref, wd_ref, o_ref):
    h = h_ref[0]
    gt = _dot(h, wg_ref[...])
    up = _dot(h, wu_ref[...])
    act = (gt * _sigmoid(gt) * up).astype(BF16)
    o_ref[0] = x_ref[0] + mod_ref[0, 5:6] * _dot(act, wd_ref[...])


def _ffn_dense(xa, h, mod, wg, wu, wd, nct):
    nb, s, d = xa.shape
    row = pl.BlockSpec((1, TR, d), lambda b, t: (b, t, 0))
    resident = lambda a: pl.BlockSpec(a.shape, lambda b, t: (0, 0), pipeline_mode=pl.Buffered(1))
    return pl.pallas_call(
        _ffn_kernel,
        grid=(nb, s // TR),
        in_specs=[row, row, pl.BlockSpec((1, N_MOD, d), _mod_index(nb, nct)), resident(wg), resident(wu),
                  resident(wd)],
        out_specs=row,
        out_shape=jax.ShapeDtypeStruct((nb, s, d), F32),
        compiler_params=_cparams(("arbitrary", "arbitrary")),
        name="ffn_dense",
    )(xa, h, mod, wg, wu, wd)


def _row_copy(src_hbm, idx_ref, dst_ref, sem, r):
    return pltpu.make_async_copy(src_hbm.at[pl.ds(idx_ref[0, 0, r], 1), :], dst_ref.at[pl.ds(r, 1), :], sem)


def _moe_kernel(be_ref, nu_ref, tok_ref, tok_next_ref, h_hbm, wg_ref, wu_ref, wd_ref, o_ref, x_buf, sem):
    i = pl.program_id(0)
    j = pl.program_id(1)
    nblk = pl.num_programs(0)
    last_j = pl.num_programs(1) - 1
    used = i < nu_ref[0]
    slot = i % 2
    cur, nxt = x_buf.at[slot], x_buf.at[1 - slot]
    rows = cur.shape[0]

    def wait_rows(idx_ref, dst, s):
        def body(r, carry):
            _row_copy(h_hbm, idx_ref, dst, s, r).wait()
            return carry
        lax.fori_loop(0, rows, body, 0, unroll=8)

    @pl.when(j == 0)
    def _():
        o_ref[...] = jnp.zeros_like(o_ref)

        @pl.when(i == 0)
        def _():
            def body(r, carry):
                _row_copy(h_hbm, tok_ref, cur, sem.at[slot], r).start()
                return carry
            lax.fori_loop(0, rows, body, 0, unroll=8)

        @pl.when(jnp.logical_or(i == 0, i - 1 < nu_ref[0]))
        def _():
            wait_rows(tok_ref, cur, sem.at[slot])

    def compute():
        x = cur[...].astype(BF16)
        gt = _dot(x, wg_ref[0])
        up = _dot(x, wu_ref[0])
        act = (gt * _sigmoid(gt) * up).astype(BF16)
        o_ref[...] += _dot(act, wd_ref[0])

    @pl.when(jnp.logical_and(used, j != last_j))
    def _():
        compute()

    @pl.when(jnp.logical_and(used, j == last_j))
    def _():
        compute()
        for r in range(rows):
            _row_copy(h_hbm, tok_next_ref, nxt, sem.at[1 - slot], r).start()

        @pl.when(i == nblk - 1)
        def _():
            wait_rows(tok_next_ref, nxt, sem.at[1 - slot])


def _moe_experts(h, slot_token, block_expert, n_used, wg, wu, wd):
    t, d = h.shape
    dff = wg.shape[2]
    nblk = slot_token.shape[0] // MOE_BM
    tok = slot_token.reshape(nblk, 1, MOE_BM)
    grid_spec = pltpu.PrefetchScalarGridSpec(
        num_scalar_prefetch=2,
        grid=(nblk, dff // MOE_TF),
        in_specs=[pl.BlockSpec((1, 1, MOE_BM), lambda i, j, be, nu: (i, 0, 0), memory_space=pltpu.SMEM),
                  pl.BlockSpec((1, 1, MOE_BM), lambda i, j, be, nu: (jnp.minimum(i + 1, nblk - 1), 0, 0),
                               memory_space=pltpu.SMEM),
                  pl.BlockSpec(memory_space=pl.ANY),
                  pl.BlockSpec((1, d, MOE_TF), lambda i, j, be, nu: (be[i], 0, j)),
                  pl.BlockSpec((1, d, MOE_TF), lambda i, j, be, nu: (be[i], 0, j)),
                  pl.BlockSpec((1, MOE_TF, d), lambda i, j, be, nu: (be[i], j, 0))],
        out_specs=pl.BlockSpec((MOE_BM, d), lambda i, j, be, nu: (i, 0)),
        scratch_shapes=[pltpu.VMEM((2, MOE_BM, d), F32), pltpu.SemaphoreType.DMA((2,))],
    )
    return pl.pallas_call(
        _moe_kernel,
        grid_spec=grid_spec,
        out_shape=jax.ShapeDtypeStruct((nblk * MOE_BM, d), F32),
        compiler_params=_cparams(("arbitrary", "arbitrary")),
        name="moe_experts",
    )(block_expert, n_used, tok, tok, h, wg, wu, wd)


def _gather_rows(src_hbm, idx_ref, dst_ref, sem):
    n = dst_ref.shape[0]

    def start(r, carry):
        _row_copy(src_hbm, idx_ref, dst_ref, sem, r).start()
        return carry

    def wait(r, carry):
        _row_copy(src_hbm, idx_ref, dst_ref, sem, r).wait()
        return carry

    lax.fori_loop(0, n, start, 0, unroll=8)
    lax.fori_loop(0, n, wait, 0, unroll=8)


def _combine_kernel(s1_ref, s2_ref, x_ref, route_ref, mod_ref, ys_hbm, o_ref, y1_ref, y2_ref, sem):
    _gather_rows(ys_hbm, s1_ref, y1_ref, sem.at[0])
    _gather_rows(ys_hbm, s2_ref, y2_ref, sem.at[1])
    route = route_ref[0]
    f = y1_ref[...] * route[:, 2:3] + y2_ref[...] * route[:, 3:4]
    o_ref[0] = x_ref[0] + mod_ref[0, 5:6] * f


def _moe_combine(xa, route, mod, ys, slot1, slot2, nct):
    nb, s, d = xa.shape
    ng = s // TR
    idx = pl.BlockSpec((1, 1, TR), lambda b, t: (b * ng + t, 0, 0), memory_space=pltpu.SMEM)
    row = lambda n: pl.BlockSpec((1, TR, n), lambda b, t: (b, t, 0))
    return pl.pallas_call(
        _combine_kernel,
        grid=(nb, ng),
        in_specs=[idx, idx, row(d), row(LANES), pl.BlockSpec((1, N_MOD, d), _mod_index(nb, nct)),
                  pl.BlockSpec(memory_space=pl.ANY)],
        out_specs=row(d),
        out_shape=jax.ShapeDtypeStruct((nb, s, d), F32),
        scratch_shapes=[pltpu.VMEM((TR, d), F32), pltpu.VMEM((TR, d), F32), pltpu.SemaphoreType.DMA((2,))],
        compiler_params=_cparams(("arbitrary", "arbitrary")),
        name="moe_combine",
    )(slot1.reshape(nb * ng, 1, TR), slot2.reshape(nb * ng, 1, TR), xa, route, mod, ys)


def _moe_ffn(xa, h, route, mod, wg, wu, wd, nct):
    nb, s, d = xa.shape
    t = nb * s
    n_assign = 2 * t
    nblk = (n_assign + N_EXPERTS * (MOE_BM - 1) + MOE_BM - 1) // MOE_BM
    n_slots = nblk * MOE_BM
    expert = route.reshape(t, LANES)[:, 0:2].astype(jnp.int32).reshape(-1)
    onehot = (expert[:, None] == jnp.arange(N_EXPERTS, dtype=jnp.int32)[None, :]).astype(jnp.int32)
    rank = jnp.sum((jnp.cumsum(onehot, axis=0) - onehot) * onehot, axis=1)
    counts = jnp.sum(onehot, axis=0)
    padded = (counts + MOE_BM - 1) // MOE_BM * MOE_BM
    pad_end = jnp.cumsum(padded)
    pad_start = pad_end - padded
    slot = pad_start[expert] + rank
    token = jnp.repeat(jnp.arange(t, dtype=jnp.int32), 2)
    slot_token = jnp.zeros((n_slots,), jnp.int32).at[slot].set(token)
    block_start = jnp.arange(nblk, dtype=jnp.int32) * MOE_BM
    block_expert = jnp.minimum(jnp.sum((pad_end[None, :] <= block_start[:, None]).astype(jnp.int32), axis=1),
                               N_EXPERTS - 1)
    n_used = (pad_end[-1] // MOE_BM).astype(jnp.int32).reshape(1)
    ys = _moe_experts(h.reshape(t, d), slot_token, block_expert, n_used, wg, wu, wd)
    slot2 = slot.reshape(t, 2)
    return _moe_combine(xa, route, mod, ys, slot2[:, 0], slot2[:, 1], nct)


def kernel(x, c, ctx, c_ctx, ada_w, ada_b, norm1_g, norm2_g, w_in, w_out, q_gain, k_gain, shift_mu, rw_w0, rw_w2,
           rw_a0, rw_a2, rw_g2, rw_kk, rw_ka, rw_rk, rw_gn_w, rw_gn_b, ffn_wg, ffn_wu, ffn_wd, moe_router, moe_wg,
           moe_wu, moe_wd):
    nb, l, d = x.shape
    lc = ctx.shape[1]
    depth = ada_w.shape[0]
    assert lc % TR == 0 and l % TR == 0 and l % GRID_W == 0 and TR % ATTN_KT == 0 and TR % CH == 0
    nct = lc // TR
    xa = jnp.concatenate([ctx, x], axis=1)

    mod_rows = -(-(nb + 1) // SUBLANES) * SUBLANES
    c_all = jnp.concatenate([c, c_ctx[None], jnp.zeros((mod_rows - nb - 1, d), F32)], axis=0)
    mod_all = _ada_all(c_all, ada_w, ada_b)[:, :nb + 1].reshape(depth, nb + 1, N_MOD, d)
    cos, slo, shi = _rope_tables(lc, l)

    for i in range(depth):
        mod = mod_all[i]
        lp = dict(mu=shift_mu[i], w0=rw_w0[i], w2=rw_w2[i], a0=rw_a0[i], a2=rw_a2[i], g2=rw_g2[i], kk=rw_kk[i],
                  ka=rw_ka[i], rk=rw_rk[i], gn_w=rw_gn_w[i], gn_b=rw_gn_b[i])
        wi = w_in[i].astype(BF16)
        pa, pr = _in_proj(xa, mod, norm1_g[i], wi[:, :ATTN_IN], wi[:, ATTN_IN:], nct)
        q, k, v = _attn_prep(pa, cos, slo, shi, q_gain[i], k_gain[i])
        att = _attention(q, k, v, nct, lc)
        o_f, bv_f = _rwkv_scan(pr, lp, nct, False)
        o_b, bv_b = _rwkv_scan(pr, lp, nct, True)
        rw = _rwkv_out(o_f, o_b, bv_f, bv_b, pr, lp, nct)
        wo = w_out[i].astype(BF16)
        j = i // 2
        if i % 2 == 0:
            xa, h = _out_proj(xa, att, rw, mod, norm2_g[i], wo[:ATTN_WIDTH], wo[ATTN_WIDTH:], nct)
            xa = _ffn_dense(xa, h, mod, ffn_wg[j].astype(BF16), ffn_wu[j].astype(BF16), ffn_wd[j].astype(BF16), nct)
        else:
            xa, h, route = _out_proj(xa, att, rw, mod, norm2_g[i], wo[:ATTN_WIDTH], wo[ATTN_WIDTH:], nct,
                                     moe_router[j])
            xa = _moe_ffn(xa, h, route, mod, moe_wg[j].astype(BF16), moe_wu[j].astype(BF16), moe_wd[j].astype(BF16),
                          nct)
    return xa[:, lc:]
```

```python
import functools
import math

import jax
import jax.numpy as jnp
from jax import lax
from jax.experimental import pallas as pl
from jax.experimental.pallas import tpu as pltpu

F32 = jnp.float32
BF16 = jnp.bfloat16
HI = lax.Precision.HIGHEST

HEAD_DIM = 64
ROPE_HALF = HEAD_DIM // 2
ROPE_THETA = 10000.0
GRID_W = 64
ATTN_HEADS = 8
ATTN_KV_HEADS = 2
ATTN_GROUP = ATTN_HEADS // ATTN_KV_HEADS
ATTN_WIDTH = ATTN_HEADS * HEAD_DIM
KV_WIDTH = ATTN_KV_HEADS * HEAD_DIM
ATTN_IN = ATTN_WIDTH + 2 * KV_WIDTH
ATTN_SCALE = HEAD_DIM ** -0.5
LOG2E = math.log2(math.e)
RWKV_HEADS = 8
RWKV_WIDTH = RWKV_HEADS * HEAD_DIM
DECAY_LORA = 64
ICLR_LORA = 64
GATE_LORA = 128
SHIFT_WIDTH = 3 * RWKV_WIDTH + 2 * DECAY_LORA + 2 * ICLR_LORA + GATE_LORA
N_EXPERTS = 8
N_MOD = 6
EPS = 1e-6
GN_EPS = 64e-5

LANES = 128
SUBLANES = 8
TR = 256
CH = 128
PAIR = LANES // HEAD_DIM
MXU_DIM = 256
ATTN_KT = 256
MOE_BM = 512
MOE_TF = 1792
VMEM_LIMIT = 56 * 1024 * 1024


def _cparams(sem):
    return pltpu.CompilerParams(dimension_semantics=sem, vmem_limit_bytes=VMEM_LIMIT)


def _dot(a, b, prec=None):
    return jnp.dot(a, b, preferred_element_type=F32, precision=prec)


def _dot_nt(a, b, prec=None):
    return lax.dot_general(a, b, (((1,), (1,)), ((), ())), preferred_element_type=F32, precision=prec)


def _dot_tn(a, b, prec=None):
    return lax.dot_general(a, b, (((0,), (0,)), ((), ())), preferred_element_type=F32, precision=prec)


def _sigmoid(x):
    return 1.0 / (1.0 + jnp.exp(-x))


def _split2(x):
    hi = x.astype(BF16)
    return hi, (x - hi.astype(F32)).astype(BF16)


def _split3(x):
    hi = x.astype(BF16)
    r = x - hi.astype(F32)
    mid = r.astype(BF16)
    return hi, mid, (r - mid.astype(F32)).astype(BF16)


def _dot3(a, b):
    a_hi, a_lo = _split2(a)
    b_hi, b_lo = _split2(b)
    return _dot(a_hi, b_hi) + (_dot(a_hi, b_lo) + _dot(a_lo, b_hi))


def _head_sums(x, scale=1.0):
    r, n = x.shape
    i = lax.broadcasted_iota(jnp.int32, (MXU_DIM, MXU_DIM), 0) // HEAD_DIM
    j = lax.broadcasted_iota(jnp.int32, (MXU_DIM, MXU_DIM), 1) // HEAD_DIM
    ones = jnp.where(i == j, scale, 0.0).astype(BF16)
    nfull = n // MXU_DIM
    parts = []
    if nfull:
        xs = jnp.concatenate([x[:, c * MXU_DIM:(c + 1) * MXU_DIM] for c in range(nfull)], axis=0)
        m = nfull * r
        s = _dot(jnp.concatenate(_split2(xs), axis=0), ones)
        s = s[:m] + s[m:]
        parts += [s[c * r:(c + 1) * r] for c in range(nfull)]
    if n % MXU_DIM:
        w = n % MXU_DIM
        s = _dot(jnp.concatenate(_split2(x[:, nfull * MXU_DIM:]), axis=0), ones[:w, :w])
        parts.append(s[:r] + s[r:])
    return jnp.concatenate(parts, axis=1)


def _ada_kernel(c_ref, w_ref, b_ref, o_ref):
    c = c_ref[...]
    sc = c * _sigmoid(c)
    o_ref[0] = _dot(sc, w_ref[0], HI) + b_ref[0]


def _ada_all(c_all, ada_w, ada_b):
    depth, d, n = ada_w.shape
    rows = c_all.shape[0]
    tn = 1536
    return pl.pallas_call(
        _ada_kernel,
        grid=(depth, n // tn),
        in_specs=[pl.BlockSpec((rows, d), lambda i, j: (0, 0)),
                  pl.BlockSpec((1, d, tn), lambda i, j: (i, 0, j)),
                  pl.BlockSpec((1, 1, tn), lambda i, j: (i, 0, j))],
        out_specs=pl.BlockSpec((1, rows, tn), lambda i, j: (i, 0, j)),
        out_shape=jax.ShapeDtypeStruct((depth, rows, n), F32),
        compiler_params=_cparams(("arbitrary", "arbitrary")),
        name="ada_mod",
    )(c_all, ada_w, ada_b.reshape(depth, 1, n))


def _norm_mod(x, g, shift, scale):
    y = x * lax.rsqrt(jnp.mean(x * x, axis=-1, keepdims=True) + EPS)
    return (y * g) * (1.0 + scale) + shift


def _in_proj_kernel(x_ref, mod_ref, g_ref, wa_ref, wr_ref, pa_ref, pr_ref):
    h = _norm_mod(x_ref[0], g_ref[...], mod_ref[0, 0:1], mod_ref[0, 1:2]).astype(BF16)
    pa_ref[0] = _dot(h, wa_ref[...])
    pr_ref[0] = _dot(h, wr_ref[...])


def _mod_index(nb, nct):
    return lambda b, t: (jnp.where(t < nct, nb, b), 0, 0)


def _in_proj(xa, mod, g, wa, wr, nct):
    nb, s, d = xa.shape
    na, nr = wa.shape[1], wr.shape[1]
    return pl.pallas_call(
        _in_proj_kernel,
        grid=(nb, s // TR),
        in_specs=[pl.BlockSpec((1, TR, d), lambda b, t: (b, t, 0)),
                  pl.BlockSpec((1, N_MOD, d), _mod_index(nb, nct)),
                  pl.BlockSpec((1, d), lambda b, t: (0, 0)),
                  pl.BlockSpec((d, na), lambda b, t: (0, 0)),
                  pl.BlockSpec((d, nr), lambda b, t: (0, 0))],
        out_specs=[pl.BlockSpec((1, TR, na), lambda b, t: (b, t, 0)),
                   pl.BlockSpec((1, TR, nr), lambda b, t: (b, t, 0))],
        out_shape=[jax.ShapeDtypeStruct((nb, s, na), F32),
                   jax.ShapeDtypeStruct((nb, s, nr), F32)],
        compiler_params=_cparams(("arbitrary", "arbitrary")),
        name="in_proj",
    )(xa, mod, g.reshape(1, d), wa, wr)


def _rope(x, cos, sin_lo, sin_hi):
    n = x.shape[-1]
    return x * cos + pltpu.roll(x, n - ROPE_HALF, 1) * sin_lo + pltpu.roll(x, ROPE_HALF, 1) * sin_hi


def _attn_prep_kernel(pa_ref, cos_ref, slo_ref, shi_ref, qg_ref, kg_ref, q_ref, k_ref, v_ref):
    pa = pa_ref[0]
    qk = pa[:, :ATTN_WIDTH + KV_WIDTH]
    v = pa[:, ATTN_WIDTH + KV_WIDTH:ATTN_IN]
    cos, slo, shi = cos_ref[...], slo_ref[...], shi_ref[...]
    inv = lax.rsqrt(_head_sums(qk * qk, 1.0 / HEAD_DIM) + EPS)
    qn = qk[:, :ATTN_WIDTH] * inv[:, :ATTN_WIDTH] * qg_ref[...]
    qr = _rope(qn, cos, slo, shi) * (ATTN_SCALE * LOG2E)
    kn = qk[:, ATTN_WIDTH:] * inv[:, ATTN_WIDTH:] * kg_ref[...]
    kr = _rope(kn, cos[:, :KV_WIDTH], slo[:, :KV_WIDTH], shi[:, :KV_WIDTH])
    lane = lax.broadcasted_iota(jnp.int32, (TR, LANES), 1)
    low = lane < HEAD_DIM
    for j in range(ATTN_KV_HEADS):
        kj = kr if j % PAIR == 0 else pltpu.roll(kr, HEAD_DIM, 1)
        vj = v if j % PAIR == 0 else pltpu.roll(v, HEAD_DIM, 1)
        k_ref[0, j] = jnp.where(low, kj, 0.0).astype(BF16)
        v_ref[0, j] = jnp.transpose(jnp.where(low, vj, jnp.where(lane == HEAD_DIM, 1.0, 0.0))).astype(BF16)
    for h in range(ATTN_HEADS):
        src = qr[:, (h // PAIR) * LANES:(h // PAIR + 1) * LANES]
        if h % PAIR:
            src = pltpu.roll(src, HEAD_DIM, 1)
        q_ref[0, h] = jnp.where(low, src, 0.0).astype(BF16)


def _attn_prep(pa, cos, slo, shi, q_gain, k_gain):
    nb, s, _ = pa.shape
    qg = jnp.tile(q_gain, ATTN_HEADS).reshape(1, ATTN_WIDTH)
    kg = jnp.tile(k_gain, ATTN_KV_HEADS).reshape(1, KV_WIDTH)
    tab = pl.BlockSpec((TR, ATTN_WIDTH), lambda t, b: (t, 0))
    return pl.pallas_call(
        _attn_prep_kernel,
        grid=(s // TR, nb),
        in_specs=[pl.BlockSpec((1, TR, ATTN_IN), lambda t, b: (b, t, 0)), tab, tab, tab,
                  pl.BlockSpec((1, ATTN_WIDTH), lambda t, b: (0, 0)),
                  pl.BlockSpec((1, KV_WIDTH), lambda t, b: (0, 0))],
        out_specs=[pl.BlockSpec((1, ATTN_HEADS, TR, LANES), lambda t, b: (b, 0, t, 0)),
                   pl.BlockSpec((1, ATTN_KV_HEADS, TR, LANES), lambda t, b: (b, 0, t, 0)),
                   pl.BlockSpec((1, ATTN_KV_HEADS, LANES, TR), lambda t, b: (b, 0, 0, t))],
        out_shape=[jax.ShapeDtypeStruct((nb, ATTN_HEADS, s, LANES), BF16),
                   jax.ShapeDtypeStruct((nb, ATTN_KV_HEADS, s, LANES), BF16),
                   jax.ShapeDtypeStruct((nb, ATTN_KV_HEADS, LANES, s), BF16)],
        compiler_params=_cparams(("arbitrary", "arbitrary")),
        name="attn_prep",
    )(pa, cos, slo, shi, qg, kg)


def _rope_tables(lc, l):
    rows = l // GRID_W
    row = jnp.repeat(jnp.arange(rows, dtype=F32), GRID_W)
    col = jnp.tile(jnp.arange(GRID_W, dtype=F32), rows)
    inv = ROPE_THETA ** (-jnp.arange(0, ROPE_HALF, 2, dtype=F32) / ROPE_HALF)
    ang = jnp.concatenate([row[:, None] * inv, col[:, None] * inv], axis=-1)
    cos = jnp.concatenate([jnp.ones((lc, ROPE_HALF), F32), jnp.cos(ang)], axis=0)
    sin = jnp.concatenate([jnp.zeros((lc, ROPE_HALF), F32), jnp.sin(ang)], axis=0)
    zero = jnp.zeros_like(sin)
    head = lambda lo, hi: jnp.tile(jnp.concatenate([lo, hi], axis=-1), (1, ATTN_HEADS))
    return head(cos, cos), head(-sin, zero), head(zero, sin)


def _attn_kernel(q_ref, k_ref, vt_ref, o_ref, s_ref, *, nct, lc):
    t = pl.program_id(2)
    n_all = k_ref.shape[2]
    low = lax.broadcasted_iota(jnp.int32, (TR, LANES), 1) < HEAD_DIM

    def run(n_keys):
        tiles = [slice(j * ATTN_KT, (j + 1) * ATTN_KT) for j in range(n_keys // ATTN_KT)]
        mx, acc = [None] * ATTN_GROUP, [None] * ATTN_GROUP
        m = [None] * ATTN_GROUP
        for stage in range(ATTN_GROUP + 1):
            g1, g2 = stage, stage - 1
            if g1 < ATTN_GROUP:
                mx[g1] = jnp.full((ATTN_KT, TR), -jnp.inf, F32)
            if g2 >= 0:
                m[g2] = jnp.max(mx[g2], axis=0, keepdims=True)
                acc[g2] = jnp.zeros((LANES, TR), F32)
            for ks in tiles:
                if g1 < ATTN_GROUP:
                    st = _dot_nt(k_ref[0, 0, ks, :], q_ref[0, g1])
                    s_ref[g1, ks, :] = st
                    mx[g1] = jnp.maximum(mx[g1], st)
                if g2 >= 0:
                    pt = jnp.exp2(s_ref[g2, ks, :] - m[g2]).astype(BF16)
                    acc[g2] = acc[g2] + _dot(vt_ref[0, 0, :, ks], pt)
        out = [jnp.transpose(a / a[HEAD_DIM:HEAD_DIM + 1, :]) for a in acc]
        for pr in range(ATTN_GROUP // PAIR):
            hi = pltpu.roll(out[PAIR * pr + 1], HEAD_DIM, 1)
            o_ref[0, :, pr * LANES:(pr + 1) * LANES] = jnp.where(low, out[PAIR * pr], hi).astype(BF16)

    @pl.when(t < nct)
    def _():
        run(lc)

    @pl.when(t >= nct)
    def _():
        run(n_all)


def _attention(q, k, v, nct, lc):
    nb, _, s, _ = q.shape
    gw = ATTN_GROUP * HEAD_DIM
    k_spec = pl.BlockSpec((1, 1, s, LANES), lambda b, j, t: (b, j, 0, 0))
    vt_spec = pl.BlockSpec((1, 1, LANES, s), lambda b, j, t: (b, j, 0, 0))
    return pl.pallas_call(
        functools.partial(_attn_kernel, nct=nct, lc=lc),
        grid=(nb, ATTN_KV_HEADS, s // TR),
        in_specs=[pl.BlockSpec((1, ATTN_GROUP, TR, LANES), lambda b, j, t: (b, j, t, 0)), k_spec, vt_spec],
        out_specs=pl.BlockSpec((1, TR, gw), lambda b, j, t: (b, t, j)),
        out_shape=jax.ShapeDtypeStruct((nb, s, ATTN_WIDTH), BF16),
        scratch_shapes=[pltpu.VMEM((ATTN_GROUP, s, TR), F32)],
        compiler_params=_cparams(("arbitrary", "arbitrary", "arbitrary")),
        name="attention",
    )(q, k, v)


def _token_shift(p, prev_row, next_row, mu):
    n = p.shape[0]
    row = lax.broadcasted_iota(jnp.int32, p.shape, 0)
    prev = jnp.where(row == 0, prev_row, pltpu.roll(p, 1, 0))
    nxt = jnp.where(row == n - 1, next_row, pltpu.roll(p, n - 1, 0))
    return p + mu * (0.5 * (prev + nxt) - p)


def _halo_rows(prev_ref, next_ref, tile, nct, ng):
    first = jnp.logical_or(tile == 0, tile == nct)
    last = jnp.logical_or(tile == nct - 1, tile == ng - 1)
    prev_row = jnp.where(first, 0.0, prev_ref[0, SUBLANES - 1:SUBLANES, :])
    next_row = jnp.where(last, 0.0, next_ref[0, 0:1, :])
    return prev_row, next_row


def _rwkv_tile(rev, g, nct, ng):
    if not rev:
        return g
    return jnp.where(g < nct, nct - 1 - g, ng - 1 - (g - nct))


def _block_diag(x):
    z = jnp.zeros((CH, CH), x.dtype)
    return jnp.concatenate([jnp.concatenate([x[:, :CH], z], axis=1), jnp.concatenate([z, x[:, CH:]], axis=1)], axis=0)


def _head_rows(x):
    h0 = (lax.broadcasted_iota(jnp.int32, x.shape, 1) & (LANES - 1)) < HEAD_DIM
    return jnp.concatenate([jnp.where(h0, x, 0.0), jnp.where(h0, 0.0, x)], axis=0).astype(BF16)


def _rwkv_dir_kernel(p_ref, prev_ref, next_ref, mu_ref, kkw_ref, ka_ref, rk_ref, w0_ref, a0_ref, w2_ref, a2_ref,
                     o_ref, bv_ref, s_ref, *, rev, nct, ng):
    g = pl.program_id(1)
    tile = _rwkv_tile(rev, g, nct, ng)
    w = RWKV_WIDTH

    @pl.when(g == 0)
    def _():
        s_ref[...] = jnp.zeros_like(s_ref)

    prev_row, next_row = _halo_rows(prev_ref, next_ref, tile, nct, ng)
    ps = _token_shift(p_ref[0], prev_row, next_row, mu_ref[...])
    r, k, v = ps[:, :w], ps[:, w:2 * w], ps[:, 2 * w:3 * w]
    wd = ps[:, 3 * w:3 * w + 2 * DECAY_LORA]
    ad = ps[:, 3 * w + 2 * DECAY_LORA:3 * w + 2 * DECAY_LORA + 2 * ICLR_LORA]
    kk = k * kkw_ref[...]
    kk = kk * lax.rsqrt(_head_sums(kk * kk) + 1e-12)
    x = w0_ref[...] + _dot3(jnp.tanh(wd), w2_ref[...])
    lw = (-math.exp(-0.5)) * _sigmoid(x)
    a = _sigmoid(a0_ref[...] + _dot(ad.astype(BF16), a2_ref[...].astype(BF16)))
    k_d = k * (1.0 + (a - 1.0) * ka_ref[...])
    bv_ref[0] = (_head_sums(r * k_d * rk_ref[...]) * v).astype(bv_ref.dtype)
    beta = kk * a

    ti = lax.broadcasted_iota(jnp.int32, (CH, PAIR * CH), 0)
    tj = lax.broadcasted_iota(jnp.int32, (CH, PAIR * CH), 1) & (CH - 1)
    prec2 = (tj > ti) if rev else (tj < ti)
    eye2 = ti == tj
    incl2 = jnp.logical_or(prec2, eye2)
    late, early = (tj, ti) if rev else (ti, tj)
    join = [jnp.logical_and(jnp.logical_and((ti ^ tj) < 2 * b, (late & b) != 0), (early & b) == 0)
            for b in (2 ** m for m in range(int(math.log2(CH))))]
    si = lax.broadcasted_iota(jnp.int32, (CH, CH), 0)
    sj = lax.broadcasted_iota(jnp.int32, (CH, CH), 1)
    eye = si == sj
    tri = jnp.where((sj >= si) if rev else (sj <= si), 1.0, 0.0).astype(BF16)
    same_head = (si // HEAD_DIM) == (sj // HEAD_DIM)
    n_chunks = TR // CH
    n_pairs = RWKV_HEADS // PAIR
    chunk_order = list(range(n_chunks - 1, -1, -1) if rev else range(n_chunks))
    units = [(ci, pp) for ci in chunk_order for pp in range(n_pairs)]
    each = lambda f, *lists: [f(*args) for args in zip(*lists)]

    cum = {}
    for ci in chunk_order:
        cs = _dot(tri, jnp.concatenate(_split3(lw[ci * CH:(ci + 1) * CH]), axis=1))
        cum[ci] = cs[:, :w] + cs[:, w:2 * w] + cs[:, 2 * w:]
    blk = lambda x: [x[ci * CH:(ci + 1) * CH, pp * LANES:(pp + 1) * LANES] for ci, pp in units]
    rx, vx, kap, lwx, kx, bet = blk(r), blk(v), blk(kk), blk(lw), blk(k_d), blk(beta)
    cumi = [cum[ci][:, pp * LANES:(pp + 1) * LANES] for ci, pp in units]
    tot = each(lambda c: c[0:1, :] if rev else c[CH - 1:CH, :], cumi)
    cc = each(lambda c: c - c[CH // 2:CH // 2 + 1, :], cumi)
    e_neg = each(lambda c: jnp.exp(-c), cc)
    e_end = each(lambda t, c: jnp.exp(t - c), tot, cumi)
    kap_t = each(lambda x, c, l: x * jnp.exp(c - l), kap, cc, lwx)
    r_t = each(lambda x, c: x * jnp.exp(c), rx, cc)
    kap_0 = each(lambda x, c, l: x * jnp.exp(c - l), kap, cumi, lwx)
    r_0 = each(lambda x, c: x * jnp.exp(c), rx, cumi)
    k_t = each(lambda x, e: x * e, kx, e_neg)
    bet_t = each(lambda x, e: x * e, bet, e_neg)
    k_e = each(lambda x, e: x * e, kx, e_end)
    bet_e = each(lambda x, e: x * e, bet, e_end)
    p_end = each(jnp.exp, tot)
    qs = each(lambda a, b: jnp.concatenate([a, b], axis=0).astype(BF16), kap_t, r_t)
    s_b = each(lambda q, x: _dot_nt(q, _head_rows(x)), qs, bet_t)
    s_k = each(lambda q, x: _dot_nt(q, _head_rows(x)), qs, k_t)
    a_b = each(lambda s: jnp.where(prec2, s[:CH], 0.0), s_b)
    a_k = each(lambda s: jnp.where(prec2, s[:CH], 0.0).astype(BF16), s_k)
    b_b = each(lambda s: jnp.where(incl2, s[CH:], 0.0).astype(BF16), s_b)
    b_k = each(lambda s: jnp.where(incl2, s[CH:], 0.0).astype(BF16), s_k)
    ts = each(lambda a: jnp.where(eye2, 1.0, jnp.where(join[0], -a, 0.0)), a_b)
    for lvl in range(1, len(join)):
        q_b = each(lambda a: _block_diag(jnp.where(join[lvl], a, 0.0).astype(BF16)), a_b)
        t_b = each(lambda t: t.astype(BF16), ts)
        dq = each(lambda t, q: _dot(t, q).astype(BF16), t_b, q_b)
        ts = each(lambda t, e, tb: t - _dot(e, _block_diag(tb)), ts, dq, t_b)
    t_b = each(lambda t: t.astype(BF16), ts)
    v_h = each(_head_rows, vx)
    av = each(_dot, a_k, v_h)
    rhs = each(lambda k0, a: _head_rows(jnp.concatenate([k0, a], axis=1)), kap_0, av)
    wz = each(_dot, t_b, rhs)
    kq = each(lambda x: x[:, :LANES], wz)
    z0 = each(lambda x: x[:, LANES:], wz)
    bz = each(lambda b, x: _dot(b, _head_rows(x)), b_b, wz)
    o0k = each(_dot, b_k, v_h)
    r_new = each(lambda x, b: x - b[:, :LANES], r_0, bz)
    o0 = each(lambda a, b: a - b[:, LANES:], o0k, bz)
    m_c = each(lambda p, b, q: jnp.where(eye, p, 0.0) - jnp.where(same_head, _dot_tn(b, q), 0.0), p_end, bet_e, kq)
    g_c = each(lambda ke, be, x, z: jnp.where(same_head, _dot_tn(jnp.concatenate([ke, be], axis=0),
                                                                  jnp.concatenate([x, -z], axis=0)), 0.0),
               k_e, bet_e, vx, z0)

    states = [s_ref[pp] for pp in range(n_pairs)]
    outs = {}
    lhs = each(lambda a, b: jnp.concatenate([a, b], axis=0).astype(BF16), r_new, m_c)
    for u, (ci, pp) in enumerate(units):
        st_hi, st_lo = _split2(states[pp])
        res = _dot(lhs[u], st_hi) + _dot(lhs[u], st_lo)
        outs[ci, pp] = res[:CH] + o0[u]
        states[pp] = res[CH:] + g_c[u]
    for ci in chunk_order:
        o_ref[0, ci * CH:(ci + 1) * CH, :] = jnp.concatenate([outs[ci, pp] for pp in range(n_pairs)],
                                                             axis=1).astype(o_ref.dtype)
    for pp in range(n_pairs):
        s_ref[pp] = states[pp]


def _rwkv_scan(pr, lp, nct, rev):
    nb, s, pw = pr.shape
    ng = s // TR
    w = RWKV_WIDTH
    hb = TR // SUBLANES
    d = int(rev)
    tile = functools.partial(_rwkv_tile, rev, nct=nct, ng=ng)
    n_row_blocks = s // SUBLANES
    main = pl.BlockSpec((1, TR, pw), lambda b, g: (b, tile(g), 0))
    prev = pl.BlockSpec((1, SUBLANES, pw), lambda b, g: (b, jnp.maximum(tile(g) * hb - 1, 0), 0))
    nxt = pl.BlockSpec((1, SUBLANES, pw), lambda b, g: (b, jnp.minimum((tile(g) + 1) * hb, n_row_blocks - 1), 0))
    vec = lambda n: pl.BlockSpec((1, n), lambda b, g: (0, 0))
    mat = pl.BlockSpec((2 * DECAY_LORA, w), lambda b, g: (0, 0))
    out = pl.BlockSpec((1, TR, w), lambda b, g: (b, tile(g), 0))
    zeros = jnp.zeros((DECAY_LORA, w), F32)
    pad = lambda m: jnp.concatenate([zeros, m] if rev else [m, zeros])
    return pl.pallas_call(
        functools.partial(_rwkv_dir_kernel, rev=rev, nct=nct, ng=ng),
        grid=(nb, ng),
        in_specs=[main, prev, nxt, vec(pw), vec(w), vec(w), vec(w), vec(w), vec(w), mat, mat],
        out_specs=[out, out],
        out_shape=[jax.ShapeDtypeStruct((nb, s, w), BF16), jax.ShapeDtypeStruct((nb, s, w), BF16)],
        scratch_shapes=[pltpu.VMEM((RWKV_HEADS // PAIR, LANES, LANES), F32)],
        compiler_params=_cparams(("arbitrary", "arbitrary")),
        name="rwkv_scan_bwd" if rev else "rwkv_scan_fwd",
    )(pr, pr, pr, lp['mu'].reshape(1, pw), lp['kk'].reshape(1, w), lp['ka'].reshape(1, w), lp['rk'].reshape(1, w),
      lp['w0'][d].reshape(1, w), lp['a0'][d].reshape(1, w), pad(lp['w2'][d]), pad(lp['a2'][d]))


def _rwkv_out_kernel(of_ref, ob_ref, bf_ref, bb_ref, gd_ref, gprev_ref, gnext_ref, mu_ref, g2_ref, gw_ref, gb_ref,
                     y_ref, *, nct, ng):
    tile = pl.program_id(1)
    o = of_ref[0].astype(F32) + ob_ref[0].astype(F32)
    cen = o - _head_sums(o, 1.0 / HEAD_DIM)
    var = _head_sums(cen * cen, 1.0 / HEAD_DIM)
    y = cen * lax.rsqrt(var + GN_EPS) * gw_ref[...] + gb_ref[...] + (bf_ref[0].astype(F32) + bb_ref[0].astype(F32))
    prev_row, next_row = _halo_rows(gprev_ref, gnext_ref, tile, nct, ng)
    gd = _token_shift(gd_ref[0], prev_row, next_row, mu_ref[...])
    gate = _dot(_sigmoid(gd).astype(BF16), g2_ref[...].astype(BF16))
    y_ref[0] = (y * gate).astype(BF16)


def _rwkv_out(o_f, o_b, bv_f, bv_b, pr, lp, nct):
    nb, s, w = o_f.shape
    ng = s // TR
    hb = TR // SUBLANES
    gcol = (SHIFT_WIDTH - GATE_LORA) // GATE_LORA
    n_row_blocks = s // SUBLANES
    row = pl.BlockSpec((1, TR, w), lambda b, t: (b, t, 0))
    vec = lambda n: pl.BlockSpec((1, n), lambda b, t: (0, 0))
    return pl.pallas_call(
        functools.partial(_rwkv_out_kernel, nct=nct, ng=ng),
        grid=(nb, ng),
        in_specs=[row, row, row, row,
                  pl.BlockSpec((1, TR, GATE_LORA), lambda b, t: (b, t, gcol)),
                  pl.BlockSpec((1, SUBLANES, GATE_LORA), lambda b, t: (b, jnp.maximum(t * hb - 1, 0), gcol)),
                  pl.BlockSpec((1, SUBLANES, GATE_LORA),
                               lambda b, t: (b, jnp.minimum((t + 1) * hb, n_row_blocks - 1), gcol)),
                  vec(GATE_LORA), pl.BlockSpec((GATE_LORA, w), lambda b, t: (0, 0)), vec(w), vec(w)],
        out_specs=row,
        out_shape=jax.ShapeDtypeStruct((nb, s, w), BF16),
        compiler_params=_cparams(("arbitrary", "arbitrary")),
        name="rwkv_out",
    )(o_f, o_b, bv_f, bv_b, pr, pr, pr, lp['mu'][SHIFT_WIDTH - GATE_LORA:].reshape(1, GATE_LORA), lp['g2'],
      lp['gn_w'].reshape(1, w), lp['gn_b'].reshape(1, w))


def _top2(logits):
    lane = lax.broadcasted_iota(jnp.int32, logits.shape, 1)
    v1 = jnp.max(logits, axis=-1, keepdims=True)
    i1 = jnp.min(jnp.where(logits == v1, lane, LANES), axis=-1, keepdims=True)
    rest = jnp.where(lane == i1, -jnp.inf, logits)
    v2 = jnp.max(rest, axis=-1, keepdims=True)
    i2 = jnp.min(jnp.where(rest == v2, lane, LANES), axis=-1, keepdims=True)
    e = jnp.exp(v2 - v1)
    g1 = 1.0 / (1.0 + e)
    g2 = e / (1.0 + e)
    out = jnp.where(lane == 0, i1.astype(F32), 0.0)
    out = jnp.where(lane == 1, i2.astype(F32), out)
    out = jnp.where(lane == 2, g1, out)
    return jnp.where(lane == 3, g2, out)


def _out_proj_kernel(x_ref, att_ref, rw_ref, mod_ref, g_ref, wa_ref, wr_ref, *rest, moe):
    if moe:
        wrt_ref, xo_ref, h_ref, route_ref = rest
    else:
        xo_ref, h_ref = rest
    mix = _dot(att_ref[0], wa_ref[...]) + _dot(rw_ref[0], wr_ref[...])
    x = x_ref[0] + mod_ref[0, 2:3] * mix
    xo_ref[0] = x
    h = _norm_mod(x, g_ref[...], mod_ref[0, 3:4], mod_ref[0, 4:5])
    h_ref[0] = h.astype(h_ref.dtype)
    if moe:
        lane = lax.broadcasted_iota(jnp.int32, (TR, LANES), 1)
        logits = jnp.where(lane < N_EXPERTS, _dot3(h, wrt_ref[...]), -jnp.inf)
        route_ref[0] = _top2(logits)


def _out_proj(xa, att, rw, mod, g, wo_a, wo_r, nct, w_router=None):
    nb, s, d = xa.shape
    moe = w_router is not None
    row = lambda n: pl.BlockSpec((1, TR, n), lambda b, t: (b, t, 0))
    full = lambda a: pl.BlockSpec(a.shape, lambda b, t: (0, 0))
    in_specs = [row(d), row(ATTN_WIDTH), row(RWKV_WIDTH), pl.BlockSpec((1, N_MOD, d), _mod_index(nb, nct)),
                pl.BlockSpec((1, d), lambda b, t: (0, 0)), full(wo_a), full(wo_r)]
    args = [xa, att, rw, mod, g.reshape(1, d), wo_a, wo_r]
    out_specs = [row(d), row(d)]
    out_shape = [jax.ShapeDtypeStruct((nb, s, d), F32), jax.ShapeDtypeStruct((nb, s, d), F32 if moe else BF16)]
    if moe:
        wrt = jnp.pad(w_router, ((0, 0), (0, LANES - N_EXPERTS)))
        in_specs.append(full(wrt))
        args.append(wrt)
        out_specs.append(row(LANES))
        out_shape.append(jax.ShapeDtypeStruct((nb, s, LANES), F32))
    return pl.pallas_call(
        functools.partial(_out_proj_kernel, moe=moe),
        grid=(nb, s // TR),
        in_specs=in_specs, out_specs=out_specs, out_shape=out_shape,
        compiler_params=_cparams(("arbitrary", "arbitrary")),
        name="out_proj_moe" if moe else "out_proj",
    )(*args)


def _ffn_kernel(x_ref, h_ref, mod_ref, wg_ref, wu_ref, wd_ref, o_ref):
    h = h_ref[0]
    gt = _dot(h, wg_ref[...])
    up = _dot(h, wu_ref[...])
    act = (gt * _sigmoid(gt) * up).astype(BF16)
    o_ref[0] = x_ref[0] + mod_ref[0, 5:6] * _dot(act, wd_ref[...])


def _ffn_dense(xa, h, mod, wg, wu, wd, nct):
    nb, s, d = xa.shape
    row = pl.BlockSpec((1, TR, d), lambda b, t: (b, t, 0))
    resident = lambda a: pl.BlockSpec(a.shape, lambda b, t: (0, 0), pipeline_mode=pl.Buffered(1))
    return pl.pallas_call(
        _ffn_kernel,
        grid=(nb, s // TR),
        in_specs=[row, row, pl.BlockSpec((1, N_MOD, d), _mod_index(nb, nct)), resident(wg), resident(wu),
                  resident(wd)],
        out_specs=row,
        out_shape=jax.ShapeDtypeStruct((nb, s, d), F32),
        compiler_params=_cparams(("arbitrary", "arbitrary")),
        name="ffn_dense",
    )(xa, h, mod, wg, wu, wd)


def _row_copy(src_hbm, idx_ref, dst_ref, sem, r):
    return pltpu.make_async_copy(src_hbm.at[pl.ds(idx_ref[0, 0, r], 1), :], dst_ref.at[pl.ds(r, 1), :], sem)


def _moe_kernel(be_ref, nu_ref, tok_ref, tok_next_ref, h_hbm, wg_ref, wu_ref, wd_ref, o_ref, x_buf, sem, *, n_tiles):
    i = pl.program_id(0)
    j = pl.program_id(1)
    nblk = pl.num_programs(0)
    used = i < nu_ref[0]
    slot = i % 2
    cur, nxt = x_buf.at[slot], x_buf.at[1 - slot]
    rows = cur.shape[0]

    def wait_rows(idx_ref, dst, s):
        def body(r, carry):
            _row_copy(h_hbm, idx_ref, dst, s, r).wait()
            return carry
        lax.fori_loop(0, rows, body, 0, unroll=8)

    @pl.when(j == 0)
    def _():
        o_ref[...] = jnp.zeros_like(o_ref)

        @pl.when(i == 0)
        def _():
            def body(r, carry):
                _row_copy(h_hbm, tok_ref, cur, sem.at[slot], r).start()
                return carry
            lax.fori_loop(0, rows, body, 0, unroll=8)

        @pl.when(jnp.logical_or(i == 0, i - 1 < nu_ref[0]))
        def _():
            wait_rows(tok_ref, cur, sem.at[slot])

    def compute():
        x = cur[...].astype(BF16)
        gt = _dot(x, wg_ref[0])
        up = _dot(x, wu_ref[0])
        act = (gt * _sigmoid(gt) * up).astype(BF16)
        o_ref[...] += _dot(act, wd_ref[0])

    share = rows // n_tiles
    for jj in range(n_tiles):
        @pl.when(jnp.logical_and(used, j == jj))
        def _(jj=jj):
            compute()
            for r in range(jj * share, (jj + 1) * share):
                _row_copy(h_hbm, tok_next_ref, nxt, sem.at[1 - slot], r).start()

            if jj == n_tiles - 1:
                @pl.when(i == nblk - 1)
                def _():
                    wait_rows(tok_next_ref, nxt, sem.at[1 - slot])


def _moe_experts(h, slot_token, block_expert, n_used, wg, wu, wd):
    t, d = h.shape
    dff = wg.shape[2]
    nblk = slot_token.shape[0] // MOE_BM
    tok = slot_token.reshape(nblk, 1, MOE_BM)
    grid_spec = pltpu.PrefetchScalarGridSpec(
        num_scalar_prefetch=2,
        grid=(nblk, dff // MOE_TF),
        in_specs=[pl.BlockSpec((1, 1, MOE_BM), lambda i, j, be, nu: (i, 0, 0), memory_space=pltpu.SMEM),
                  pl.BlockSpec((1, 1, MOE_BM), lambda i, j, be, nu: (jnp.minimum(i + 1, nblk - 1), 0, 0),
                               memory_space=pltpu.SMEM),
                  pl.BlockSpec(memory_space=pl.ANY),
                  pl.BlockSpec((1, d, MOE_TF), lambda i, j, be, nu: (be[i], 0, j)),
                  pl.BlockSpec((1, d, MOE_TF), lambda i, j, be, nu: (be[i], 0, j)),
                  pl.BlockSpec((1, MOE_TF, d), lambda i, j, be, nu: (be[i], j, 0))],
        out_specs=pl.BlockSpec((MOE_BM, d), lambda i, j, be, nu: (i, 0)),
        scratch_shapes=[pltpu.VMEM((2, MOE_BM, d), F32), pltpu.SemaphoreType.DMA((2,))],
    )
    return pl.pallas_call(
        functools.partial(_moe_kernel, n_tiles=dff // MOE_TF),
        grid_spec=grid_spec,
        out_shape=jax.ShapeDtypeStruct((nblk * MOE_BM, d), F32),
        compiler_params=_cparams(("arbitrary", "arbitrary")),
        name="moe_experts",
    )(block_expert, n_used, tok, tok, h, wg, wu, wd)


def _rows_loop(src_hbm, idx_ref, dst_ref, sem, op):
    def body(r, carry):
        getattr(_row_copy(src_hbm, idx_ref, dst_ref, sem, r), op)()
        return carry
    lax.fori_loop(0, dst_ref.shape[0], body, 0, unroll=8)


def _combine_kernel(s1_ref, s2_ref, n1_ref, n2_ref, x_ref, route_ref, mod_ref, ys_hbm, o_ref, y_buf, sem):
    n = pl.program_id(0) * pl.num_programs(1) + pl.program_id(1)
    last = pl.num_programs(0) * pl.num_programs(1) - 1
    slot = n % 2
    cur = [(s1_ref, y_buf.at[slot, 0], sem.at[slot, 0]), (s2_ref, y_buf.at[slot, 1], sem.at[slot, 1])]
    nxt = [(n1_ref, y_buf.at[1 - slot, 0], sem.at[1 - slot, 0]), (n2_ref, y_buf.at[1 - slot, 1], sem.at[1 - slot, 1])]

    @pl.when(n == 0)
    def _():
        for idx, dst, s in cur:
            _rows_loop(ys_hbm, idx, dst, s, "start")

    for idx, dst, s in cur:
        _rows_loop(ys_hbm, idx, dst, s, "wait")
    for idx, dst, s in nxt:
        _rows_loop(ys_hbm, idx, dst, s, "start")
    route = route_ref[0]
    f = y_buf[slot, 0] * route[:, 2:3] + y_buf[slot, 1] * route[:, 3:4]
    o_ref[0] = x_ref[0] + mod_ref[0, 5:6] * f

    @pl.when(n == last)
    def _():
        for idx, dst, s in nxt:
            _rows_loop(ys_hbm, idx, dst, s, "wait")


def _moe_combine(xa, route, mod, ys, slot1, slot2, nct):
    nb, s, d = xa.shape
    ng = s // TR
    idx = pl.BlockSpec((1, 1, TR), lambda b, t: (b * ng + t, 0, 0), memory_space=pltpu.SMEM)
    idx_next = pl.BlockSpec((1, 1, TR), lambda b, t: (jnp.minimum(b * ng + t + 1, nb * ng - 1), 0, 0),
                            memory_space=pltpu.SMEM)
    row = lambda n: pl.BlockSpec((1, TR, n), lambda b, t: (b, t, 0))
    s1, s2 = slot1.reshape(nb * ng, 1, TR), slot2.reshape(nb * ng, 1, TR)
    return pl.pallas_call(
        _combine_kernel,
        grid=(nb, ng),
        in_specs=[idx, idx, idx_next, idx_next, row(d), row(LANES), pl.BlockSpec((1, N_MOD, d), _mod_index(nb, nct)),
                  pl.BlockSpec(memory_space=pl.ANY)],
        out_specs=row(d),
        out_shape=jax.ShapeDtypeStruct((nb, s, d), F32),
        scratch_shapes=[pltpu.VMEM((2, 2, TR, d), F32), pltpu.SemaphoreType.DMA((2, 2))],
        compiler_params=_cparams(("arbitrary", "arbitrary")),
        name="moe_combine",
    )(s1, s2, s1, s2, xa, route, mod, ys)


def _moe_ffn(xa, h, route, mod, wg, wu, wd, nct):
    nb, s, d = xa.shape
    t = nb * s
    n_assign = 2 * t
    nblk = (n_assign + N_EXPERTS * (MOE_BM - 1) + MOE_BM - 1) // MOE_BM
    n_slots = nblk * MOE_BM
    expert = route.reshape(t, LANES)[:, 0:2].astype(jnp.int32).reshape(-1)
    onehot = (expert[:, None] == jnp.arange(N_EXPERTS, dtype=jnp.int32)[None, :]).astype(jnp.int32)
    rank = jnp.sum((jnp.cumsum(onehot, axis=0) - onehot) * onehot, axis=1)
    counts = jnp.sum(onehot, axis=0)
    padded = (counts + MOE_BM - 1) // MOE_BM * MOE_BM
    pad_end = jnp.cumsum(padded)
    pad_start = pad_end - padded
    slot = pad_start[expert] + rank
    token = jnp.repeat(jnp.arange(t, dtype=jnp.int32), 2)
    slot_token = jnp.zeros((n_slots,), jnp.int32).at[slot].set(token)
    block_start = jnp.arange(nblk, dtype=jnp.int32) * MOE_BM
    block_expert = jnp.minimum(jnp.sum((pad_end[None, :] <= block_start[:, None]).astype(jnp.int32), axis=1),
                               N_EXPERTS - 1)
    n_used = (pad_end[-1] // MOE_BM).astype(jnp.int32).reshape(1)
    ys = _moe_experts(h.reshape(t, d), slot_token, block_expert, n_used, wg, wu, wd)
    slot2 = slot.reshape(t, 2)
    return _moe_combine(xa, route, mod, ys, slot2[:, 0], slot2[:, 1], nct)


def kernel(x, c, ctx, c_ctx, ada_w, ada_b, norm1_g, norm2_g, w_in, w_out, q_gain, k_gain, shift_mu, rw_w0, rw_w2,
           rw_a0, rw_a2, rw_g2, rw_kk, rw_ka, rw_rk, rw_gn_w, rw_gn_b, ffn_wg, ffn_wu, ffn_wd, moe_router, moe_wg,
           moe_wu, moe_wd):
    nb, l, d = x.shape
    lc = ctx.shape[1]
    depth = ada_w.shape[0]
    assert lc % TR == 0 and l % TR == 0 and l % GRID_W == 0 and TR % ATTN_KT == 0 and TR % CH == 0
    nct = lc // TR
    xa = jnp.concatenate([ctx, x], axis=1)

    mod_rows = -(-(nb + 1) // SUBLANES) * SUBLANES
    c_all = jnp.concatenate([c, c_ctx[None], jnp.zeros((mod_rows - nb - 1, d), F32)], axis=0)
    mod_all = _ada_all(c_all, ada_w, ada_b)[:, :nb + 1].reshape(depth, nb + 1, N_MOD, d)
    cos, slo, shi = _rope_tables(lc, l)

    for i in range(depth):
        mod = mod_all[i]
        lp = dict(mu=shift_mu[i], w0=rw_w0[i], w2=rw_w2[i], a0=rw_a0[i], a2=rw_a2[i], g2=rw_g2[i], kk=rw_kk[i],
                  ka=rw_ka[i], rk=rw_rk[i], gn_w=rw_gn_w[i], gn_b=rw_gn_b[i])
        wi = w_in[i].astype(BF16)
        pa, pr = _in_proj(xa, mod, norm1_g[i], wi[:, :ATTN_IN], wi[:, ATTN_IN:], nct)
        q, k, v = _attn_prep(pa, cos, slo, shi, q_gain[i], k_gain[i])
        att = _attention(q, k, v, nct, lc)
        o_f, bv_f = _rwkv_scan(pr, lp, nct, False)
        o_b, bv_b = _rwkv_scan(pr, lp, nct, True)
        rw = _rwkv_out(o_f, o_b, bv_f, bv_b, pr, lp, nct)
        wo = w_out[i].astype(BF16)
        j = i // 2
        if i % 2 == 0:
            xa, h = _out_proj(xa, att, rw, mod, norm2_g[i], wo[:ATTN_WIDTH], wo[ATTN_WIDTH:], nct)
            xa = _ffn_dense(xa, h, mod, ffn_wg[j].astype(BF16), ffn_wu[j].astype(BF16), ffn_wd[j].astype(BF16), nct)
        else:
            xa, h, route = _out_proj(xa, att, rw, mod, norm2_g[i], wo[:ATTN_WIDTH], wo[ATTN_WIDTH:], nct,
                                     moe_router[j])
            xa = _moe_ffn(xa, h, route, mod, moe_wg[j].astype(BF16), moe_wu[j].astype(BF16), moe_wd[j].astype(BF16),
                          nct)
    return xa[:, lc:]
```

```python
import functools
import math

import jax
import jax.numpy as jnp
from jax import lax
from jax.experimental import pallas as pl
from jax.experimental.pallas import tpu as pltpu

F32 = jnp.float32
BF16 = jnp.bfloat16
HI = lax.Precision.HIGHEST

HEAD_DIM = 64
ROPE_HALF = HEAD_DIM // 2
ROPE_THETA = 10000.0
GRID_W = 64
ATTN_HEADS = 8
ATTN_KV_HEADS = 2
ATTN_GROUP = ATTN_HEADS // ATTN_KV_HEADS
ATTN_WIDTH = ATTN_HEADS * HEAD_DIM
KV_WIDTH = ATTN_KV_HEADS * HEAD_DIM
ATTN_IN = ATTN_WIDTH + 2 * KV_WIDTH
ATTN_SCALE = HEAD_DIM ** -0.5
LOG2E = math.log2(math.e)
RWKV_HEADS = 8
RWKV_WIDTH = RWKV_HEADS * HEAD_DIM
DECAY_LORA = 64
ICLR_LORA = 64
GATE_LORA = 128
SHIFT_WIDTH = 3 * RWKV_WIDTH + 2 * DECAY_LORA + 2 * ICLR_LORA + GATE_LORA
N_EXPERTS = 8
N_MOD = 6
EPS = 1e-6
GN_EPS = 64e-5

LANES = 128
SUBLANES = 8
TR = 256
CH = 128
PAIR = LANES // HEAD_DIM
MXU_DIM = 256
ATTN_KT = 256
MOE_BM = 512
MOE_TF = 1792
VMEM_LIMIT = 56 * 1024 * 1024


def _cparams(sem):
    return pltpu.CompilerParams(dimension_semantics=sem, vmem_limit_bytes=VMEM_LIMIT)


def _dot(a, b, prec=None):
    return jnp.dot(a, b, preferred_element_type=F32, precision=prec)


def _dot_nt(a, b, prec=None):
    return lax.dot_general(a, b, (((1,), (1,)), ((), ())), preferred_element_type=F32, precision=prec)


def _dot_tn(a, b, prec=None):
    return lax.dot_general(a, b, (((0,), (0,)), ((), ())), preferred_element_type=F32, precision=prec)


def _sigmoid(x):
    return 1.0 / (1.0 + jnp.exp(-x))


def _split2(x):
    hi = x.astype(BF16)
    return hi, (x - hi.astype(F32)).astype(BF16)


def _split3(x):
    hi = x.astype(BF16)
    r = x - hi.astype(F32)
    mid = r.astype(BF16)
    return hi, mid, (r - mid.astype(F32)).astype(BF16)


def _dot3(a, b):
    a_hi, a_lo = _split2(a)
    b_hi, b_lo = _split2(b)
    return _dot(a_hi, b_hi) + (_dot(a_hi, b_lo) + _dot(a_lo, b_hi))


def _head_sums(x, scale=1.0):
    r, n = x.shape
    i = lax.broadcasted_iota(jnp.int32, (MXU_DIM, MXU_DIM), 0) // HEAD_DIM
    j = lax.broadcasted_iota(jnp.int32, (MXU_DIM, MXU_DIM), 1) // HEAD_DIM
    ones = jnp.where(i == j, scale, 0.0).astype(BF16)
    nfull = n // MXU_DIM
    parts = []
    if nfull:
        xs = jnp.concatenate([x[:, c * MXU_DIM:(c + 1) * MXU_DIM] for c in range(nfull)], axis=0)
        m = nfull * r
        s = _dot(jnp.concatenate(_split2(xs), axis=0), ones)
        s = s[:m] + s[m:]
        parts += [s[c * r:(c + 1) * r] for c in range(nfull)]
    if n % MXU_DIM:
        w = n % MXU_DIM
        s = _dot(jnp.concatenate(_split2(x[:, nfull * MXU_DIM:]), axis=0), ones[:w, :w])
        parts.append(s[:r] + s[r:])
    return jnp.concatenate(parts, axis=1)


def _ada_kernel(c_ref, w_ref, b_ref, o_ref):
    c = c_ref[...]
    sc = c * _sigmoid(c)
    o_ref[0] = _dot(sc, w_ref[0], HI) + b_ref[0]


def _ada_all(c_all, ada_w, ada_b):
    depth, d, n = ada_w.shape
    rows = c_all.shape[0]
    tn = 1536
    return pl.pallas_call(
        _ada_kernel,
        grid=(depth, n // tn),
        in_specs=[pl.BlockSpec((rows, d), lambda i, j: (0, 0)),
                  pl.BlockSpec((1, d, tn), lambda i, j: (i, 0, j)),
                  pl.BlockSpec((1, 1, tn), lambda i, j: (i, 0, j))],
        out_specs=pl.BlockSpec((1, rows, tn), lambda i, j: (i, 0, j)),
        out_shape=jax.ShapeDtypeStruct((depth, rows, n), F32),
        compiler_params=_cparams(("arbitrary", "arbitrary")),
        name="ada_mod",
    )(c_all, ada_w, ada_b.reshape(depth, 1, n))


def _norm_mod(x, g, shift, scale):
    y = x * lax.rsqrt(jnp.mean(x * x, axis=-1, keepdims=True) + EPS)
    return (y * g) * (1.0 + scale) + shift


def _in_proj_kernel(x_ref, mod_ref, g_ref, wa_ref, wr_ref, pa_ref, pr_ref):
    h = _norm_mod(x_ref[0], g_ref[...], mod_ref[0, 0:1], mod_ref[0, 1:2]).astype(BF16)
    pa_ref[0] = _dot(h, wa_ref[...])
    pr_ref[0] = _dot(h, wr_ref[...])


def _mod_index(nb, nct):
    return lambda b, t: (jnp.where(t < nct, nb, b), 0, 0)


def _in_proj(xa, mod, g, wa, wr, nct):
    nb, s, d = xa.shape
    na, nr = wa.shape[1], wr.shape[1]
    return pl.pallas_call(
        _in_proj_kernel,
        grid=(nb, s // TR),
        in_specs=[pl.BlockSpec((1, TR, d), lambda b, t: (b, t, 0)),
                  pl.BlockSpec((1, N_MOD, d), _mod_index(nb, nct)),
                  pl.BlockSpec((1, d), lambda b, t: (0, 0)),
                  pl.BlockSpec((d, na), lambda b, t: (0, 0)),
                  pl.BlockSpec((d, nr), lambda b, t: (0, 0))],
        out_specs=[pl.BlockSpec((1, TR, na), lambda b, t: (b, t, 0)),
                   pl.BlockSpec((1, TR, nr), lambda b, t: (b, t, 0))],
        out_shape=[jax.ShapeDtypeStruct((nb, s, na), F32),
                   jax.ShapeDtypeStruct((nb, s, nr), F32)],
        compiler_params=_cparams(("arbitrary", "arbitrary")),
        name="in_proj",
    )(xa, mod, g.reshape(1, d), wa, wr)


def _rope(x, cos, sin_lo, sin_hi):
    n = x.shape[-1]
    return x * cos + pltpu.roll(x, n - ROPE_HALF, 1) * sin_lo + pltpu.roll(x, ROPE_HALF, 1) * sin_hi


def _attn_prep_kernel(pa_ref, cos_ref, slo_ref, shi_ref, qg_ref, kg_ref, q_ref, k_ref, v_ref):
    pa = pa_ref[0]
    qk = pa[:, :ATTN_WIDTH + KV_WIDTH]
    v = pa[:, ATTN_WIDTH + KV_WIDTH:ATTN_IN]
    cos, slo, shi = cos_ref[...], slo_ref[...], shi_ref[...]
    inv = lax.rsqrt(_head_sums(qk * qk, 1.0 / HEAD_DIM) + EPS)
    qn = qk[:, :ATTN_WIDTH] * inv[:, :ATTN_WIDTH] * qg_ref[...]
    qr = _rope(qn, cos, slo, shi) * (ATTN_SCALE * LOG2E)
    kn = qk[:, ATTN_WIDTH:] * inv[:, ATTN_WIDTH:] * kg_ref[...]
    kr = _rope(kn, cos[:, :KV_WIDTH], slo[:, :KV_WIDTH], shi[:, :KV_WIDTH])
    lane = lax.broadcasted_iota(jnp.int32, (TR, LANES), 1)
    low = lane < HEAD_DIM
    for j in range(ATTN_KV_HEADS):
        kj = kr if j % PAIR == 0 else pltpu.roll(kr, HEAD_DIM, 1)
        vj = v if j % PAIR == 0 else pltpu.roll(v, HEAD_DIM, 1)
        k_ref[0, j] = jnp.where(low, kj, 0.0).astype(BF16)
        v_ref[0, j] = jnp.transpose(jnp.where(low, vj, jnp.where(lane == HEAD_DIM, 1.0, 0.0))).astype(BF16)
    for h in range(ATTN_HEADS):
        src = qr[:, (h // PAIR) * LANES:(h // PAIR + 1) * LANES]
        if h % PAIR:
            src = pltpu.roll(src, HEAD_DIM, 1)
        q_ref[0, h] = jnp.where(low, src, 0.0).astype(BF16)


def _attn_prep(pa, cos, slo, shi, q_gain, k_gain):
    nb, s, _ = pa.shape
    qg = jnp.tile(q_gain, ATTN_HEADS).reshape(1, ATTN_WIDTH)
    kg = jnp.tile(k_gain, ATTN_KV_HEADS).reshape(1, KV_WIDTH)
    tab = pl.BlockSpec((TR, ATTN_WIDTH), lambda t, b: (t, 0))
    return pl.pallas_call(
        _attn_prep_kernel,
        grid=(s // TR, nb),
        in_specs=[pl.BlockSpec((1, TR, ATTN_IN), lambda t, b: (b, t, 0)), tab, tab, tab,
                  pl.BlockSpec((1, ATTN_WIDTH), lambda t, b: (0, 0)),
                  pl.BlockSpec((1, KV_WIDTH), lambda t, b: (0, 0))],
        out_specs=[pl.BlockSpec((1, ATTN_HEADS, TR, LANES), lambda t, b: (b, 0, t, 0)),
                   pl.BlockSpec((1, ATTN_KV_HEADS, TR, LANES), lambda t, b: (b, 0, t, 0)),
                   pl.BlockSpec((1, ATTN_KV_HEADS, LANES, TR), lambda t, b: (b, 0, 0, t))],
        out_shape=[jax.ShapeDtypeStruct((nb, ATTN_HEADS, s, LANES), BF16),
                   jax.ShapeDtypeStruct((nb, ATTN_KV_HEADS, s, LANES), BF16),
                   jax.ShapeDtypeStruct((nb, ATTN_KV_HEADS, LANES, s), BF16)],
        compiler_params=_cparams(("arbitrary", "arbitrary")),
        name="attn_prep",
    )(pa, cos, slo, shi, qg, kg)


def _rope_tables(lc, l):
    rows = l // GRID_W
    row = jnp.repeat(jnp.arange(rows, dtype=F32), GRID_W)
    col = jnp.tile(jnp.arange(GRID_W, dtype=F32), rows)
    inv = ROPE_THETA ** (-jnp.arange(0, ROPE_HALF, 2, dtype=F32) / ROPE_HALF)
    ang = jnp.concatenate([row[:, None] * inv, col[:, None] * inv], axis=-1)
    cos = jnp.concatenate([jnp.ones((lc, ROPE_HALF), F32), jnp.cos(ang)], axis=0)
    sin = jnp.concatenate([jnp.zeros((lc, ROPE_HALF), F32), jnp.sin(ang)], axis=0)
    zero = jnp.zeros_like(sin)
    head = lambda lo, hi: jnp.tile(jnp.concatenate([lo, hi], axis=-1), (1, ATTN_HEADS))
    return head(cos, cos), head(-sin, zero), head(zero, sin)


def _attn_kernel(q_ref, k_ref, vt_ref, o_ref, s_ref, *, nct, lc):
    t = pl.program_id(2)
    n_all = k_ref.shape[2]
    low = lax.broadcasted_iota(jnp.int32, (TR, LANES), 1) < HEAD_DIM

    def run(n_keys):
        tiles = [slice(j * ATTN_KT, (j + 1) * ATTN_KT) for j in range(n_keys // ATTN_KT)]
        mx, acc = [None] * ATTN_GROUP, [None] * ATTN_GROUP
        m = [None] * ATTN_GROUP
        for stage in range(ATTN_GROUP + 1):
            g1, g2 = stage, stage - 1
            if g1 < ATTN_GROUP:
                mx[g1] = jnp.full((ATTN_KT, TR), -jnp.inf, F32)
            if g2 >= 0:
                m[g2] = jnp.max(mx[g2], axis=0, keepdims=True)
                acc[g2] = jnp.zeros((LANES, TR), F32)
            for ks in tiles:
                if g1 < ATTN_GROUP:
                    st = _dot_nt(k_ref[0, 0, ks, :], q_ref[0, g1])
                    s_ref[g1, ks, :] = st
                    mx[g1] = jnp.maximum(mx[g1], st)
                if g2 >= 0:
                    pt = jnp.exp2(s_ref[g2, ks, :] - m[g2]).astype(BF16)
                    acc[g2] = acc[g2] + _dot(vt_ref[0, 0, :, ks], pt)
        out = [jnp.transpose(a / a[HEAD_DIM:HEAD_DIM + 1, :]) for a in acc]
        for pr in range(ATTN_GROUP // PAIR):
            hi = pltpu.roll(out[PAIR * pr + 1], HEAD_DIM, 1)
            o_ref[0, :, pr * LANES:(pr + 1) * LANES] = jnp.where(low, out[PAIR * pr], hi).astype(BF16)

    @pl.when(t < nct)
    def _():
        run(lc)

    @pl.when(t >= nct)
    def _():
        run(n_all)


def _attention(q, k, v, nct, lc):
    nb, _, s, _ = q.shape
    gw = ATTN_GROUP * HEAD_DIM
    k_spec = pl.BlockSpec((1, 1, s, LANES), lambda b, j, t: (b, j, 0, 0))
    vt_spec = pl.BlockSpec((1, 1, LANES, s), lambda b, j, t: (b, j, 0, 0))
    return pl.pallas_call(
        functools.partial(_attn_kernel, nct=nct, lc=lc),
        grid=(nb, ATTN_KV_HEADS, s // TR),
        in_specs=[pl.BlockSpec((1, ATTN_GROUP, TR, LANES), lambda b, j, t: (b, j, t, 0)), k_spec, vt_spec],
        out_specs=pl.BlockSpec((1, TR, gw), lambda b, j, t: (b, t, j)),
        out_shape=jax.ShapeDtypeStruct((nb, s, ATTN_WIDTH), BF16),
        scratch_shapes=[pltpu.VMEM((ATTN_GROUP, s, TR), F32)],
        compiler_params=_cparams(("arbitrary", "arbitrary", "arbitrary")),
        name="attention",
    )(q, k, v)


def _token_shift(p, prev_row, next_row, mu):
    n = p.shape[0]
    row = lax.broadcasted_iota(jnp.int32, p.shape, 0)
    prev = jnp.where(row == 0, prev_row, pltpu.roll(p, 1, 0))
    nxt = jnp.where(row == n - 1, next_row, pltpu.roll(p, n - 1, 0))
    return p + mu * (0.5 * (prev + nxt) - p)


def _halo_rows(prev_ref, next_ref, tile, nct, ng):
    first = jnp.logical_or(tile == 0, tile == nct)
    last = jnp.logical_or(tile == nct - 1, tile == ng - 1)
    prev_row = jnp.where(first, 0.0, prev_ref[0, SUBLANES - 1:SUBLANES, :])
    next_row = jnp.where(last, 0.0, next_ref[0, 0:1, :])
    return prev_row, next_row


def _rwkv_tile(rev, g, nct, ng):
    if not rev:
        return g
    return jnp.where(g < nct, nct - 1 - g, ng - 1 - (g - nct))


def _block_diag(x):
    z = jnp.zeros((CH, CH), x.dtype)
    return jnp.concatenate([jnp.concatenate([x[:, :CH], z], axis=1), jnp.concatenate([z, x[:, CH:]], axis=1)], axis=0)


def _head_rows(x):
    h0 = (lax.broadcasted_iota(jnp.int32, x.shape, 1) & (LANES - 1)) < HEAD_DIM
    return jnp.concatenate([jnp.where(h0, x, 0.0), jnp.where(h0, 0.0, x)], axis=0).astype(BF16)


def _rwkv_dir_kernel(p_ref, prev_ref, next_ref, mu_ref, kkw_ref, ka_ref, rk_ref, w0_ref, a0_ref, w2_ref, a2_ref,
                     o_ref, bv_ref, s_ref, *, rev, nct, ng):
    g = pl.program_id(1)
    tile = _rwkv_tile(rev, g, nct, ng)
    w = RWKV_WIDTH

    @pl.when(g == 0)
    def _():
        s_ref[...] = jnp.zeros_like(s_ref)

    prev_row, next_row = _halo_rows(prev_ref, next_ref, tile, nct, ng)
    ps = _token_shift(p_ref[0], prev_row, next_row, mu_ref[...])
    r, k, v = ps[:, :w], ps[:, w:2 * w], ps[:, 2 * w:3 * w]
    wd = ps[:, 3 * w:3 * w + 2 * DECAY_LORA]
    ad = ps[:, 3 * w + 2 * DECAY_LORA:3 * w + 2 * DECAY_LORA + 2 * ICLR_LORA]
    kk = k * kkw_ref[...]
    kk = kk * lax.rsqrt(_head_sums(kk * kk) + 1e-12)
    x = w0_ref[...] + _dot3(jnp.tanh(wd), w2_ref[...])
    lw = (-math.exp(-0.5)) * _sigmoid(x)
    a = _sigmoid(a0_ref[...] + _dot(ad.astype(BF16), a2_ref[...].astype(BF16)))
    k_d = k * (1.0 + (a - 1.0) * ka_ref[...])
    bv_ref[0] = (_head_sums(r * k_d * rk_ref[...]) * v).astype(bv_ref.dtype)
    beta = kk * a

    ti = lax.broadcasted_iota(jnp.int32, (CH, PAIR * CH), 0)
    tj = lax.broadcasted_iota(jnp.int32, (CH, PAIR * CH), 1) & (CH - 1)
    prec2 = (tj > ti) if rev else (tj < ti)
    eye2 = ti == tj
    incl2 = jnp.logical_or(prec2, eye2)
    late, early = (tj, ti) if rev else (ti, tj)
    join = [jnp.logical_and(jnp.logical_and((ti ^ tj) < 2 * b, (late & b) != 0), (early & b) == 0)
            for b in (2 ** m for m in range(int(math.log2(CH))))]
    si = lax.broadcasted_iota(jnp.int32, (CH, CH), 0)
    sj = lax.broadcasted_iota(jnp.int32, (CH, CH), 1)
    eye = si == sj
    tri = jnp.where((sj >= si) if rev else (sj <= si), 1.0, 0.0).astype(BF16)
    same_head = (si // HEAD_DIM) == (sj // HEAD_DIM)
    n_chunks = TR // CH
    n_pairs = RWKV_HEADS // PAIR
    chunk_order = list(range(n_chunks - 1, -1, -1) if rev else range(n_chunks))
    units = [(ci, pp) for ci in chunk_order for pp in range(n_pairs)]
    each = lambda f, *lists: [f(*args) for args in zip(*lists)]

    cum = {}
    for ci in chunk_order:
        cs = _dot(tri, jnp.concatenate(_split3(lw[ci * CH:(ci + 1) * CH]), axis=1))
        cum[ci] = cs[:, :w] + cs[:, w:2 * w] + cs[:, 2 * w:]
    blk = lambda x: [x[ci * CH:(ci + 1) * CH, pp * LANES:(pp + 1) * LANES] for ci, pp in units]
    rx, vx, kap, lwx, kx, bet = blk(r), blk(v), blk(kk), blk(lw), blk(k_d), blk(beta)
    cumi = [cum[ci][:, pp * LANES:(pp + 1) * LANES] for ci, pp in units]
    tot = each(lambda c: c[0:1, :] if rev else c[CH - 1:CH, :], cumi)
    cc = each(lambda c: c - c[CH // 2:CH // 2 + 1, :], cumi)
    e_neg = each(lambda c: jnp.exp(-c), cc)
    e_end = each(lambda t, c: jnp.exp(t - c), tot, cumi)
    kap_t = each(lambda x, c, l: x * jnp.exp(c - l), kap, cc, lwx)
    r_t = each(lambda x, c: x * jnp.exp(c), rx, cc)
    kap_0 = each(lambda x, c, l: x * jnp.exp(c - l), kap, cumi, lwx)
    r_0 = each(lambda x, c: x * jnp.exp(c), rx, cumi)
    k_t = each(lambda x, e: x * e, kx, e_neg)
    bet_t = each(lambda x, e: x * e, bet, e_neg)
    k_e = each(lambda x, e: x * e, kx, e_end)
    bet_e = each(lambda x, e: x * e, bet, e_end)
    p_end = each(jnp.exp, tot)
    qs = each(lambda a, b: jnp.concatenate([a, b], axis=0).astype(BF16), kap_t, r_t)
    s_b = each(lambda q, x: _dot_nt(q, _head_rows(x)), qs, bet_t)
    s_k = each(lambda q, x: _dot_nt(q, _head_rows(x)), qs, k_t)
    a_b = each(lambda s: jnp.where(prec2, s[:CH], 0.0), s_b)
    a_k = each(lambda s: jnp.where(prec2, s[:CH], 0.0).astype(BF16), s_k)
    b_b = each(lambda s: jnp.where(incl2, s[CH:], 0.0).astype(BF16), s_b)
    b_k = each(lambda s: jnp.where(incl2, s[CH:], 0.0).astype(BF16), s_k)
    ts = each(lambda a: jnp.where(eye2, 1.0, jnp.where(join[0], -a, 0.0)), a_b)
    for lvl in range(1, len(join)):
        q_b = each(lambda a: _block_diag(jnp.where(join[lvl], a, 0.0).astype(BF16)), a_b)
        t_b = each(lambda t: t.astype(BF16), ts)
        dq = each(lambda t, q: _dot(t, q).astype(BF16), t_b, q_b)
        ts = each(lambda t, e, tb: t - _dot(e, _block_diag(tb)), ts, dq, t_b)
    t_b = each(lambda t: t.astype(BF16), ts)
    v_h = each(_head_rows, vx)
    av = each(_dot, a_k, v_h)
    rhs = each(lambda k0, a: _head_rows(jnp.concatenate([k0, a], axis=1)), kap_0, av)
    wz = each(_dot, t_b, rhs)
    kq = each(lambda x: x[:, :LANES], wz)
    z0 = each(lambda x: x[:, LANES:], wz)
    bz = each(lambda b, x: _dot(b, _head_rows(x)), b_b, wz)
    o0k = each(_dot, b_k, v_h)
    r_new = each(lambda x, b: x - b[:, :LANES], r_0, bz)
    o0 = each(lambda a, b: a - b[:, LANES:], o0k, bz)
    m_c = each(lambda p, b, q: jnp.where(eye, p, 0.0) - jnp.where(same_head, _dot_tn(b, q), 0.0), p_end, bet_e, kq)
    g_c = each(lambda ke, be, x, z: jnp.where(same_head, _dot_tn(jnp.concatenate([ke, be], axis=0),
                                                                  jnp.concatenate([x, -z], axis=0)), 0.0),
               k_e, bet_e, vx, z0)

    states = [s_ref[pp] for pp in range(n_pairs)]
    outs = {}
    lhs = each(lambda a, b: jnp.concatenate([a, b], axis=0).astype(BF16), r_new, m_c)
    for u, (ci, pp) in enumerate(units):
        st_hi, st_lo = _split2(states[pp])
        res = _dot(lhs[u], st_hi) + _dot(lhs[u], st_lo)
        outs[ci, pp] = res[:CH] + o0[u]
        states[pp] = res[CH:] + g_c[u]
    for ci in chunk_order:
        o_ref[0, ci * CH:(ci + 1) * CH, :] = jnp.concatenate([outs[ci, pp] for pp in range(n_pairs)],
                                                             axis=1).astype(o_ref.dtype)
    for pp in range(n_pairs):
        s_ref[pp] = states[pp]


def _rwkv_scan(pr, lp, nct, rev):
    nb, s, pw = pr.shape
    ng = s // TR
    w = RWKV_WIDTH
    hb = TR // SUBLANES
    d = int(rev)
    tile = functools.partial(_rwkv_tile, rev, nct=nct, ng=ng)
    n_row_blocks = s // SUBLANES
    main = pl.BlockSpec((1, TR, pw), lambda b, g: (b, tile(g), 0))
    prev = pl.BlockSpec((1, SUBLANES, pw), lambda b, g: (b, jnp.maximum(tile(g) * hb - 1, 0), 0))
    nxt = pl.BlockSpec((1, SUBLANES, pw), lambda b, g: (b, jnp.minimum((tile(g) + 1) * hb, n_row_blocks - 1), 0))
    vec = lambda n: pl.BlockSpec((1, n), lambda b, g: (0, 0))
    mat = pl.BlockSpec((2 * DECAY_LORA, w), lambda b, g: (0, 0))
    out = pl.BlockSpec((1, TR, w), lambda b, g: (b, tile(g), 0))
    zeros = jnp.zeros((DECAY_LORA, w), F32)
    pad = lambda m: jnp.concatenate([zeros, m] if rev else [m, zeros])
    return pl.pallas_call(
        functools.partial(_rwkv_dir_kernel, rev=rev, nct=nct, ng=ng),
        grid=(nb, ng),
        in_specs=[main, prev, nxt, vec(pw), vec(w), vec(w), vec(w), vec(w), vec(w), mat, mat],
        out_specs=[out, out],
        out_shape=[jax.ShapeDtypeStruct((nb, s, w), BF16), jax.ShapeDtypeStruct((nb, s, w), BF16)],
        scratch_shapes=[pltpu.VMEM((RWKV_HEADS // PAIR, LANES, LANES), F32)],
        compiler_params=_cparams(("arbitrary", "arbitrary")),
        name="rwkv_scan_bwd" if rev else "rwkv_scan_fwd",
    )(pr, pr, pr, lp['mu'].reshape(1, pw), lp['kk'].reshape(1, w), lp['ka'].reshape(1, w), lp['rk'].reshape(1, w),
      lp['w0'][d].reshape(1, w), lp['a0'][d].reshape(1, w), pad(lp['w2'][d]), pad(lp['a2'][d]))


def _rwkv_out_kernel(of_ref, ob_ref, bf_ref, bb_ref, gd_ref, gprev_ref, gnext_ref, mu_ref, g2_ref, gw_ref, gb_ref,
                     y_ref, *, nct, ng):
    tile = pl.program_id(1)
    o = of_ref[0].astype(F32) + ob_ref[0].astype(F32)
    cen = o - _head_sums(o, 1.0 / HEAD_DIM)
    var = _head_sums(cen * cen, 1.0 / HEAD_DIM)
    y = cen * lax.rsqrt(var + GN_EPS) * gw_ref[...] + gb_ref[...] + (bf_ref[0].astype(F32) + bb_ref[0].astype(F32))
    prev_row, next_row = _halo_rows(gprev_ref, gnext_ref, tile, nct, ng)
    gd = _token_shift(gd_ref[0], prev_row, next_row, mu_ref[...])
    gate = _dot(_sigmoid(gd).astype(BF16), g2_ref[...].astype(BF16))
    y_ref[0] = (y * gate).astype(BF16)


def _rwkv_out(o_f, o_b, bv_f, bv_b, pr, lp, nct):
    nb, s, w = o_f.shape
    ng = s // TR
    hb = TR // SUBLANES
    gcol = (SHIFT_WIDTH - GATE_LORA) // GATE_LORA
    n_row_blocks = s // SUBLANES
    row = pl.BlockSpec((1, TR, w), lambda b, t: (b, t, 0))
    vec = lambda n: pl.BlockSpec((1, n), lambda b, t: (0, 0))
    return pl.pallas_call(
        functools.partial(_rwkv_out_kernel, nct=nct, ng=ng),
        grid=(nb, ng),
        in_specs=[row, row, row, row,
                  pl.BlockSpec((1, TR, GATE_LORA), lambda b, t: (b, t, gcol)),
                  pl.BlockSpec((1, SUBLANES, GATE_LORA), lambda b, t: (b, jnp.maximum(t * hb - 1, 0), gcol)),
                  pl.BlockSpec((1, SUBLANES, GATE_LORA),
                               lambda b, t: (b, jnp.minimum((t + 1) * hb, n_row_blocks - 1), gcol)),
                  vec(GATE_LORA), pl.BlockSpec((GATE_LORA, w), lambda b, t: (0, 0)), vec(w), vec(w)],
        out_specs=row,
        out_shape=jax.ShapeDtypeStruct((nb, s, w), BF16),
        compiler_params=_cparams(("arbitrary", "arbitrary")),
        name="rwkv_out",
    )(o_f, o_b, bv_f, bv_b, pr, pr, pr, lp['mu'][SHIFT_WIDTH - GATE_LORA:].reshape(1, GATE_LORA), lp['g2'],
      lp['gn_w'].reshape(1, w), lp['gn_b'].reshape(1, w))


def _top2(logits):
    lane = lax.broadcasted_iota(jnp.int32, logits.shape, 1)
    v1 = jnp.max(logits, axis=-1, keepdims=True)
    i1 = jnp.min(jnp.where(logits == v1, lane, LANES), axis=-1, keepdims=True)
    rest = jnp.where(lane == i1, -jnp.inf, logits)
    v2 = jnp.max(rest, axis=-1, keepdims=True)
    i2 = jnp.min(jnp.where(rest == v2, lane, LANES), axis=-1, keepdims=True)
    e = jnp.exp(v2 - v1)
    g1 = 1.0 / (1.0 + e)
    g2 = e / (1.0 + e)
    out = jnp.where(lane == 0, i1.astype(F32), 0.0)
    out = jnp.where(lane == 1, i2.astype(F32), out)
    out = jnp.where(lane == 2, g1, out)
    return jnp.where(lane == 3, g2, out)


def _out_proj_kernel(x_ref, att_ref, rw_ref, mod_ref, g_ref, wa_ref, wr_ref, *rest, moe):
    if moe:
        wrt_ref, xo_ref, h_ref, route_ref = rest
    else:
        xo_ref, h_ref = rest
    mix = _dot(att_ref[0], wa_ref[...]) + _dot(rw_ref[0], wr_ref[...])
    x = x_ref[0] + mod_ref[0, 2:3] * mix
    xo_ref[0] = x
    h = _norm_mod(x, g_ref[...], mod_ref[0, 3:4], mod_ref[0, 4:5])
    h_ref[0] = h.astype(h_ref.dtype)
    if moe:
        lane = lax.broadcasted_iota(jnp.int32, (TR, LANES), 1)
        logits = jnp.where(lane < N_EXPERTS, _dot3(h, wrt_ref[...]), -jnp.inf)
        route_ref[0] = _top2(logits)


def _out_proj(xa, att, rw, mod, g, wo_a, wo_r, nct, w_router=None):
    nb, s, d = xa.shape
    moe = w_router is not None
    row = lambda n: pl.BlockSpec((1, TR, n), lambda b, t: (b, t, 0))
    full = lambda a: pl.BlockSpec(a.shape, lambda b, t: (0, 0))
    in_specs = [row(d), row(ATTN_WIDTH), row(RWKV_WIDTH), pl.BlockSpec((1, N_MOD, d), _mod_index(nb, nct)),
                pl.BlockSpec((1, d), lambda b, t: (0, 0)), full(wo_a), full(wo_r)]
    args = [xa, att, rw, mod, g.reshape(1, d), wo_a, wo_r]
    out_specs = [row(d), row(d)]
    out_shape = [jax.ShapeDtypeStruct((nb, s, d), F32), jax.ShapeDtypeStruct((nb, s, d), F32 if moe else BF16)]
    if moe:
        wrt = jnp.pad(w_router, ((0, 0), (0, LANES - N_EXPERTS)))
        in_specs.append(full(wrt))
        args.append(wrt)
        out_specs.append(row(LANES))
        out_shape.append(jax.ShapeDtypeStruct((nb, s, LANES), F32))
    return pl.pallas_call(
        functools.partial(_out_proj_kernel, moe=moe),
        grid=(nb, s // TR),
        in_specs=in_specs, out_specs=out_specs, out_shape=out_shape,
        compiler_params=_cparams(("arbitrary", "arbitrary")),
        name="out_proj_moe" if moe else "out_proj",
    )(*args)


def _ffn_kernel(x_ref, h_ref, mod_ref, wg_ref, wu_ref, wd_ref, o_ref):
    h = h_ref[0]
    gt = _dot(h, wg_ref[...])
    up = _dot(h, wu_ref[...])
    act = (gt * _sigmoid(gt) * up).astype(BF16)
    o_ref[0] = x_ref[0] + mod_ref[0, 5:6] * _dot(act, wd_ref[...])


def _ffn_dense(xa, h, mod, wg, wu, wd, nct):
    nb, s, d = xa.shape
    row = pl.BlockSpec((1, TR, d), lambda b, t: (b, t, 0))
    resident = lambda a: pl.BlockSpec(a.shape, lambda b, t: (0, 0), pipeline_mode=pl.Buffered(1))
    return pl.pallas_call(
        _ffn_kernel,
        grid=(nb, s // TR),
        in_specs=[row, row, pl.BlockSpec((1, N_MOD, d), _mod_index(nb, nct)), resident(wg), resident(wu),
                  resident(wd)],
        out_specs=row,
        out_shape=jax.ShapeDtypeStruct((nb, s, d), F32),
        compiler_params=_cparams(("arbitrary", "arbitrary")),
        name="ffn_dense",
    )(xa, h, mod, wg, wu, wd)


def _row_copy(src_hbm, idx_ref, dst_ref, sem, r):
    return pltpu.make_async_copy(src_hbm.at[pl.ds(idx_ref[0, 0, r], 1), :], dst_ref.at[pl.ds(r, 1), :], sem)


def _moe_kernel(be_ref, nu_ref, tok_ref, tok_next_ref, h_hbm, wg_ref, wu_ref, wd_ref, o_ref, x_buf, sem, *, n_tiles):
    i = pl.program_id(0)
    j = pl.program_id(1)
    nblk = pl.num_programs(0)
    used = i < nu_ref[0]
    slot = i % 2
    cur, nxt = x_buf.at[slot], x_buf.at[1 - slot]
    rows = cur.shape[0]

    def wait_rows(idx_ref, dst, s):
        def body(r, carry):
            _row_copy(h_hbm, idx_ref, dst, s, r).wait()
            return carry
        lax.fori_loop(0, rows, body, 0, unroll=8)

    @pl.when(j == 0)
    def _():
        o_ref[...] = jnp.zeros_like(o_ref)

        @pl.when(i == 0)
        def _():
            def body(r, carry):
                _row_copy(h_hbm, tok_ref, cur, sem.at[slot], r).start()
                return carry
            lax.fori_loop(0, rows, body, 0, unroll=8)

        @pl.when(jnp.logical_or(i == 0, i - 1 < nu_ref[0]))
        def _():
            wait_rows(tok_ref, cur, sem.at[slot])

    def compute():
        x = cur[...].astype(BF16)
        gt = _dot(x, wg_ref[0])
        up = _dot(x, wu_ref[0])
        act = (gt * _sigmoid(gt) * up).astype(BF16)
        o_ref[...] += _dot(act, wd_ref[0])

    share = rows // n_tiles
    for jj in range(n_tiles):
        @pl.when(jnp.logical_and(used, j == jj))
        def _(jj=jj):
            compute()
            for r in range(jj * share, (jj + 1) * share):
                _row_copy(h_hbm, tok_next_ref, nxt, sem.at[1 - slot], r).start()

            if jj == n_tiles - 1:
                @pl.when(i == nblk - 1)
                def _():
                    wait_rows(tok_next_ref, nxt, sem.at[1 - slot])


def _moe_experts(h, slot_token, block_expert, n_used, wg, wu, wd):
    t, d = h.shape
    dff = wg.shape[2]
    nblk = slot_token.shape[0] // MOE_BM
    tok = slot_token.reshape(nblk, 1, MOE_BM)
    grid_spec = pltpu.PrefetchScalarGridSpec(
        num_scalar_prefetch=2,
        grid=(nblk, dff // MOE_TF),
        in_specs=[pl.BlockSpec((1, 1, MOE_BM), lambda i, j, be, nu: (i, 0, 0), memory_space=pltpu.SMEM),
                  pl.BlockSpec((1, 1, MOE_BM), lambda i, j, be, nu: (jnp.minimum(i + 1, nblk - 1), 0, 0),
                               memory_space=pltpu.SMEM),
                  pl.BlockSpec(memory_space=pl.ANY),
                  pl.BlockSpec((1, d, MOE_TF), lambda i, j, be, nu: (be[i], 0, j)),
                  pl.BlockSpec((1, d, MOE_TF), lambda i, j, be, nu: (be[i], 0, j)),
                  pl.BlockSpec((1, MOE_TF, d), lambda i, j, be, nu: (be[i], j, 0))],
        out_specs=pl.BlockSpec((MOE_BM, d), lambda i, j, be, nu: (i, 0)),
        scratch_shapes=[pltpu.VMEM((2, MOE_BM, d), F32), pltpu.SemaphoreType.DMA((2,))],
    )
    return pl.pallas_call(
        functools.partial(_moe_kernel, n_tiles=dff // MOE_TF),
        grid_spec=grid_spec,
        out_shape=jax.ShapeDtypeStruct((nblk * MOE_BM, d), F32),
        compiler_params=_cparams(("arbitrary", "arbitrary")),
        name="moe_experts",
    )(block_expert, n_used, tok, tok, h, wg, wu, wd)


def _rows_loop(src_hbm, idx_ref, dst_ref, sem, op):
    def body(r, carry):
        getattr(_row_copy(src_hbm, idx_ref, dst_ref, sem, r), op)()
        return carry
    lax.fori_loop(0, dst_ref.shape[0], body, 0, unroll=8)


def _combine_kernel(s1_ref, s2_ref, n1_ref, n2_ref, x_ref, route_ref, mod_ref, ys_hbm, o_ref, y_buf, sem):
    n = pl.program_id(0) * pl.num_programs(1) + pl.program_id(1)
    last = pl.num_programs(0) * pl.num_programs(1) - 1
    slot = n % 2
    cur = [(s1_ref, y_buf.at[slot, 0], sem.at[slot, 0]), (s2_ref, y_buf.at[slot, 1], sem.at[slot, 1])]
    nxt = [(n1_ref, y_buf.at[1 - slot, 0], sem.at[1 - slot, 0]), (n2_ref, y_buf.at[1 - slot, 1], sem.at[1 - slot, 1])]

    @pl.when(n == 0)
    def _():
        for idx, dst, s in cur:
            _rows_loop(ys_hbm, idx, dst, s, "start")

    for idx, dst, s in cur:
        _rows_loop(ys_hbm, idx, dst, s, "wait")
    for idx, dst, s in nxt:
        _rows_loop(ys_hbm, idx, dst, s, "start")
    route = route_ref[0]
    f = y_buf[slot, 0] * route[:, 2:3] + y_buf[slot, 1] * route[:, 3:4]
    o_ref[0] = x_ref[0] + mod_ref[0, 5:6] * f

    @pl.when(n == last)
    def _():
        for idx, dst, s in nxt:
            _rows_loop(ys_hbm, idx, dst, s, "wait")


def _moe_combine(xa, route, mod, ys, slot1, slot2, nct, latent_only):
    nb, s, d = xa.shape
    ng = s // TR
    idx = pl.BlockSpec((1, 1, TR), lambda b, t: (b * ng + t, 0, 0), memory_space=pltpu.SMEM)
    idx_next = pl.BlockSpec((1, 1, TR), lambda b, t: (jnp.minimum(b * ng + t + 1, nb * ng - 1), 0, 0),
                            memory_space=pltpu.SMEM)
    row = lambda n: pl.BlockSpec((1, TR, n), lambda b, t: (b, t, 0))
    s1, s2 = slot1.reshape(nb * ng, 1, TR), slot2.reshape(nb * ng, 1, TR)
    return pl.pallas_call(
        _combine_kernel,
        grid=(nb, ng),
        in_specs=[idx, idx, idx_next, idx_next, row(d), row(LANES), pl.BlockSpec((1, N_MOD, d), _mod_index(nb, nct)),
                  pl.BlockSpec(memory_space=pl.ANY)],
        out_specs=pl.BlockSpec((1, TR, d), lambda b, t: (b, jnp.maximum(t - nct, 0), 0)) if latent_only else row(d),
        out_shape=jax.ShapeDtypeStruct((nb, s - nct * TR if latent_only else s, d), F32),
        scratch_shapes=[pltpu.VMEM((2, 2, TR, d), F32), pltpu.SemaphoreType.DMA((2, 2))],
        compiler_params=_cparams(("arbitrary", "arbitrary")),
        name="moe_combine",
    )(s1, s2, s1, s2, xa, route, mod, ys)


def _moe_ffn(xa, h, route, mod, wg, wu, wd, nct, latent_only):
    nb, s, d = xa.shape
    t = nb * s
    n_assign = 2 * t
    nblk = (n_assign + N_EXPERTS * (MOE_BM - 1) + MOE_BM - 1) // MOE_BM
    n_slots = nblk * MOE_BM
    expert = route.reshape(t, LANES)[:, 0:2].astype(jnp.int32).reshape(-1)
    onehot = (expert[:, None] == jnp.arange(N_EXPERTS, dtype=jnp.int32)[None, :]).astype(jnp.int32)
    rank = jnp.sum((jnp.cumsum(onehot, axis=0) - onehot) * onehot, axis=1)
    counts = jnp.sum(onehot, axis=0)
    padded = (counts + MOE_BM - 1) // MOE_BM * MOE_BM
    pad_end = jnp.cumsum(padded)
    pad_start = pad_end - padded
    slot = pad_start[expert] + rank
    token = jnp.repeat(jnp.arange(t, dtype=jnp.int32), 2)
    slot_token = jnp.zeros((n_slots,), jnp.int32).at[slot].set(token)
    block_start = jnp.arange(nblk, dtype=jnp.int32) * MOE_BM
    block_expert = jnp.minimum(jnp.sum((pad_end[None, :] <= block_start[:, None]).astype(jnp.int32), axis=1),
                               N_EXPERTS - 1)
    n_used = (pad_end[-1] // MOE_BM).astype(jnp.int32).reshape(1)
    ys = _moe_experts(h.reshape(t, d), slot_token, block_expert, n_used, wg, wu, wd)
    slot2 = slot.reshape(t, 2)
    return _moe_combine(xa, route, mod, ys, slot2[:, 0], slot2[:, 1], nct, latent_only)


def kernel(x, c, ctx, c_ctx, ada_w, ada_b, norm1_g, norm2_g, w_in, w_out, q_gain, k_gain, shift_mu, rw_w0, rw_w2,
           rw_a0, rw_a2, rw_g2, rw_kk, rw_ka, rw_rk, rw_gn_w, rw_gn_b, ffn_wg, ffn_wu, ffn_wd, moe_router, moe_wg,
           moe_wu, moe_wd):
    nb, l, d = x.shape
    lc = ctx.shape[1]
    depth = ada_w.shape[0]
    assert lc % TR == 0 and l % TR == 0 and l % GRID_W == 0 and TR % ATTN_KT == 0 and TR % CH == 0
    nct = lc // TR
    xa = jnp.concatenate([ctx, x], axis=1)

    mod_rows = -(-(nb + 1) // SUBLANES) * SUBLANES
    c_all = jnp.concatenate([c, c_ctx[None], jnp.zeros((mod_rows - nb - 1, d), F32)], axis=0)
    mod_all = _ada_all(c_all, ada_w, ada_b)[:, :nb + 1].reshape(depth, nb + 1, N_MOD, d)
    cos, slo, shi = _rope_tables(lc, l)

    for i in range(depth):
        mod = mod_all[i]
        lp = dict(mu=shift_mu[i], w0=rw_w0[i], w2=rw_w2[i], a0=rw_a0[i], a2=rw_a2[i], g2=rw_g2[i], kk=rw_kk[i],
                  ka=rw_ka[i], rk=rw_rk[i], gn_w=rw_gn_w[i], gn_b=rw_gn_b[i])
        wi = w_in[i].astype(BF16)
        pa, pr = _in_proj(xa, mod, norm1_g[i], wi[:, :ATTN_IN], wi[:, ATTN_IN:], nct)
        q, k, v = _attn_prep(pa, cos, slo, shi, q_gain[i], k_gain[i])
        att = _attention(q, k, v, nct, lc)
        o_f, bv_f = _rwkv_scan(pr, lp, nct, False)
        o_b, bv_b = _rwkv_scan(pr, lp, nct, True)
        rw = _rwkv_out(o_f, o_b, bv_f, bv_b, pr, lp, nct)
        wo = w_out[i].astype(BF16)
        j = i // 2
        if i % 2 == 0:
            xa, h = _out_proj(xa, att, rw, mod, norm2_g[i], wo[:ATTN_WIDTH], wo[ATTN_WIDTH:], nct)
            xa = _ffn_dense(xa, h, mod, ffn_wg[j].astype(BF16), ffn_wu[j].astype(BF16), ffn_wd[j].astype(BF16), nct)
        else:
            xa, h, route = _out_proj(xa, att, rw, mod, norm2_g[i], wo[:ATTN_WIDTH], wo[ATTN_WIDTH:], nct,
                                     moe_router[j])
            xa = _moe_ffn(xa, h, route, mod, moe_wg[j].astype(BF16), moe_wu[j].astype(BF16), moe_wd[j].astype(BF16),
                          nct, latent_only=i == depth - 1)
    return xa if depth % 2 == 0 else xa[:, lc:]
```
